```python
import math
import jax, jax.numpy as jnp
from jax import lax
import numpy as np

D_MODEL = 1024
BATCH = 2
SEQ = 8192
DEPTH = 1

D_MIX = D_MODEL
HEAD_DIM = 64
ATT_WIDTH = D_MIX // 2
N_Q_HEADS = ATT_WIDTH // HEAD_DIM
N_KV_HEADS = 2
Q_WIDTH = N_Q_HEADS * HEAD_DIM
KV_WIDTH = N_KV_HEADS * HEAD_DIM
HY_WIDTH = D_MIX - ATT_WIDTH
HY_ORDER = 2
IN_PROJ_WIDTH = Q_WIDTH + 2 * KV_WIDTH + (HY_ORDER + 1) * HY_WIDTH
D_FF = 4 * D_MODEL

GRID_W = 64
Q_BLOCK = 128
ROPE_THETA = 10000.0

SHORT_CONV = 3
FILTER_EMB = 33
FILTER_HIDDEN = 64
N_DIRS = 2
N_FILT = HY_ORDER * N_DIRS * HY_WIDTH
SHORT_DECAY_PCT = 0.3
LONG_DECAY_PCT = 1.5
DECAY_TARGET = 1e-2

EPS = 1e-6

kernel_name = 'hymba_attn_hyena_encoder'


def rms_norm(x, g):
    xf = x.astype(jnp.float32)
    y = xf * lax.rsqrt(jnp.mean(xf * xf, axis=-1, keepdims=True) + EPS)
    return (y * g.astype(jnp.float32)).astype(x.dtype)


def axial_rope_tables(seq_len):
    rows = seq_len // GRID_W
    row = jnp.repeat(jnp.arange(rows, dtype=jnp.float32), GRID_W)
    col = jnp.tile(jnp.arange(GRID_W, dtype=jnp.float32), rows)
    half = HEAD_DIM // 2
    inv_freq = ROPE_THETA ** (-jnp.arange(0, half, 2, dtype=jnp.float32) / half)
    ang_r = row[:, None] * inv_freq[None, :]
    ang_c = col[:, None] * inv_freq[None, :]
    return (jnp.cos(ang_r), jnp.sin(ang_r), jnp.cos(ang_c), jnp.sin(ang_c))


def _rotate(x, cos, sin):
    x1, x2 = jnp.split(x, 2, axis=-1)
    cos = cos[None, :, None, :]
    sin = sin[None, :, None, :]
    return jnp.concatenate([x1 * cos - x2 * sin, x2 * cos + x1 * sin], axis=-1)


def apply_axial_rope(x, cos_r, sin_r, cos_c, sin_c):
    xf = x.astype(jnp.float32)
    half = HEAD_DIM // 2
    out = jnp.concatenate([_rotate(xf[..., :half], cos_r, sin_r),
                           _rotate(xf[..., half:], cos_c, sin_c)], axis=-1)
    return out.astype(x.dtype)


def block_attention(q, k, v):
    b, seq_len = q.shape[0], q.shape[1]
    n_blocks = seq_len // Q_BLOCK
    group = N_Q_HEADS // N_KV_HEADS
    scale = HEAD_DIM ** -0.5
    qb = q.reshape(b, n_blocks, Q_BLOCK, N_KV_HEADS, group, HEAD_DIM).transpose(1, 0, 2, 3, 4, 5)

    def one_block(q_blk):
        s = jnp.einsum('bqkgd,bskd->bkgqs', q_blk, k, preferred_element_type=jnp.float32) * scale
        p = jax.nn.softmax(s, axis=-1).astype(v.dtype)
        return jnp.einsum('bkgqs,bskd->bqkgd', p, v)

    o = lax.map(one_block, qb)
    return o.transpose(1, 0, 2, 3, 4, 5).reshape(b, seq_len, Q_WIDTH)


def short_conv(u, w, bias):
    c = u.shape[-1]
    y = lax.conv_general_dilated(u, w[:, None, :], window_strides=(1,), padding=[(1, 1)],
                                 dimension_numbers=('NWC', 'WIO', 'NWC'), feature_group_count=c)
    return y + bias


def hyena_filters_freq(seq_len, w1, b1, w2, b2, w3, b3, w4, freq, deltas):
    f32 = jnp.float32
    w1, b1, w2, b2, w3, b3, w4, freq, deltas = [a.astype(f32) for a in (w1, b1, w2, b2, w3, b3, w4, freq, deltas)]
    t = jnp.linspace(0.0, 1.0, seq_len, dtype=f32)[:, None]
    bands = (FILTER_EMB - 1) // 2
    w_ang = 2.0 * math.pi * jnp.arange(seq_len, dtype=f32) / seq_len
    band_f = jnp.linspace(1e-4, bands - 1, bands, dtype=f32)
    ang = w_ang[:, None] * band_f[None, :]
    z = jnp.concatenate([t, jnp.cos(ang), -jnp.sin(ang)], axis=-1)
    h = jnp.sin(freq * (z @ w1 + b1))
    h = jnp.sin(freq * (h @ w2 + b2))
    h = jnp.sin(freq * (h @ w3 + b3))
    h = h @ w4
    h = h * jnp.exp(-t * jnp.abs(deltas)[None, :])
    h = h.reshape(seq_len, HY_ORDER, N_DIRS, HY_WIDTH)
    fwd = h[:, :, 0]
    bwd = h[:, :, 1]
    taps = jnp.concatenate([fwd, jnp.zeros((1, HY_ORDER, HY_WIDTH), f32), bwd[:0:-1]], axis=0)
    taps = taps / jnp.sum(jnp.abs(taps), axis=0, keepdims=True)
    return jnp.fft.rfft(taps, axis=0)


def fft_long_conv(z, taps_f):
    seq_len = z.shape[1]
    zf = jnp.fft.rfft(z.astype(jnp.float32), n=2 * seq_len, axis=1)
    return jnp.fft.irfft(zf * taps_f[None], n=2 * seq_len, axis=1)[:, :seq_len]


def hyena_mixer(u, conv_w, conv_b, w1, b1, w2, b2, w3, b3, w4, freq, deltas, skip_d):
    seq_len = u.shape[1]
    uc = short_conv(u, conv_w, conv_b)
    v, x1, x2 = jnp.split(uc, HY_ORDER + 1, axis=-1)
    taps_f = hyena_filters_freq(seq_len, w1, b1, w2, b2, w3, b3, w4, freq, deltas)
    zz = v.astype(jnp.float32)
    gates = (x1, x2)
    for n in range(HY_ORDER):
        conv = fft_long_conv(zz, taps_f[:, n]) + skip_d[n].astype(jnp.float32) * zz
        zz = gates[n].astype(jnp.float32) * conv
    return zz.astype(u.dtype)


def setup_inputs(seed: int = 0) -> dict:
    key = jax.random.key(seed)
    ks = jax.random.split(key, 24)
    f32 = jnp.float32

    def nrm(k, shape, scale):
        return jax.random.normal(k, shape, f32) * scale

    def gain(k, shape):
        return 1.0 + 0.02 * jax.random.normal(k, shape, f32)

    min_decay = math.log(DECAY_TARGET) / LONG_DECAY_PCT
    max_decay = math.log(DECAY_TARGET) / SHORT_DECAY_PCT
    base = jnp.tile(jnp.linspace(min_decay, max_decay, HY_WIDTH, dtype=f32), HY_ORDER * N_DIRS)

    return {
        'x': jax.random.normal(ks[0], (BATCH, SEQ, D_MODEL), f32),
        'norm1_g': gain(ks[1], (DEPTH, D_MODEL)),
        'w_in': nrm(ks[2], (DEPTH, D_MODEL, IN_PROJ_WIDTH), D_MODEL ** -0.5),
        'q_norm_g': gain(ks[3], (DEPTH, HEAD_DIM)),
        'k_norm_g': gain(ks[4], (DEPTH, HEAD_DIM)),
        'hy_conv_w': nrm(ks[5], (DEPTH, SHORT_CONV, (HY_ORDER + 1) * HY_WIDTH), SHORT_CONV ** -0.5),
        'hy_conv_b': nrm(ks[6], (DEPTH, (HY_ORDER + 1) * HY_WIDTH), 0.02),
        'filt_w1': nrm(ks[7], (DEPTH, FILTER_EMB, FILTER_HIDDEN), FILTER_EMB ** -0.5),
        'filt_b1': nrm(ks[8], (DEPTH, FILTER_HIDDEN), 0.02),
        'filt_w2': nrm(ks[9], (DEPTH, FILTER_HIDDEN, FILTER_HIDDEN), FILTER_HIDDEN ** -0.5),
        'filt_b2': nrm(ks[10], (DEPTH, FILTER_HIDDEN), 0.02),
        'filt_w3': nrm(ks[11], (DEPTH, FILTER_HIDDEN, FILTER_HIDDEN), FILTER_HIDDEN ** -0.5),
        'filt_b3': nrm(ks[12], (DEPTH, FILTER_HIDDEN), 0.02),
        'filt_w4': nrm(ks[13], (DEPTH, FILTER_HIDDEN, N_FILT), FILTER_HIDDEN ** -0.5),
        'filt_freq': gain(ks[14], (DEPTH, FILTER_HIDDEN)),
        'filt_deltas': base[None, :] + 0.05 * jax.random.normal(ks[15], (DEPTH, N_FILT), f32),
        'hy_skip_d': nrm(ks[16], (DEPTH, HY_ORDER, HY_WIDTH), 0.1),
        'attn_out_g': gain(ks[17], (DEPTH, ATT_WIDTH)),
        'hy_out_g': gain(ks[18], (DEPTH, HY_WIDTH)),
        'w_out': nrm(ks[19], (DEPTH, D_MIX, D_MODEL), D_MIX ** -0.5),
        'norm2_g': gain(ks[20], (DEPTH, D_MODEL)),
        'w_mlp_in': nrm(ks[21], (DEPTH, D_MODEL, D_FF), D_MODEL ** -0.5),
        'w_mlp_out': nrm(ks[22], (DEPTH, D_FF, D_MODEL), D_FF ** -0.5),
        'final_g': gain(ks[23], (D_MODEL,)),
    }


def reference(x, norm1_g, w_in, q_norm_g, k_norm_g, hy_conv_w, hy_conv_b,
              filt_w1, filt_b1, filt_w2, filt_b2, filt_w3, filt_b3, filt_w4,
              filt_freq, filt_deltas, hy_skip_d, attn_out_g, hy_out_g, w_out,
              norm2_g, w_mlp_in, w_mlp_out, final_g):
    b, seq_len = x.shape[0], x.shape[1]
    cos_r, sin_r, cos_c, sin_c = axial_rope_tables(seq_len)
    h = x
    for i in range(DEPTH):
        a = rms_norm(h, norm1_g[i])
        proj = a @ w_in[i]
        q, k, v, u = jnp.split(proj, [Q_WIDTH, Q_WIDTH + KV_WIDTH, Q_WIDTH + 2 * KV_WIDTH], axis=-1)
        q = rms_norm(q.reshape(b, seq_len, N_Q_HEADS, HEAD_DIM), q_norm_g[i])
        k = rms_norm(k.reshape(b, seq_len, N_KV_HEADS, HEAD_DIM), k_norm_g[i])
        v = v.reshape(b, seq_len, N_KV_HEADS, HEAD_DIM)
        q = apply_axial_rope(q, cos_r, sin_r, cos_c, sin_c)
        k = apply_axial_rope(k, cos_r, sin_r, cos_c, sin_c)
        att = block_attention(q, k, v)
        hy = hyena_mixer(u, hy_conv_w[i], hy_conv_b[i], filt_w1[i], filt_b1[i], filt_w2[i],
                         filt_b2[i], filt_w3[i], filt_b3[i], filt_w4[i], filt_freq[i],
                         filt_deltas[i], hy_skip_d[i])
        mix = jnp.concatenate([rms_norm(att, attn_out_g[i]), rms_norm(hy, hy_out_g[i])], axis=-1)
        h = h + mix @ w_out[i]
        m = rms_norm(h, norm2_g[i])
        h = h + jnp.square(jax.nn.relu(m @ w_mlp_in[i])) @ w_mlp_out[i]
    return rms_norm(h, final_g)
```

```python
import functools
import math

import jax
import jax.numpy as jnp
from jax import lax
from jax.experimental import pallas as pl
from jax.experimental.pallas import tpu as pltpu

F32 = jnp.float32
BF16 = jnp.bfloat16

HEAD_DIM = 64
N_Q_HEADS = 8
N_KV_HEADS = 2
Q_WIDTH = N_Q_HEADS * HEAD_DIM
KV_WIDTH = N_KV_HEADS * HEAD_DIM
HY_WIDTH = 512
GRID_W = 64
ROPE_THETA = 10000.0
FILTER_EMB = 33
FILTER_PAD = 64
EPS = 1e-6

LANES = 128
HALO = 16
VMEM_LIMIT = 56 * 1024 * 1024

ROW_TILE = 512
Q_TILE = 128
KV_CHUNK = 512
FFT_COLS = 4096
MID_K1 = 4
FILT_TILE = 512


def _dot(a, b):
    return jnp.dot(a, b, preferred_element_type=F32)


def _dot3(a, b):
    ah = a.astype(BF16)
    al = (a - ah.astype(F32)).astype(BF16)
    bh = b.astype(BF16)
    bl = (b - bh.astype(F32)).astype(BF16)
    return _dot(ah, bh) + (_dot(ah, bl) + _dot(al, bh))


def _rms(v, g):
    return v * lax.rsqrt(jnp.mean(v * v, axis=-1, keepdims=True) + EPS) * g


def _params(*sem):
    return pltpu.CompilerParams(dimension_semantics=sem, vmem_limit_bytes=VMEM_LIMIT)


def _in_proj_kernel(x_ref, xp_ref, xn_ref, g1_ref, wqkv_ref, wu_ref, qg_ref, kg_ref, cos_ref, sin_ref,
                    bdq_ref, bdk_ref, cw_ref, cb_ref,
                    q_ref, k_ref, v_ref, hv_ref, x1_ref, x2_ref, a_scr, *, tm, n_tiles):
    i = pl.program_id(1)
    g1 = g1_ref[...]
    a_scr[HALO:HALO + tm, :] = _rms(x_ref[...], g1).astype(BF16)
    prev = jnp.where(i > 0, _rms(xp_ref[...], g1), 0.0)
    nxt = jnp.where(i < n_tiles - 1, _rms(xn_ref[...], g1), 0.0)
    a_scr[0:HALO, :] = prev.astype(BF16)
    a_scr[HALO + tm:, :] = nxt.astype(BF16)

    qkv = _dot(a_scr[HALO:HALO + tm, :], wqkv_ref[...])
    cos = cos_ref[...]
    sin = sin_ref[...]
    lane = lax.broadcasted_iota(jnp.int32, (tm, LANES), 1)
    first16 = (lane % 32) < 16
    low_half = lane < HEAD_DIM

    def norm_rope(blk, ms, gain):
        y = blk * lax.rsqrt(ms + EPS) * gain
        partner = jnp.where(first16, pltpu.roll(y, LANES - 16, 1), pltpu.roll(y, 16, 1))
        return y * cos + partner * sin

    q = qkv[:, :Q_WIDTH]
    q_ms = _dot((q * q).astype(BF16), bdq_ref[...])
    qg = qg_ref[...]
    for m in range(Q_WIDTH // LANES):
        sl = slice(m * LANES, (m + 1) * LANES)
        blk = norm_rope(q[:, sl], q_ms[:, sl], qg[:, sl]) * (HEAD_DIM ** -0.5)
        q_ref[:, 2 * m * LANES:(2 * m + 1) * LANES] = blk.astype(BF16)
        q_ref[:, (2 * m + 1) * LANES:(2 * m + 2) * LANES] = pltpu.roll(blk, HEAD_DIM, 1).astype(BF16)

    k = qkv[:, Q_WIDTH:Q_WIDTH + KV_WIDTH]
    k_ms = _dot((k * k).astype(BF16), bdk_ref[...])
    k = norm_rope(k, k_ms, kg_ref[...])
    v = qkv[:, Q_WIDTH + KV_WIDTH:]
    k_ref[0] = jnp.where(low_half, k, 0.0).astype(BF16)
    k_ref[1] = jnp.where(low_half, pltpu.roll(k, HEAD_DIM, 1), 0.0).astype(BF16)
    v_ref[0] = jnp.where(low_half, v, 0.0).astype(BF16)
    v_ref[1] = jnp.where(low_half, pltpu.roll(v, HEAD_DIM, 1), 0.0).astype(BF16)

    rows = tm + 2 * HALO
    for c, out in enumerate((hv_ref, x1_ref, x2_ref)):
        cs = slice(c * HY_WIDTH, (c + 1) * HY_WIDTH)
        u = _dot(a_scr[...], wu_ref[:, cs])
        w = cw_ref[:, cs]
        y = (pltpu.roll(u, 1, 0) * w[0:1] + u * w[1:2] + pltpu.roll(u, rows - 1, 0) * w[2:3]) + cb_ref[:, cs]
        out[...] = y[HALO:HALO + tm]


def _in_proj(x, g1, w_qkv, w_u, qg, kg, cos_t, sin_t, bdq, bdk, conv_w, conv_b):
    b, seq, d = x.shape
    tm = min(ROW_TILE, seq)
    n_tiles = seq // tm
    hb = tm // HALO
    full = lambda shape: pl.BlockSpec(shape, lambda bi, i: (0,) * len(shape))
    row_out = lambda w: pl.BlockSpec((None, tm, w), lambda bi, i: (bi, i, 0))
    kv_out = pl.BlockSpec((None, N_KV_HEADS, tm, LANES), lambda bi, i: (bi, 0, i, 0))
    return pl.pallas_call(
        functools.partial(_in_proj_kernel, tm=tm, n_tiles=n_tiles),
        grid=(b, n_tiles),
        in_specs=[
            pl.BlockSpec((None, tm, d), lambda bi, i: (bi, i, 0)),
            pl.BlockSpec((None, HALO, d), lambda bi, i: (bi, jnp.maximum(i * hb - 1, 0), 0)),
            pl.BlockSpec((None, HALO, d), lambda bi, i: (bi, jnp.minimum((i + 1) * hb, seq // HALO - 1), 0)),
            full((1, d)),
            full(w_qkv.shape),
            full(w_u.shape),
            full((1, Q_WIDTH)),
            full((1, KV_WIDTH)),
            pl.BlockSpec((tm, LANES), lambda bi, i: (i, 0)),
            pl.BlockSpec((tm, LANES), lambda bi, i: (i, 0)),
            full(bdq.shape),
            full(bdk.shape),
            full(conv_w.shape),
            full(conv_b.shape),
        ],
        out_specs=[row_out(2 * Q_WIDTH), kv_out, kv_out, row_out(HY_WIDTH), row_out(HY_WIDTH), row_out(HY_WIDTH)],
        out_shape=[
            jax.ShapeDtypeStruct((b, seq, 2 * Q_WIDTH), BF16),
            jax.ShapeDtypeStruct((b, N_KV_HEADS, seq, LANES), BF16),
            jax.ShapeDtypeStruct((b, N_KV_HEADS, seq, LANES), BF16),
            jax.ShapeDtypeStruct((b, seq, HY_WIDTH), F32),
            jax.ShapeDtypeStruct((b, seq, HY_WIDTH), F32),
            jax.ShapeDtypeStruct((b, seq, HY_WIDTH), F32),
        ],
        scratch_shapes=[pltpu.VMEM((tm + 2 * HALO, d), BF16)],
        compiler_params=_params("parallel", "arbitrary"),
        name="in_proj",
    )(x, x, x, g1, w_qkv, w_u, qg, kg, cos_t, sin_t, bdq, bdk, conv_w, conv_b)


def _attn_kernel(q_ref, k_ref, v_ref, o_ref, *, tq, sc, n_chunks):
    group = N_Q_HEADS // N_KV_HEADS
    q = jnp.concatenate([q_ref[:, g * LANES:(g + 1) * LANES] for g in range(group)], axis=0)
    rows = group * tq

    def body(c, carry):
        m, l, acc = carry
        off = pl.multiple_of(c * sc, sc)
        k = k_ref[pl.ds(off, sc), :]
        v = v_ref[pl.ds(off, sc), :]
        s = lax.dot_general(q, k, (((1,), (1,)), ((), ())), preferred_element_type=F32)
        m_new = jnp.maximum(m, jnp.max(s, axis=1, keepdims=True))
        p = jnp.exp(s - m_new)
        alpha = jnp.exp(m - m_new)
        l = alpha * l + jnp.sum(p, axis=1, keepdims=True)
        acc = alpha * acc + _dot(p.astype(BF16), v)
        return m_new, l, acc

    init = (jnp.full((rows, 1), jnp.finfo(F32).min, F32), jnp.zeros((rows, 1), F32), jnp.zeros((rows, LANES), F32))
    _, l, acc = lax.fori_loop(0, n_chunks, body, init)
    o = acc / l
    pairs = [o[(2 * p) * tq:(2 * p + 1) * tq] + pltpu.roll(o[(2 * p + 1) * tq:(2 * p + 2) * tq], HEAD_DIM, 1)
             for p in range(group // 2)]
    o_ref[...] = jnp.concatenate(pairs, axis=1)


def _attention(q, k, v):
    b, seq, _ = q.shape
    tq = min(Q_TILE, seq)
    sc = min(KV_CHUNK, seq)
    group = N_Q_HEADS // N_KV_HEADS
    return pl.pallas_call(
        functools.partial(_attn_kernel, tq=tq, sc=sc, n_chunks=seq // sc),
        grid=(b, N_KV_HEADS, seq // tq),
        in_specs=[
            pl.BlockSpec((None, tq, group * LANES), lambda bi, j, i: (bi, i, j)),
            pl.BlockSpec((None, None, seq, LANES), lambda bi, j, i: (bi, j, 0, 0)),
            pl.BlockSpec((None, None, seq, LANES), lambda bi, j, i: (bi, j, 0, 0)),
        ],
        out_specs=pl.BlockSpec((None, tq, group * HEAD_DIM), lambda bi, j, i: (bi, i, j)),
        out_shape=jax.ShapeDtypeStruct((b, seq, Q_WIDTH), F32),
        compiler_params=_params("parallel", "parallel", "arbitrary"),
        name="attention",
    )(q, k, v)


def _filter_kernel(z_ref, w1_ref, b1_ref, w2_ref, b2_ref, w3_ref, b3_ref, w4_ref, fr_ref, dl_ref,
                   taps_ref, s_ref, *, tl):
    i = pl.program_id(0)
    z = z_ref[...]
    fr = fr_ref[...]
    h = jnp.sin(fr * (_dot3(z, w1_ref[...]) + b1_ref[...]))
    h = jnp.sin(fr * (_dot3(h, w2_ref[...]) + b2_ref[...]))
    h = jnp.sin(fr * (_dot3(h, w3_ref[...]) + b3_ref[...]))
    h = _dot3(h, w4_ref[...])
    t = z[:, 0:1]
    h = h * jnp.exp(-t * jnp.abs(dl_ref[...]))
    row = i * tl + lax.broadcasted_iota(jnp.int32, (tl, HY_WIDTH), 0)
    sums = []
    for j in range(4):
        hj = h[:, j * HY_WIDTH:(j + 1) * HY_WIDTH]
        if j % 2 == 1:
            hj = jnp.where(row == 0, 0.0, hj)
        taps_ref[j] = hj
        sums.append(jnp.sum(jnp.abs(hj), axis=0, keepdims=True))
    ssum = jnp.concatenate(sums, axis=1)

    @pl.when(i == 0)
    def _():
        s_ref[...] = ssum

    @pl.when(i > 0)
    def _():
        s_ref[...] += ssum


def _filter_taps(zf, w1, b1, w2, b2, w3, b3, w4, freq, deltas):
    seq = zf.shape[0]
    tl = min(FILT_TILE, seq)
    n_filt = w4.shape[1]
    full = lambda a: pl.BlockSpec(a.shape, lambda i: (0,) * a.ndim)
    return pl.pallas_call(
        functools.partial(_filter_kernel, tl=tl),
        grid=(seq // tl,),
        in_specs=[pl.BlockSpec((tl, FILTER_PAD), lambda i: (i, 0))] + [full(a) for a in (w1, b1, w2, b2, w3, b3, w4, freq, deltas)],
        out_specs=[pl.BlockSpec((4, tl, HY_WIDTH), lambda i: (0, i, 0)), pl.BlockSpec((1, n_filt), lambda i: (0, 0))],
        out_shape=[jax.ShapeDtypeStruct((4, seq, HY_WIDTH), F32), jax.ShapeDtypeStruct((1, n_filt), F32)],
        compiler_params=_params("arbitrary"),
        name="filter_taps",
    )(zf, w1, b1, w2, b2, w3, b3, w4, freq, deltas)


def _fft_outer_complex_kernel(xr_ref, xi_ref, f_ref, ar_ref, ai_ref, *, n1):
    x = jnp.concatenate([xr_ref[...], xi_ref[...]], axis=0).astype(BF16)
    y = _dot(f_ref[...], x)
    ar_ref[...] = y[:n1].astype(BF16)
    ai_ref[...] = y[n1:].astype(BF16)


def _fft_outer_complex(z, f1):
    _, nh, cols = z.shape
    n1 = f1.shape[0] // 2
    tn = min(FFT_COLS, cols)
    out = pl.BlockSpec((n1, tn), lambda j: (0, j))
    return pl.pallas_call(
        functools.partial(_fft_outer_complex_kernel, n1=n1),
        grid=(cols // tn,),
        in_specs=[pl.BlockSpec((None, nh, tn), lambda j: (0, 0, j)),
                  pl.BlockSpec((None, nh, tn), lambda j: (1, 0, j)),
                  pl.BlockSpec(f1.shape, lambda j: (0, 0))],
        out_specs=[out, out],
        out_shape=[jax.ShapeDtypeStruct((n1, cols), BF16)] * 2,
        compiler_params=_params("parallel"),
        name="fft_outer",
    )(z, z, f1)


def _fft_outer_real_kernel(x_ref, f_ref, ar_ref, ai_ref, *, n1):
    y = _dot(f_ref[...], x_ref[...].astype(BF16))
    ar_ref[...] = y[:n1].astype(BF16)
    ai_ref[...] = y[n1:].astype(BF16)


def _fft_outer_real(taps, f1r):
    n_sig, nh, cols = taps.shape
    n1 = f1r.shape[0] // 2
    tn = min(FFT_COLS, cols)
    out = pl.BlockSpec((None, n1, tn), lambda s, j: (s, 0, j))
    return pl.pallas_call(
        functools.partial(_fft_outer_real_kernel, n1=n1),
        grid=(n_sig, cols // tn),
        in_specs=[pl.BlockSpec((None, nh, tn), lambda s, j: (s, 0, j)),
                  pl.BlockSpec(f1r.shape, lambda s, j: (0, 0))],
        out_specs=[out, out],
        out_shape=[jax.ShapeDtypeStruct((n_sig, n1, cols), BF16)] * 2,
        compiler_params=_params("parallel", "parallel"),
        name="fft_outer_filter",
    )(taps, f1r)


def _filter_spectrum_kernel(ar_ref, ai_ref, gf_ref, s_ref, hr_ref, hi_ref, *, kb, n_fft):
    scale = 1.0 / ((s_ref[0] + s_ref[1]) * n_fft)
    for kk in range(kb):
        g = gf_ref[kk]
        cf = _dot(g, jnp.concatenate([ar_ref[0, kk], ai_ref[0, kk]], axis=0))
        cb = _dot(g, jnp.concatenate([ar_ref[1, kk], ai_ref[1, kk]], axis=0))
        hr_ref[kk] = (cf[:LANES] + cb[:LANES]) * scale
        hi_ref[kk] = (cf[LANES:] - cb[LANES:]) * scale


def _filter_spectrum(ar, ai, gf, s, n_fft):
    n_ord, _, n1, _, c = ar.shape
    kb = min(MID_K1, n1)
    a_spec = pl.BlockSpec((None, 2, kb, LANES, c), lambda o, k: (o, 0, k, 0, 0))
    h_spec = pl.BlockSpec((None, kb, LANES, c), lambda o, k: (o, k, 0, 0))
    return pl.pallas_call(
        functools.partial(_filter_spectrum_kernel, kb=kb, n_fft=n_fft),
        grid=(n_ord, n1 // kb),
        in_specs=[a_spec, a_spec,
                  pl.BlockSpec((kb, 2 * LANES, 2 * LANES), lambda o, k: (k, 0, 0)),
                  pl.BlockSpec((None, 2, 1, c), lambda o, k: (o, 0, 0, 0))],
        out_specs=[h_spec, h_spec],
        out_shape=[jax.ShapeDtypeStruct((n_ord, n1, LANES, c), F32)] * 2,
        compiler_params=_params("parallel", "parallel"),
        name="filter_spectrum",
    )(ar, ai, gf, s)


def _fft_inner_kernel(ar_ref, ai_ref, gf_ref, gi_ref, hr_ref, hi_ref, br_ref, bi_ref, *, kb):
    for kk in range(kb):
        x = jnp.concatenate([ar_ref[kk], ai_ref[kk]], axis=0)
        c = _dot(gf_ref[kk], x)
        cr, ci = c[:LANES], c[LANES:]
        hr, hi = hr_ref[kk], hi_ref[kk]
        p = jnp.concatenate([cr * hr - ci * hi, cr * hi + ci * hr], axis=0).astype(BF16)
        y = _dot(gi_ref[kk], p)
        br_ref[kk] = y[:LANES].astype(BF16)
        bi_ref[kk] = y[LANES:].astype(BF16)


def _fft_inner(ar, ai, gf, gi, hr, hi, order):
    n1, _, c = ar.shape
    kb = min(MID_K1, n1)
    a_spec = pl.BlockSpec((kb, LANES, c), lambda k: (k, 0, 0))
    g_spec = pl.BlockSpec((kb, 2 * LANES, 2 * LANES), lambda k: (k, 0, 0))
    h_spec = pl.BlockSpec((None, kb, LANES, c), lambda k: (order, k, 0, 0))
    return pl.pallas_call(
        functools.partial(_fft_inner_kernel, kb=kb),
        grid=(n1 // kb,),
        in_specs=[a_spec, a_spec, g_spec, g_spec, h_spec, h_spec],
        out_specs=[a_spec, a_spec],
        out_shape=[jax.ShapeDtypeStruct((n1, LANES, c), BF16)] * 2,
        compiler_params=_params("parallel"),
        name="fft_inner",
    )(ar, ai, gf, gi, hr, hi)


def _ifft_outer_gate_kernel(br_ref, bi_ref, f_ref, zr_ref, zi_ref, gr_ref, gi_ref, skip_ref, o_ref, *, nh):
    y = _dot(f_ref[...], jnp.concatenate([br_ref[...], bi_ref[...]], axis=0))
    skip = skip_ref[...]
    o_ref[0] = gr_ref[...] * (y[:nh] + skip * zr_ref[...])
    o_ref[1] = gi_ref[...] * (y[nh:] + skip * zi_ref[...])


def _ifft_outer_gate(br, bi, f1inv, z, gate, skip_row):
    _, nh, cols = z.shape
    n1 = br.shape[0]
    tn = min(FFT_COLS, cols)
    b_spec = pl.BlockSpec((n1, tn), lambda j: (0, j))
    re = pl.BlockSpec((None, nh, tn), lambda j: (0, 0, j))
    im = pl.BlockSpec((None, nh, tn), lambda j: (1, 0, j))
    return pl.pallas_call(
        functools.partial(_ifft_outer_gate_kernel, nh=nh),
        grid=(cols // tn,),
        in_specs=[b_spec, b_spec, pl.BlockSpec(f1inv.shape, lambda j: (0, 0)), re, im, re, im,
                  pl.BlockSpec((1, tn), lambda j: (0, 0))],
        out_specs=pl.BlockSpec((2, nh, tn), lambda j: (0, 0, j)),
        out_shape=jax.ShapeDtypeStruct((2, nh, cols), F32),
        compiler_params=_params("parallel"),
        name="ifft_outer_gate",
    )(br, bi, f1inv, z, z, gate, gate, skip_row)


def _dft_tables(seq):
    n = 2 * seq
    n1 = n // LANES
    nh = n1 // 2
    k1 = jnp.arange(n1, dtype=jnp.int32)
    ang = ((k1[:, None] * jnp.arange(nh, dtype=jnp.int32)[None, :]) % n1).astype(F32) * (2.0 * math.pi / n1)
    fc, fs = jnp.cos(ang), jnp.sin(ang)
    f1 = jnp.concatenate([jnp.concatenate([fc, fs], 1), jnp.concatenate([-fs, fc], 1)], 0).astype(BF16)
    f1r = jnp.concatenate([fc, -fs], 0).astype(BF16)
    f1inv = jnp.concatenate([jnp.concatenate([fc.T, -fs.T], 1), jnp.concatenate([fs.T, fc.T], 1)], 0).astype(BF16)
    k2 = jnp.arange(LANES, dtype=jnp.int32)
    k = k1[:, None, None] + n1 * k2[None, :, None]
    th = ((k * k2[None, None, :]) % n).astype(F32) * (2.0 * math.pi / n)
    c, s = jnp.cos(th), jnp.sin(th)
    gf = jnp.concatenate([jnp.concatenate([c, s], 2), jnp.concatenate([-s, c], 2)], 1).astype(BF16)
    gi = jnp.swapaxes(gf, 1, 2)
    return f1, f1r, f1inv, gf, gi


def _filter_features(seq):
    t = jnp.linspace(0.0, 1.0, seq, dtype=F32)[:, None]
    bands = (FILTER_EMB - 1) // 2
    w_ang = 2.0 * math.pi * jnp.arange(seq, dtype=F32) / seq
    band_f = jnp.linspace(1e-4, bands - 1, bands, dtype=F32)
    ang = w_ang[:, None] * band_f[None, :]
    z = jnp.concatenate([t, jnp.cos(ang), -jnp.sin(ang)], axis=-1)
    return jnp.pad(z, ((0, 0), (0, FILTER_PAD - FILTER_EMB)))


def _hyena(hv, x1, x2, fw1, fb1, fw2, fb2, fw3, fb3, fw4, ffreq, fdeltas, skip_d):
    b, seq, c = hv.shape
    assert b == 2, "the two batch rows ride as the real and imaginary parts of one transform"
    n = 2 * seq
    n1 = n // LANES
    nh = n1 // 2
    cols = LANES * c
    f1, f1r, f1inv, gf, gi = _dft_tables(seq)

    zf = _filter_features(seq)
    w1p = jnp.pad(fw1, ((0, FILTER_PAD - FILTER_EMB), (0, 0)))
    taps, s = _filter_taps(zf, w1p, fb1[None], fw2, fb2[None], fw3, fb3[None], fw4, ffreq[None], fdeltas[None])
    tr, ti = _fft_outer_real(taps.reshape(4, nh, cols), f1r)
    hr, hi = _filter_spectrum(tr.reshape(2, 2, n1, LANES, c), ti.reshape(2, 2, n1, LANES, c), gf,
                              s.reshape(2, 2, 1, c), n)

    tn = min(FFT_COLS, cols)
    zz = hv.reshape(2, nh, cols)
    for order, gate in enumerate((x1, x2)):
        ar, ai = _fft_outer_complex(zz, f1)
        br, bi = _fft_inner(ar.reshape(n1, LANES, c), ai.reshape(n1, LANES, c), gf, gi, hr, hi, order)
        skip_row = jnp.tile(skip_d[order], tn // c)[None]
        zz = _ifft_outer_gate(br.reshape(n1, cols), bi.reshape(n1, cols), f1inv, zz, gate.reshape(2, nh, cols), skip_row)
    return zz.reshape(b, seq, c)


def _out_mlp_kernel(x_ref, att_ref, hy_ref, ga_ref, gh_ref, wo_ref, g2_ref, w1_ref, w2_ref, gf_ref, o_ref):
    a = _rms(att_ref[...], ga_ref[...]).astype(BF16)
    hyn = _rms(hy_ref[...], gh_ref[...]).astype(BF16)
    half = a.shape[1]
    h = x_ref[...] + _dot(a, wo_ref[:half, :]) + _dot(hyn, wo_ref[half:, :])
    m = _rms(h, g2_ref[...]).astype(BF16)
    t = jnp.square(jnp.maximum(_dot(m, w1_ref[...]), 0.0)).astype(BF16)
    h = h + _dot(t, w2_ref[...])
    o_ref[...] = _rms(h, gf_ref[...])


def _out_mlp(x, att, hy, ga, gh, w_out, g2, w1, w2, gfin):
    b, seq, d = x.shape
    tm = min(ROW_TILE, seq)
    row = lambda w: pl.BlockSpec((None, tm, w), lambda bi, i: (bi, i, 0))
    full = lambda a: pl.BlockSpec(a.shape, lambda bi, i: (0,) * a.ndim, pipeline_mode=pl.Buffered(1))
    return pl.pallas_call(
        _out_mlp_kernel,
        grid=(b, seq // tm),
        in_specs=[row(d), row(att.shape[-1]), row(hy.shape[-1]), full(ga), full(gh), full(w_out), full(g2),
                  full(w1), full(w2), full(gfin)],
        out_specs=row(d),
        out_shape=jax.ShapeDtypeStruct((b, seq, d), F32),
        compiler_params=_params("parallel", "parallel"),
        name="out_mlp",
    )(x, att, hy, ga, gh, w_out, g2, w1, w2, gfin)


def _rope_tables(seq):
    rows = seq // GRID_W
    row = jnp.repeat(jnp.arange(rows, dtype=F32), GRID_W)
    col = jnp.tile(jnp.arange(GRID_W, dtype=F32), rows)
    half = HEAD_DIM // 2
    inv_freq = ROPE_THETA ** (-jnp.arange(0, half, 2, dtype=F32) / half)
    ang_r = row[:, None] * inv_freq[None, :]
    ang_c = col[:, None] * inv_freq[None, :]
    cos = jnp.concatenate([jnp.cos(ang_r)] * 2 + [jnp.cos(ang_c)] * 2, axis=1)
    sin = jnp.concatenate([-jnp.sin(ang_r), jnp.sin(ang_r), -jnp.sin(ang_c), jnp.sin(ang_c)], axis=1)
    return jnp.tile(cos, (1, LANES // HEAD_DIM)), jnp.tile(sin, (1, LANES // HEAD_DIM))


def _head_mean_matrix(width):
    head = jnp.arange(width, dtype=jnp.int32) // HEAD_DIM
    return jnp.where(head[:, None] == head[None, :], 1.0 / HEAD_DIM, 0.0).astype(BF16)


def kernel(x, norm1_g, w_in, q_norm_g, k_norm_g, hy_conv_w, hy_conv_b, filt_w1, filt_b1, filt_w2, filt_b2, filt_w3, filt_b3, filt_w4, filt_freq, filt_deltas, hy_skip_d, attn_out_g, hy_out_g, w_out, norm2_g, w_mlp_in, w_mlp_out, final_g):
    seq = x.shape[1]
    cos_t, sin_t = _rope_tables(seq)
    bdq = _head_mean_matrix(Q_WIDTH)
    bdk = _head_mean_matrix(KV_WIDTH)
    h = x
    for i in range(norm1_g.shape[0]):
        w_qkv = w_in[i][:, :Q_WIDTH + 2 * KV_WIDTH].astype(BF16)
        w_u = w_in[i][:, Q_WIDTH + 2 * KV_WIDTH:].astype(BF16)
        q, k, v, hv, x1, x2 = _in_proj(
            h, norm1_g[i][None], w_qkv, w_u, jnp.tile(q_norm_g[i], N_Q_HEADS)[None],
            jnp.tile(k_norm_g[i], N_KV_HEADS)[None], cos_t, sin_t, bdq, bdk, hy_conv_w[i], hy_conv_b[i][None])
        att = _attention(q, k, v)
        hy = _hyena(hv, x1, x2, filt_w1[i], filt_b1[i], filt_w2[i], filt_b2[i], filt_w3[i], filt_b3[i],
                    filt_w4[i], filt_freq[i], filt_deltas[i], hy_skip_d[i])
        last = i == norm1_g.shape[0] - 1
        assert last, "single-layer trunk"
        h = _out_mlp(h, att, hy, attn_out_g[i][None], hy_out_g[i][None], w_out[i].astype(BF16), norm2_g[i][None],
                     w_mlp_in[i].astype(BF16), w_mlp_out[i].astype(BF16), final_g[None])
    return h
```

```python
import functools
import math

import jax
import jax.numpy as jnp
from jax import lax
from jax.experimental import pallas as pl
from jax.experimental.pallas import tpu as pltpu

F32 = jnp.float32
BF16 = jnp.bfloat16

HEAD_DIM = 64
N_Q_HEADS = 8
N_KV_HEADS = 2
Q_WIDTH = N_Q_HEADS * HEAD_DIM
KV_WIDTH = N_KV_HEADS * HEAD_DIM
HY_WIDTH = 512
GRID_W = 64
ROPE_THETA = 10000.0
FILTER_EMB = 33
FILTER_PAD = 64
EPS = 1e-6
LOG2E = math.log2(math.e)
MAX_FIXED_SHIFT = 50.0

LANES = 128
HALO = 16
VMEM_LIMIT = 56 * 1024 * 1024

ROW_TILE = 512
Q_TILE = 128
KV_CHUNK = 512
KV_CHUNK_BOUNDED = 2048
FFT_COLS = 4096
MID_K1 = 4
FILT_TILE = 512


def _dot(a, b):
    return jnp.dot(a, b, preferred_element_type=F32)


def _dot3(a, b):
    ah = a.astype(BF16)
    al = (a - ah.astype(F32)).astype(BF16)
    bh = b.astype(BF16)
    bl = (b - bh.astype(F32)).astype(BF16)
    return _dot(ah, bh) + (_dot(ah, bl) + _dot(al, bh))


def _rms(v, g):
    return v * lax.rsqrt(jnp.mean(v * v, axis=-1, keepdims=True) + EPS) * g


def _params(*sem):
    return pltpu.CompilerParams(dimension_semantics=sem, vmem_limit_bytes=VMEM_LIMIT)


def _in_proj_kernel(x_ref, xp_ref, xn_ref, g1_ref, wqkv_ref, wu_ref, qg_ref, kg_ref, cos_ref, sin_ref,
                    bdq_ref, bdk_ref, cw_ref, cb_ref, bound_ref,
                    q_ref, k_ref, v_ref, hv_ref, x1_ref, x2_ref, a_scr, *, tm, n_tiles):
    i = pl.program_id(1)
    g1 = g1_ref[...]
    a_scr[HALO:HALO + tm, :] = _rms(x_ref[...], g1).astype(BF16)
    prev = jnp.where(i > 0, _rms(xp_ref[...], g1), 0.0)
    nxt = jnp.where(i < n_tiles - 1, _rms(xn_ref[...], g1), 0.0)
    a_scr[0:HALO, :] = prev.astype(BF16)
    a_scr[HALO + tm:, :] = nxt.astype(BF16)

    qkv = _dot(a_scr[HALO:HALO + tm, :], wqkv_ref[...])
    cos = cos_ref[...]
    sin = sin_ref[...]
    lane = lax.broadcasted_iota(jnp.int32, (tm, LANES), 1)
    first16 = (lane % 32) < 16
    low_half = lane < HEAD_DIM

    def norm_rope(blk, ms, gain):
        y = blk * lax.rsqrt(ms + EPS) * gain
        partner = jnp.where(first16, pltpu.roll(y, LANES - 16, 1), pltpu.roll(y, 16, 1))
        return y * cos + partner * sin

    q = qkv[:, :Q_WIDTH]
    q_ms = _dot((q * q).astype(BF16), bdq_ref[...])
    qg = qg_ref[...]
    neg_bound = -bound_ref[0, 0]
    extra = lane == HEAD_DIM
    for m in range(Q_WIDTH // LANES):
        sl = slice(m * LANES, (m + 1) * LANES)
        blk = norm_rope(q[:, sl], q_ms[:, sl], qg[:, sl]) * (HEAD_DIM ** -0.5 * LOG2E)
        q_ref[:, 2 * m * LANES:(2 * m + 1) * LANES] = jnp.where(extra, neg_bound, blk).astype(BF16)
        q_ref[:, (2 * m + 1) * LANES:(2 * m + 2) * LANES] = jnp.where(
            extra, neg_bound, pltpu.roll(blk, HEAD_DIM, 1)).astype(BF16)

    k = qkv[:, Q_WIDTH:Q_WIDTH + KV_WIDTH]
    k_ms = _dot((k * k).astype(BF16), bdk_ref[...])
    k = norm_rope(k, k_ms, kg_ref[...])
    v = qkv[:, Q_WIDTH + KV_WIDTH:]
    pad = jnp.where(extra, 1.0, 0.0)
    k_ref[0] = jnp.where(low_half, k, pad).astype(BF16)
    k_ref[1] = jnp.where(low_half, pltpu.roll(k, HEAD_DIM, 1), pad).astype(BF16)
    v_ref[0] = jnp.where(low_half, v, pad).astype(BF16)
    v_ref[1] = jnp.where(low_half, pltpu.roll(v, HEAD_DIM, 1), pad).astype(BF16)

    rows = tm + 2 * HALO
    for c, out in enumerate((hv_ref, x1_ref, x2_ref)):
        cs = slice(c * HY_WIDTH, (c + 1) * HY_WIDTH)
        u = _dot(a_scr[...], wu_ref[:, cs])
        w = cw_ref[:, cs]
        y = (pltpu.roll(u, 1, 0) * w[0:1] + u * w[1:2] + pltpu.roll(u, rows - 1, 0) * w[2:3]) + cb_ref[:, cs]
        out[...] = y[HALO:HALO + tm]


def _in_proj(x, g1, w_qkv, w_u, qg, kg, cos_t, sin_t, bdq, bdk, conv_w, conv_b, bound):
    b, seq, d = x.shape
    tm = min(ROW_TILE, seq)
    n_tiles = seq // tm
    hb = tm // HALO
    full = lambda shape: pl.BlockSpec(shape, lambda bi, i: (0,) * len(shape))
    row_out = lambda w: pl.BlockSpec((None, tm, w), lambda bi, i: (bi, i, 0))
    kv_out = pl.BlockSpec((None, N_KV_HEADS, tm, LANES), lambda bi, i: (bi, 0, i, 0))
    return pl.pallas_call(
        functools.partial(_in_proj_kernel, tm=tm, n_tiles=n_tiles),
        grid=(b, n_tiles),
        in_specs=[
            pl.BlockSpec((None, tm, d), lambda bi, i: (bi, i, 0)),
            pl.BlockSpec((None, HALO, d), lambda bi, i: (bi, jnp.maximum(i * hb - 1, 0), 0)),
            pl.BlockSpec((None, HALO, d), lambda bi, i: (bi, jnp.minimum((i + 1) * hb, seq // HALO - 1), 0)),
            full((1, d)),
            full(w_qkv.shape),
            full(w_u.shape),
            full((1, Q_WIDTH)),
            full((1, KV_WIDTH)),
            pl.BlockSpec((tm, LANES), lambda bi, i: (i, 0)),
            pl.BlockSpec((tm, LANES), lambda bi, i: (i, 0)),
            full(bdq.shape),
            full(bdk.shape),
            full(conv_w.shape),
            full(conv_b.shape),
            pl.BlockSpec(memory_space=pltpu.SMEM),
        ],
        out_specs=[row_out(2 * Q_WIDTH), kv_out, kv_out, row_out(HY_WIDTH), row_out(HY_WIDTH), row_out(HY_WIDTH)],
        out_shape=[
            jax.ShapeDtypeStruct((b, seq, 2 * Q_WIDTH), BF16),
            jax.ShapeDtypeStruct((b, N_KV_HEADS, seq, LANES), BF16),
            jax.ShapeDtypeStruct((b, N_KV_HEADS, seq, LANES), BF16),
            jax.ShapeDtypeStruct((b, seq, HY_WIDTH), F32),
            jax.ShapeDtypeStruct((b, seq, HY_WIDTH), F32),
            jax.ShapeDtypeStruct((b, seq, HY_WIDTH), F32),
        ],
        scratch_shapes=[pltpu.VMEM((tm + 2 * HALO, d), BF16)],
        compiler_params=_params("parallel", "arbitrary"),
        name="in_proj",
    )(x, x, x, g1, w_qkv, w_u, qg, kg, cos_t, sin_t, bdq, bdk, conv_w, conv_b, bound)


def _stack_heads(q_ref):
    group = N_Q_HEADS // N_KV_HEADS
    return jnp.concatenate([q_ref[:, g * LANES:(g + 1) * LANES] for g in range(group)], axis=0)


def _store_heads(o_ref, o, tq):
    group = N_Q_HEADS // N_KV_HEADS
    pairs = [o[(2 * p) * tq:(2 * p + 1) * tq] + pltpu.roll(o[(2 * p + 1) * tq:(2 * p + 2) * tq], HEAD_DIM, 1)
             for p in range(group // 2)]
    o_ref[...] = jnp.concatenate(pairs, axis=1)


def _attn_bounded_kernel(q_ref, k_ref, v_ref, o_ref, *, tq, sc, n_chunks):
    q = _stack_heads(q_ref)
    rows = q.shape[0]

    def body(c, acc):
        off = pl.multiple_of(c * sc, sc)
        s = lax.dot_general(q, k_ref[pl.ds(off, sc), :], (((1,), (1,)), ((), ())), preferred_element_type=F32)
        return acc + _dot(jnp.exp2(s).astype(BF16), v_ref[pl.ds(off, sc), :])

    acc = lax.fori_loop(0, n_chunks, body, jnp.zeros((rows, LANES), F32))
    lane = lax.broadcasted_iota(jnp.int32, (rows, LANES), 1)
    _store_heads(o_ref, jnp.where(lane < HEAD_DIM, acc / acc[:, HEAD_DIM:HEAD_DIM + 1], 0.0), tq)


def _attn_online_kernel(q_ref, k_ref, v_ref, o_ref, *, tq, sc, n_chunks):
    q = _stack_heads(q_ref)
    rows = q.shape[0]

    def body(c, carry):
        m, l, acc = carry
        off = pl.multiple_of(c * sc, sc)
        s = lax.dot_general(q, k_ref[pl.ds(off, sc), :], (((1,), (1,)), ((), ())), preferred_element_type=F32)
        m_new = jnp.maximum(m, jnp.max(s, axis=1, keepdims=True))
        p = jnp.exp2(s - m_new)
        alpha = jnp.exp2(m - m_new)
        l = alpha * l + jnp.sum(p, axis=1, keepdims=True)
        acc = alpha * acc + _dot(p.astype(BF16), v_ref[pl.ds(off, sc), :])
        return m_new, l, acc

    init = (jnp.full((rows, 1), jnp.finfo(F32).min, F32), jnp.zeros((rows, 1), F32), jnp.zeros((rows, LANES), F32))
    _, l, acc = lax.fori_loop(0, n_chunks, body, init)
    lane = lax.broadcasted_iota(jnp.int32, (rows, LANES), 1)
    _store_heads(o_ref, jnp.where(lane < HEAD_DIM, acc / l, 0.0), tq)


def _attention(body, sc, name, q, k, v):
    b, seq, _ = q.shape
    tq = min(Q_TILE, seq)
    sc = min(sc, seq)
    group = N_Q_HEADS // N_KV_HEADS
    return pl.pallas_call(
        functools.partial(body, tq=tq, sc=sc, n_chunks=seq // sc),
        grid=(b, N_KV_HEADS, seq // tq),
        in_specs=[
            pl.BlockSpec((None, tq, group * LANES), lambda bi, j, i: (bi, i, j)),
            pl.BlockSpec((None, None, seq, LANES), lambda bi, j, i: (bi, j, 0, 0)),
            pl.BlockSpec((None, None, seq, LANES), lambda bi, j, i: (bi, j, 0, 0)),
        ],
        out_specs=pl.BlockSpec((None, tq, group * HEAD_DIM), lambda bi, j, i: (bi, i, j)),
        out_shape=jax.ShapeDtypeStruct((b, seq, Q_WIDTH), F32),
        compiler_params=_params("parallel", "parallel", "arbitrary"),
        name=name,
    )(q, k, v)


def _filter_kernel(z_ref, w1_ref, b1_ref, w2_ref, b2_ref, w3_ref, b3_ref, w4_ref, fr_ref, dl_ref,
                   taps_ref, s_ref, *, tl):
    i = pl.program_id(0)
    z = z_ref[...]
    fr = fr_ref[...]
    h = jnp.sin(fr * (_dot3(z, w1_ref[...]) + b1_ref[...]))
    h = jnp.sin(fr * (_dot3(h, w2_ref[...]) + b2_ref[...]))
    h = jnp.sin(fr * (_dot3(h, w3_ref[...]) + b3_ref[...]))
    h = _dot3(h, w4_ref[...])
    t = z[:, 0:1]
    h = h * jnp.exp(-t * jnp.abs(dl_ref[...]))
    row = i * tl + lax.broadcasted_iota(jnp.int32, (tl, HY_WIDTH), 0)
    sums = []
    for j in range(4):
        hj = h[:, j * HY_WIDTH:(j + 1) * HY_WIDTH]
        if j % 2 == 1:
            hj = jnp.where(row == 0, 0.0, hj)
        taps_ref[j] = hj
        sums.append(jnp.sum(jnp.abs(hj), axis=0, keepdims=True))
    ssum = jnp.concatenate(sums, axis=1)

    @pl.when(i == 0)
    def _():
        s_ref[...] = ssum

    @pl.when(i > 0)
    def _():
        s_ref[...] += ssum


def _filter_taps(zf, w1, b1, w2, b2, w3, b3, w4, freq, deltas):
    seq = zf.shape[0]
    tl = min(FILT_TILE, seq)
    n_filt = w4.shape[1]
    full = lambda a: pl.BlockSpec(a.shape, lambda i: (0,) * a.ndim)
    return pl.pallas_call(
        functools.partial(_filter_kernel, tl=tl),
        grid=(seq // tl,),
        in_specs=[pl.BlockSpec((tl, FILTER_PAD), lambda i: (i, 0))] + [full(a) for a in (w1, b1, w2, b2, w3, b3, w4, freq, deltas)],
        out_specs=[pl.BlockSpec((4, tl, HY_WIDTH), lambda i: (0, i, 0)), pl.BlockSpec((1, n_filt), lambda i: (0, 0))],
        out_shape=[jax.ShapeDtypeStruct((4, seq, HY_WIDTH), F32), jax.ShapeDtypeStruct((1, n_filt), F32)],
        compiler_params=_params("arbitrary"),
        name="filter_taps",
    )(zf, w1, b1, w2, b2, w3, b3, w4, freq, deltas)


def _fft_outer_complex_kernel(xr_ref, xi_ref, f_ref, ar_ref, ai_ref, *, n1):
    x = jnp.concatenate([xr_ref[...], xi_ref[...]], axis=0).astype(BF16)
    y = _dot(f_ref[...], x)
    ar_ref[...] = y[:n1].astype(BF16)
    ai_ref[...] = y[n1:].astype(BF16)


def _fft_outer_complex(z, f1):
    _, nh, cols = z.shape
    n1 = f1.shape[0] // 2
    tn = min(FFT_COLS, cols)
    out = pl.BlockSpec((n1, tn), lambda j: (0, j))
    return pl.pallas_call(
        functools.partial(_fft_outer_complex_kernel, n1=n1),
        grid=(cols // tn,),
        in_specs=[pl.BlockSpec((None, nh, tn), lambda j: (0, 0, j)),
                  pl.BlockSpec((None, nh, tn), lambda j: (1, 0, j)),
                  pl.BlockSpec(f1.shape, lambda j: (0, 0))],
        out_specs=[out, out],
        out_shape=[jax.ShapeDtypeStruct((n1, cols), BF16)] * 2,
        compiler_params=_params("parallel"),
        name="fft_outer",
    )(z, z, f1)


def _fft_outer_real_kernel(x_ref, f_ref, ar_ref, ai_ref, *, n1):
    y = _dot(f_ref[...], x_ref[...].astype(BF16))
    ar_ref[...] = y[:n1].astype(BF16)
    ai_ref[...] = y[n1:].astype(BF16)


def _fft_outer_real(taps, f1r):
    n_sig, nh, cols = taps.shape
    n1 = f1r.shape[0] // 2
    tn = min(FFT_COLS, cols)
    out = pl.BlockSpec((None, n1, tn), lambda s, j: (s, 0, j))
    return pl.pallas_call(
        functools.partial(_fft_outer_real_kernel, n1=n1),
        grid=(n_sig, cols // tn),
        in_specs=[pl.BlockSpec((None, nh, tn), lambda s, j: (s, 0, j)),
                  pl.BlockSpec(f1r.shape, lambda s, j: (0, 0))],
        out_specs=[out, out],
        out_shape=[jax.ShapeDtypeStruct((n_sig, n1, cols), BF16)] * 2,
        compiler_params=_params("parallel", "parallel"),
        name="fft_outer_filter",
    )(taps, f1r)


def _filter_spectrum_kernel(ar_ref, ai_ref, gf_ref, s_ref, hr_ref, hi_ref, *, kb, n_fft):
    scale = 1.0 / ((s_ref[0] + s_ref[1]) * n_fft)
    for kk in range(kb):
        g = gf_ref[kk]
        cf = _dot(g, jnp.concatenate([ar_ref[0, kk], ai_ref[0, kk]], axis=0))
        cb = _dot(g, jnp.concatenate([ar_ref[1, kk], ai_ref[1, kk]], axis=0))
        hr_ref[kk] = (cf[:LANES] + cb[:LANES]) * scale
        hi_ref[kk] = (cf[LANES:] - cb[LANES:]) * scale


def _filter_spectrum(ar, ai, gf, s, n_fft):
    n_ord, _, n1, _, c = ar.shape
    kb = min(MID_K1, n1)
    a_spec = pl.BlockSpec((None, 2, kb, LANES, c), lambda o, k: (o, 0, k, 0, 0))
    h_spec = pl.BlockSpec((None, kb, LANES, c), lambda o, k: (o, k, 0, 0))
    return pl.pallas_call(
        functools.partial(_filter_spectrum_kernel, kb=kb, n_fft=n_fft),
        grid=(n_ord, n1 // kb),
        in_specs=[a_spec, a_spec,
                  pl.BlockSpec((kb, 2 * LANES, 2 * LANES), lambda o, k: (k, 0, 0)),
                  pl.BlockSpec((None, 2, 1, c), lambda o, k: (o, 0, 0, 0))],
        out_specs=[h_spec, h_spec],
        out_shape=[jax.ShapeDtypeStruct((n_ord, n1, LANES, c), F32)] * 2,
        compiler_params=_params("parallel", "parallel"),
        name="filter_spectrum",
    )(ar, ai, gf, s)


def _fft_inner_kernel(ar_ref, ai_ref, gf_ref, gi_ref, hr_ref, hi_ref, br_ref, bi_ref, *, kb):
    for kk in range(kb):
        x = jnp.concatenate([ar_ref[kk], ai_ref[kk]], axis=0)
        c = _dot(gf_ref[kk], x)
        cr, ci = c[:LANES], c[LANES:]
        hr, hi = hr_ref[kk], hi_ref[kk]
        p = jnp.concatenate([cr * hr - ci * hi, cr * hi + ci * hr], axis=0).astype(BF16)
        y = _dot(gi_ref[kk], p)
        br_ref[kk] = y[:LANES].astype(BF16)
        bi_ref[kk] = y[LANES:].astype(BF16)


def _fft_inner(ar, ai, gf, gi, hr, hi, order):
    n1, _, c = ar.shape
    kb = min(MID_K1, n1)
    a_spec = pl.BlockSpec((kb, LANES, c), lambda k: (k, 0, 0))
    g_spec = pl.BlockSpec((kb, 2 * LANES, 2 * LANES), lambda k: (k, 0, 0))
    h_spec = pl.BlockSpec((None, kb, LANES, c), lambda k: (order, k, 0, 0))
    return pl.pallas_call(
        functools.partial(_fft_inner_kernel, kb=kb),
        grid=(n1 // kb,),
        in_specs=[a_spec, a_spec, g_spec, g_spec, h_spec, h_spec],
        out_specs=[a_spec, a_spec],
        out_shape=[jax.ShapeDtypeStruct((n1, LANES, c), BF16)] * 2,
        compiler_params=_params("parallel"),
        name="fft_inner",
    )(ar, ai, gf, gi, hr, hi)


def _ifft_outer_gate_kernel(br_ref, bi_ref, f_ref, zr_ref, zi_ref, gr_ref, gi_ref, skip_ref, o_ref, *, nh):
    y = _dot(f_ref[...], jnp.concatenate([br_ref[...], bi_ref[...]], axis=0))
    skip = skip_ref[...]
    o_ref[0] = gr_ref[...] * (y[:nh] + skip * zr_ref[...])
    o_ref[1] = gi_ref[...] * (y[nh:] + skip * zi_ref[...])


def _ifft_outer_gate(br, bi, f1inv, z, gate, skip_row):
    _, nh, cols = z.shape
    n1 = br.shape[0]
    tn = min(FFT_COLS, cols)
    b_spec = pl.BlockSpec((n1, tn), lambda j: (0, j))
    re = pl.BlockSpec((None, nh, tn), lambda j: (0, 0, j))
    im = pl.BlockSpec((None, nh, tn), lambda j: (1, 0, j))
    return pl.pallas_call(
        functools.partial(_ifft_outer_gate_kernel, nh=nh),
        grid=(cols // tn,),
        in_specs=[b_spec, b_spec, pl.BlockSpec(f1inv.shape, lambda j: (0, 0)), re, im, re, im,
                  pl.BlockSpec((1, tn), lambda j: (0, 0))],
        out_specs=pl.BlockSpec((2, nh, tn), lambda j: (0, 0, j)),
        out_shape=jax.ShapeDtypeStruct((2, nh, cols), F32),
        compiler_params=_params("parallel"),
        name="ifft_outer_gate",
    )(br, bi, f1inv, z, z, gate, gate, skip_row)


def _dft_tables(seq):
    n = 2 * seq
    n1 = n // LANES
    nh = n1 // 2
    k1 = jnp.arange(n1, dtype=jnp.int32)
    ang = ((k1[:, None] * jnp.arange(nh, dtype=jnp.int32)[None, :]) % n1).astype(F32) * (2.0 * math.pi / n1)
    fc, fs = jnp.cos(ang), jnp.sin(ang)
    f1 = jnp.concatenate([jnp.concatenate([fc, fs], 1), jnp.concatenate([-fs, fc], 1)], 0).astype(BF16)
    f1r = jnp.concatenate([fc, -fs], 0).astype(BF16)
    f1inv = jnp.concatenate([jnp.concatenate([fc.T, -fs.T], 1), jnp.concatenate([fs.T, fc.T], 1)], 0).astype(BF16)
    k2 = jnp.arange(LANES, dtype=jnp.int32)
    k = k1[:, None, None] + n1 * k2[None, :, None]
    th = ((k * k2[None, None, :]) % n).astype(F32) * (2.0 * math.pi / n)
    c, s = jnp.cos(th), jnp.sin(th)
    gf = jnp.concatenate([jnp.concatenate([c, s], 2), jnp.concatenate([-s, c], 2)], 1).astype(BF16)
    gi = jnp.swapaxes(gf, 1, 2)
    return f1, f1r, f1inv, gf, gi


def _filter_features(seq):
    t = jnp.linspace(0.0, 1.0, seq, dtype=F32)[:, None]
    bands = (FILTER_EMB - 1) // 2
    w_ang = 2.0 * math.pi * jnp.arange(seq, dtype=F32) / seq
    band_f = jnp.linspace(1e-4, bands - 1, bands, dtype=F32)
    ang = w_ang[:, None] * band_f[None, :]
    z = jnp.concatenate([t, jnp.cos(ang), -jnp.sin(ang)], axis=-1)
    return jnp.pad(z, ((0, 0), (0, FILTER_PAD - FILTER_EMB)))


def _hyena(hv, x1, x2, fw1, fb1, fw2, fb2, fw3, fb3, fw4, ffreq, fdeltas, skip_d):
    b, seq, c = hv.shape
    assert b == 2, "the two batch rows ride as the real and imaginary parts of one transform"
    n = 2 * seq
    n1 = n // LANES
    nh = n1 // 2
    cols = LANES * c
    f1, f1r, f1inv, gf, gi = _dft_tables(seq)

    zf = _filter_features(seq)
    w1p = jnp.pad(fw1, ((0, FILTER_PAD - FILTER_EMB), (0, 0)))
    taps, s = _filter_taps(zf, w1p, fb1[None], fw2, fb2[None], fw3, fb3[None], fw4, ffreq[None], fdeltas[None])
    tr, ti = _fft_outer_real(taps.reshape(4, nh, cols), f1r)
    hr, hi = _filter_spectrum(tr.reshape(2, 2, n1, LANES, c), ti.reshape(2, 2, n1, LANES, c), gf,
                              s.reshape(2, 2, 1, c), n)

    tn = min(FFT_COLS, cols)
    zz = hv.reshape(2, nh, cols)
    for order, gate in enumerate((x1, x2)):
        ar, ai = _fft_outer_complex(zz, f1)
        br, bi = _fft_inner(ar.reshape(n1, LANES, c), ai.reshape(n1, LANES, c), gf, gi, hr, hi, order)
        skip_row = jnp.tile(skip_d[order], tn // c)[None]
        zz = _ifft_outer_gate(br.reshape(n1, cols), bi.reshape(n1, cols), f1inv, zz, gate.reshape(2, nh, cols), skip_row)
    return zz.reshape(b, seq, c)


def _out_mlp_kernel(x_ref, att_ref, hy_ref, ga_ref, gh_ref, wo_ref, g2_ref, w1_ref, w2_ref, gf_ref, o_ref):
    a = _rms(att_ref[...], ga_ref[...]).astype(BF16)
    hyn = _rms(hy_ref[...], gh_ref[...]).astype(BF16)
    half = a.shape[1]
    h = x_ref[...] + _dot(a, wo_ref[:half, :]) + _dot(hyn, wo_ref[half:, :])
    m = _rms(h, g2_ref[...]).astype(BF16)
    t = jnp.square(jnp.maximum(_dot(m, w1_ref[...]), 0.0)).astype(BF16)
    h = h + _dot(t, w2_ref[...])
    o_ref[...] = _rms(h, gf_ref[...])


def _out_mlp(x, att, hy, ga, gh, w_out, g2, w1, w2, gfin):
    b, seq, d = x.shape
    tm = min(ROW_TILE, seq)
    row = lambda w: pl.BlockSpec((None, tm, w), lambda bi, i: (bi, i, 0))
    full = lambda a: pl.BlockSpec(a.shape, lambda bi, i: (0,) * a.ndim, pipeline_mode=pl.Buffered(1))
    return pl.pallas_call(
        _out_mlp_kernel,
        grid=(b, seq // tm),
        in_specs=[row(d), row(att.shape[-1]), row(hy.shape[-1]), full(ga), full(gh), full(w_out), full(g2),
                  full(w1), full(w2), full(gfin)],
        out_specs=row(d),
        out_shape=jax.ShapeDtypeStruct((b, seq, d), F32),
        compiler_params=_params("parallel", "parallel"),
        name="out_mlp",
    )(x, att, hy, ga, gh, w_out, g2, w1, w2, gfin)


def _rope_tables(seq):
    rows = seq // GRID_W
    row = jnp.repeat(jnp.arange(rows, dtype=F32), GRID_W)
    col = jnp.tile(jnp.arange(GRID_W, dtype=F32), rows)
    half = HEAD_DIM // 2
    inv_freq = ROPE_THETA ** (-jnp.arange(0, half, 2, dtype=F32) / half)
    ang_r = row[:, None] * inv_freq[None, :]
    ang_c = col[:, None] * inv_freq[None, :]
    cos = jnp.concatenate([jnp.cos(ang_r)] * 2 + [jnp.cos(ang_c)] * 2, axis=1)
    sin = jnp.concatenate([-jnp.sin(ang_r), jnp.sin(ang_r), -jnp.sin(ang_c), jnp.sin(ang_c)], axis=1)
    return jnp.tile(cos, (1, LANES // HEAD_DIM)), jnp.tile(sin, (1, LANES // HEAD_DIM))


def _head_mean_matrix(width):
    head = jnp.arange(width, dtype=jnp.int32) // HEAD_DIM
    return jnp.where(head[:, None] == head[None, :], 1.0 / HEAD_DIM, 0.0).astype(BF16)


def kernel(x, norm1_g, w_in, q_norm_g, k_norm_g, hy_conv_w, hy_conv_b, filt_w1, filt_b1, filt_w2, filt_b2, filt_w3, filt_b3, filt_w4, filt_freq, filt_deltas, hy_skip_d, attn_out_g, hy_out_g, w_out, norm2_g, w_mlp_in, w_mlp_out, final_g):
    seq = x.shape[1]
    cos_t, sin_t = _rope_tables(seq)
    bdq = _head_mean_matrix(Q_WIDTH)
    bdk = _head_mean_matrix(KV_WIDTH)
    h = x
    for i in range(norm1_g.shape[0]):
        w_qkv = w_in[i][:, :Q_WIDTH + 2 * KV_WIDTH].astype(BF16)
        w_u = w_in[i][:, Q_WIDTH + 2 * KV_WIDTH:].astype(BF16)
        bound = (1.01 * HEAD_DIM ** 0.5 * LOG2E) * jnp.max(jnp.abs(q_norm_g[i])) * jnp.max(jnp.abs(k_norm_g[i]))
        q, k, v, hv, x1, x2 = _in_proj(
            h, norm1_g[i][None], w_qkv, w_u, jnp.tile(q_norm_g[i], N_Q_HEADS)[None],
            jnp.tile(k_norm_g[i], N_KV_HEADS)[None], cos_t, sin_t, bdq, bdk, hy_conv_w[i], hy_conv_b[i][None],
            bound.reshape(1, 1))
        att = lax.cond(
            bound <= MAX_FIXED_SHIFT,
            functools.partial(_attention, _attn_bounded_kernel, KV_CHUNK_BOUNDED, "attention_bounded"),
            functools.partial(_attention, _attn_online_kernel, KV_CHUNK, "attention_online"),
            q, k, v)
        hy = _hyena(hv, x1, x2, filt_w1[i], filt_b1[i], filt_w2[i], filt_b2[i], filt_w3[i], filt_b3[i],
                    filt_w4[i], filt_freq[i], filt_deltas[i], hy_skip_d[i])
        last = i == norm1_g.shape[0] - 1
        assert last, "single-layer trunk"
        h = _out_mlp(h, att, hy, attn_out_g[i][None], hy_out_g[i][None], w_out[i].astype(BF16), norm2_g[i][None],
                     w_mlp_in[i].astype(BF16), w_mlp_out[i].astype(BF16), final_g[None])
    return h
```

```python
import functools
import math

import jax
import jax.numpy as jnp
from jax import lax
from jax.experimental import pallas as pl
from jax.experimental.pallas import tpu as pltpu

F32 = jnp.float32
BF16 = jnp.bfloat16

HEAD_DIM = 64
N_Q_HEADS = 8
N_KV_HEADS = 2
Q_WIDTH = N_Q_HEADS * HEAD_DIM
KV_WIDTH = N_KV_HEADS * HEAD_DIM
HY_WIDTH = 512
GRID_W = 64
ROPE_THETA = 10000.0
FILTER_EMB = 33
FILTER_PAD = 64
EPS = 1e-6
LOG2E = math.log2(math.e)
MAX_FIXED_SHIFT = 50.0

LANES = 128
HALO = 16
VMEM_LIMIT = 56 * 1024 * 1024

ROW_TILE = 512
Q_TILE = 128
KV_CHUNK = 512
KV_CHUNK_BOUNDED = 8192
FFT_COLS = 4096
MID_K1 = 4
FILT_TILE = 512


def _dot(a, b):
    return jnp.dot(a, b, preferred_element_type=F32)


def _dot3(a, b):
    ah = a.astype(BF16)
    al = (a - ah.astype(F32)).astype(BF16)
    bh = b.astype(BF16)
    bl = (b - bh.astype(F32)).astype(BF16)
    return _dot(ah, bh) + (_dot(ah, bl) + _dot(al, bh))


def _rms(v, g):
    return v * lax.rsqrt(jnp.mean(v * v, axis=-1, keepdims=True) + EPS) * g


def _params(*sem):
    return pltpu.CompilerParams(dimension_semantics=sem, vmem_limit_bytes=VMEM_LIMIT)


def _in_proj_kernel(x_ref, xp_ref, xn_ref, g1_ref, wqkv_ref, wu_ref, qg_ref, kg_ref, cos_ref, sin_ref,
                    bdq_ref, bdk_ref, cw_ref, cb_ref, bound_ref,
                    q_ref, k_ref, v_ref, hv_ref, x1_ref, x2_ref, a_scr, *, tm, n_tiles):
    i = pl.program_id(1)
    g1 = g1_ref[...]
    a_scr[HALO:HALO + tm, :] = _rms(x_ref[...], g1).astype(BF16)
    prev = jnp.where(i > 0, _rms(xp_ref[...], g1), 0.0)
    nxt = jnp.where(i < n_tiles - 1, _rms(xn_ref[...], g1), 0.0)
    a_scr[0:HALO, :] = prev.astype(BF16)
    a_scr[HALO + tm:, :] = nxt.astype(BF16)

    qkv = _dot(a_scr[HALO:HALO + tm, :], wqkv_ref[...])
    cos = cos_ref[...]
    sin = sin_ref[...]
    lane = lax.broadcasted_iota(jnp.int32, (tm, LANES), 1)
    first16 = (lane % 32) < 16
    low_half = lane < HEAD_DIM

    def norm_rope(blk, ms, gain):
        y = blk * lax.rsqrt(ms + EPS) * gain
        partner = jnp.where(first16, pltpu.roll(y, LANES - 16, 1), pltpu.roll(y, 16, 1))
        return y * cos + partner * sin

    q = qkv[:, :Q_WIDTH]
    q_ms = _dot((q * q).astype(BF16), bdq_ref[...])
    qg = qg_ref[...]
    neg_bound = -bound_ref[0, 0]
    extra = lane == HEAD_DIM
    for m in range(Q_WIDTH // LANES):
        sl = slice(m * LANES, (m + 1) * LANES)
        blk = norm_rope(q[:, sl], q_ms[:, sl], qg[:, sl]) * (HEAD_DIM ** -0.5 * LOG2E)
        q_ref[:, 2 * m * LANES:(2 * m + 1) * LANES] = jnp.where(extra, neg_bound, blk).astype(BF16)
        q_ref[:, (2 * m + 1) * LANES:(2 * m + 2) * LANES] = jnp.where(
            extra, neg_bound, pltpu.roll(blk, HEAD_DIM, 1)).astype(BF16)

    k = qkv[:, Q_WIDTH:Q_WIDTH + KV_WIDTH]
    k_ms = _dot((k * k).astype(BF16), bdk_ref[...])
    k = norm_rope(k, k_ms, kg_ref[...])
    v = qkv[:, Q_WIDTH + KV_WIDTH:]
    pad = jnp.where(extra, 1.0, 0.0)
    k_ref[0] = jnp.where(low_half, k, pad).astype(BF16)
    k_ref[1] = jnp.where(low_half, pltpu.roll(k, HEAD_DIM, 1), pad).astype(BF16)
    v_ref[0] = jnp.where(low_half, v, pad).astype(BF16)
    v_ref[1] = jnp.where(low_half, pltpu.roll(v, HEAD_DIM, 1), pad).astype(BF16)

    rows = tm + 2 * HALO
    for c, out in enumerate((hv_ref, x1_ref, x2_ref)):
        cs = slice(c * HY_WIDTH, (c + 1) * HY_WIDTH)
        u = _dot(a_scr[...], wu_ref[:, cs])
        w = cw_ref[:, cs]
        y = (pltpu.roll(u, 1, 0) * w[0:1] + u * w[1:2] + pltpu.roll(u, rows - 1, 0) * w[2:3]) + cb_ref[:, cs]
        out[...] = y[HALO:HALO + tm]


def _in_proj(x, g1, w_qkv, w_u, qg, kg, cos_t, sin_t, bdq, bdk, conv_w, conv_b, bound):
    b, seq, d = x.shape
    tm = min(ROW_TILE, seq)
    n_tiles = seq // tm
    hb = tm // HALO
    full = lambda shape: pl.BlockSpec(shape, lambda bi, i: (0,) * len(shape))
    row_out = lambda w: pl.BlockSpec((None, tm, w), lambda bi, i: (bi, i, 0))
    kv_out = pl.BlockSpec((None, N_KV_HEADS, tm, LANES), lambda bi, i: (bi, 0, i, 0))
    return pl.pallas_call(
        functools.partial(_in_proj_kernel, tm=tm, n_tiles=n_tiles),
        grid=(b, n_tiles),
        in_specs=[
            pl.BlockSpec((None, tm, d), lambda bi, i: (bi, i, 0)),
            pl.BlockSpec((None, HALO, d), lambda bi, i: (bi, jnp.maximum(i * hb - 1, 0), 0)),
            pl.BlockSpec((None, HALO, d), lambda bi, i: (bi, jnp.minimum((i + 1) * hb, seq // HALO - 1), 0)),
            full((1, d)),
            full(w_qkv.shape),
            full(w_u.shape),
            full((1, Q_WIDTH)),
            full((1, KV_WIDTH)),
            pl.BlockSpec((tm, LANES), lambda bi, i: (i, 0)),
            pl.BlockSpec((tm, LANES), lambda bi, i: (i, 0)),
            full(bdq.shape),
            full(bdk.shape),
            full(conv_w.shape),
            full(conv_b.shape),
            pl.BlockSpec(memory_space=pltpu.SMEM),
        ],
        out_specs=[row_out(2 * Q_WIDTH), kv_out, kv_out, row_out(HY_WIDTH), row_out(HY_WIDTH), row_out(HY_WIDTH)],
        out_shape=[
            jax.ShapeDtypeStruct((b, seq, 2 * Q_WIDTH), BF16),
            jax.ShapeDtypeStruct((b, N_KV_HEADS, seq, LANES), BF16),
            jax.ShapeDtypeStruct((b, N_KV_HEADS, seq, LANES), BF16),
            jax.ShapeDtypeStruct((b, seq, HY_WIDTH), F32),
            jax.ShapeDtypeStruct((b, seq, HY_WIDTH), F32),
            jax.ShapeDtypeStruct((b, seq, HY_WIDTH), F32),
        ],
        scratch_shapes=[pltpu.VMEM((tm + 2 * HALO, d), BF16)],
        compiler_params=_params("parallel", "arbitrary"),
        name="in_proj",
    )(x, x, x, g1, w_qkv, w_u, qg, kg, cos_t, sin_t, bdq, bdk, conv_w, conv_b, bound)


def _stack_heads(q_ref):
    group = N_Q_HEADS // N_KV_HEADS
    return jnp.concatenate([q_ref[:, g * LANES:(g + 1) * LANES] for g in range(group)], axis=0)


def _store_heads(o_ref, o, tq):
    group = N_Q_HEADS // N_KV_HEADS
    pairs = [o[(2 * p) * tq:(2 * p + 1) * tq] + pltpu.roll(o[(2 * p + 1) * tq:(2 * p + 2) * tq], HEAD_DIM, 1)
             for p in range(group // 2)]
    o_ref[...] = jnp.concatenate(pairs, axis=1)


def _attn_bounded_kernel(q_ref, k_ref, v_ref, o_ref, *, tq, sc, n_chunks):
    q = _stack_heads(q_ref)
    rows = q.shape[0]

    def body(c, acc):
        off = pl.multiple_of(c * sc, sc)
        s = lax.dot_general(q, k_ref[pl.ds(off, sc), :], (((1,), (1,)), ((), ())), preferred_element_type=F32)
        return acc + _dot(jnp.exp2(s).astype(BF16), v_ref[pl.ds(off, sc), :])

    acc = lax.fori_loop(0, n_chunks, body, jnp.zeros((rows, LANES), F32))
    lane = lax.broadcasted_iota(jnp.int32, (rows, LANES), 1)
    _store_heads(o_ref, jnp.where(lane < HEAD_DIM, acc / acc[:, HEAD_DIM:HEAD_DIM + 1], 0.0), tq)


def _attn_online_kernel(q_ref, k_ref, v_ref, o_ref, *, tq, sc, n_chunks):
    q = _stack_heads(q_ref)
    rows = q.shape[0]

    def body(c, carry):
        m, l, acc = carry
        off = pl.multiple_of(c * sc, sc)
        s = lax.dot_general(q, k_ref[pl.ds(off, sc), :], (((1,), (1,)), ((), ())), preferred_element_type=F32)
        m_new = jnp.maximum(m, jnp.max(s, axis=1, keepdims=True))
        p = jnp.exp2(s - m_new)
        alpha = jnp.exp2(m - m_new)
        l = alpha * l + jnp.sum(p, axis=1, keepdims=True)
        acc = alpha * acc + _dot(p.astype(BF16), v_ref[pl.ds(off, sc), :])
        return m_new, l, acc

    init = (jnp.full((rows, 1), jnp.finfo(F32).min, F32), jnp.zeros((rows, 1), F32), jnp.zeros((rows, LANES), F32))
    _, l, acc = lax.fori_loop(0, n_chunks, body, init)
    lane = lax.broadcasted_iota(jnp.int32, (rows, LANES), 1)
    _store_heads(o_ref, jnp.where(lane < HEAD_DIM, acc / l, 0.0), tq)


def _attention(body, sc, name, q, k, v):
    b, seq, _ = q.shape
    tq = min(Q_TILE, seq)
    sc = min(sc, seq)
    group = N_Q_HEADS // N_KV_HEADS
    return pl.pallas_call(
        functools.partial(body, tq=tq, sc=sc, n_chunks=seq // sc),
        grid=(b, N_KV_HEADS, seq // tq),
        in_specs=[
            pl.BlockSpec((None, tq, group * LANES), lambda bi, j, i: (bi, i, j)),
            pl.BlockSpec((None, None, seq, LANES), lambda bi, j, i: (bi, j, 0, 0)),
            pl.BlockSpec((None, None, seq, LANES), lambda bi, j, i: (bi, j, 0, 0)),
        ],
        out_specs=pl.BlockSpec((None, tq, group * HEAD_DIM), lambda bi, j, i: (bi, i, j)),
        out_shape=jax.ShapeDtypeStruct((b, seq, Q_WIDTH), F32),
        compiler_params=_params("parallel", "parallel", "arbitrary"),
        name=name,
    )(q, k, v)


def _filter_kernel(z_ref, w1_ref, b1_ref, w2_ref, b2_ref, w3_ref, b3_ref, w4_ref, fr_ref, dl_ref,
                   taps_ref, s_ref, *, tl):
    i = pl.program_id(0)
    z = z_ref[...]
    fr = fr_ref[...]
    h = jnp.sin(fr * (_dot3(z, w1_ref[...]) + b1_ref[...]))
    h = jnp.sin(fr * (_dot3(h, w2_ref[...]) + b2_ref[...]))
    h = jnp.sin(fr * (_dot3(h, w3_ref[...]) + b3_ref[...]))
    h = _dot3(h, w4_ref[...])
    t = z[:, 0:1]
    h = h * jnp.exp(-t * jnp.abs(dl_ref[...]))
    row = i * tl + lax.broadcasted_iota(jnp.int32, (tl, HY_WIDTH), 0)
    sums = []
    for j in range(4):
        hj = h[:, j * HY_WIDTH:(j + 1) * HY_WIDTH]
        if j % 2 == 1:
            hj = jnp.where(row == 0, 0.0, hj)
        taps_ref[j] = hj
        sums.append(jnp.sum(jnp.abs(hj), axis=0, keepdims=True))
    ssum = jnp.concatenate(sums, axis=1)

    @pl.when(i == 0)
    def _():
        s_ref[...] = ssum

    @pl.when(i > 0)
    def _():
        s_ref[...] += ssum


def _filter_taps(zf, w1, b1, w2, b2, w3, b3, w4, freq, deltas):
    seq = zf.shape[0]
    tl = min(FILT_TILE, seq)
    n_filt = w4.shape[1]
    full = lambda a: pl.BlockSpec(a.shape, lambda i: (0,) * a.ndim)
    return pl.pallas_call(
        functools.partial(_filter_kernel, tl=tl),
        grid=(seq // tl,),
        in_specs=[pl.BlockSpec((tl, FILTER_PAD), lambda i: (i, 0))] + [full(a) for a in (w1, b1, w2, b2, w3, b3, w4, freq, deltas)],
        out_specs=[pl.BlockSpec((4, tl, HY_WIDTH), lambda i: (0, i, 0)), pl.BlockSpec((1, n_filt), lambda i: (0, 0))],
        out_shape=[jax.ShapeDtypeStruct((4, seq, HY_WIDTH), F32), jax.ShapeDtypeStruct((1, n_filt), F32)],
        compiler_params=_params("arbitrary"),
        name="filter_taps",
    )(zf, w1, b1, w2, b2, w3, b3, w4, freq, deltas)


def _fft_outer_complex_kernel(xr_ref, xi_ref, f_ref, ar_ref, ai_ref, *, n1):
    x = jnp.concatenate([xr_ref[...], xi_ref[...]], axis=0).astype(BF16)
    y = _dot(f_ref[...], x)
    ar_ref[...] = y[:n1].astype(BF16)
    ai_ref[...] = y[n1:].astype(BF16)


def _fft_outer_complex(z, f1):
    _, nh, cols = z.shape
    n1 = f1.shape[0] // 2
    tn = min(FFT_COLS, cols)
    out = pl.BlockSpec((n1, tn), lambda j: (0, j))
    return pl.pallas_call(
        functools.partial(_fft_outer_complex_kernel, n1=n1),
        grid=(cols // tn,),
        in_specs=[pl.BlockSpec((None, nh, tn), lambda j: (0, 0, j)),
                  pl.BlockSpec((None, nh, tn), lambda j: (1, 0, j)),
                  pl.BlockSpec(f1.shape, lambda j: (0, 0))],
        out_specs=[out, out],
        out_shape=[jax.ShapeDtypeStruct((n1, cols), BF16)] * 2,
        compiler_params=_params("parallel"),
        name="fft_outer",
    )(z, z, f1)


def _fft_outer_real_kernel(x_ref, f_ref, ar_ref, ai_ref, *, n1):
    y = _dot(f_ref[...], x_ref[...].astype(BF16))
    ar_ref[...] = y[:n1].astype(BF16)
    ai_ref[...] = y[n1:].astype(BF16)


def _fft_outer_real(taps, f1r):
    n_sig, nh, cols = taps.shape
    n1 = f1r.shape[0] // 2
    tn = min(FFT_COLS, cols)
    out = pl.BlockSpec((None, n1, tn), lambda s, j: (s, 0, j))
    return pl.pallas_call(
        functools.partial(_fft_outer_real_kernel, n1=n1),
        grid=(n_sig, cols // tn),
        in_specs=[pl.BlockSpec((None, nh, tn), lambda s, j: (s, 0, j)),
                  pl.BlockSpec(f1r.shape, lambda s, j: (0, 0))],
        out_specs=[out, out],
        out_shape=[jax.ShapeDtypeStruct((n_sig, n1, cols), BF16)] * 2,
        compiler_params=_params("parallel", "parallel"),
        name="fft_outer_filter",
    )(taps, f1r)


def _filter_spectrum_kernel(ar_ref, ai_ref, gf_ref, s_ref, hr_ref, hi_ref, *, kb, n_fft):
    scale = 1.0 / ((s_ref[0] + s_ref[1]) * n_fft)
    for kk in range(kb):
        g = gf_ref[kk]
        cf = _dot(g, jnp.concatenate([ar_ref[0, kk], ai_ref[0, kk]], axis=0))
        cb = _dot(g, jnp.concatenate([ar_ref[1, kk], ai_ref[1, kk]], axis=0))
        hr_ref[kk] = (cf[:LANES] + cb[:LANES]) * scale
        hi_ref[kk] = (cf[LANES:] - cb[LANES:]) * scale


def _filter_spectrum(ar, ai, gf, s, n_fft):
    n_ord, _, n1, _, c = ar.shape
    kb = min(MID_K1, n1)
    a_spec = pl.BlockSpec((None, 2, kb, LANES, c), lambda o, k: (o, 0, k, 0, 0))
    h_spec = pl.BlockSpec((None, kb, LANES, c), lambda o, k: (o, k, 0, 0))
    return pl.pallas_call(
        functools.partial(_filter_spectrum_kernel, kb=kb, n_fft=n_fft),
        grid=(n_ord, n1 // kb),
        in_specs=[a_spec, a_spec,
                  pl.BlockSpec((kb, 2 * LANES, 2 * LANES), lambda o, k: (k, 0, 0)),
                  pl.BlockSpec((None, 2, 1, c), lambda o, k: (o, 0, 0, 0))],
        out_specs=[h_spec, h_spec],
        out_shape=[jax.ShapeDtypeStruct((n_ord, n1, LANES, c), F32)] * 2,
        compiler_params=_params("parallel", "parallel"),
        name="filter_spectrum",
    )(ar, ai, gf, s)


def _fft_inner_kernel(ar_ref, ai_ref, gf_ref, gi_ref, hr_ref, hi_ref, br_ref, bi_ref, *, kb):
    for kk in range(kb):
        x = jnp.concatenate([ar_ref[kk], ai_ref[kk]], axis=0)
        c = _dot(gf_ref[kk], x)
        cr, ci = c[:LANES], c[LANES:]
        hr, hi = hr_ref[kk], hi_ref[kk]
        p = jnp.concatenate([cr * hr - ci * hi, cr * hi + ci * hr], axis=0).astype(BF16)
        y = _dot(gi_ref[kk], p)
        br_ref[kk] = y[:LANES].astype(BF16)
        bi_ref[kk] = y[LANES:].astype(BF16)


def _fft_inner(ar, ai, gf, gi, hr, hi, order):
    n1, _, c = ar.shape
    kb = min(MID_K1, n1)
    a_spec = pl.BlockSpec((kb, LANES, c), lambda k: (k, 0, 0))
    g_spec = pl.BlockSpec((kb, 2 * LANES, 2 * LANES), lambda k: (k, 0, 0))
    h_spec = pl.BlockSpec((None, kb, LANES, c), lambda k: (order, k, 0, 0))
    return pl.pallas_call(
        functools.partial(_fft_inner_kernel, kb=kb),
        grid=(n1 // kb,),
        in_specs=[a_spec, a_spec, g_spec, g_spec, h_spec, h_spec],
        out_specs=[a_spec, a_spec],
        out_shape=[jax.ShapeDtypeStruct((n1, LANES, c), BF16)] * 2,
        compiler_params=_params("parallel"),
        name="fft_inner",
    )(ar, ai, gf, gi, hr, hi)


def _ifft_outer_gate_kernel(br_ref, bi_ref, f_ref, zr_ref, zi_ref, gr_ref, gi_ref, skip_ref, o_ref, *, nh):
    y = _dot(f_ref[...], jnp.concatenate([br_ref[...], bi_ref[...]], axis=0))
    skip = skip_ref[...]
    o_ref[0] = gr_ref[...] * (y[:nh] + skip * zr_ref[...])
    o_ref[1] = gi_ref[...] * (y[nh:] + skip * zi_ref[...])


def _ifft_outer_gate(br, bi, f1inv, z, gate, skip_row):
    _, nh, cols = z.shape
    n1 = br.shape[0]
    tn = min(FFT_COLS, cols)
    b_spec = pl.BlockSpec((n1, tn), lambda j: (0, j))
    re = pl.BlockSpec((None, nh, tn), lambda j: (0, 0, j))
    im = pl.BlockSpec((None, nh, tn), lambda j: (1, 0, j))
    return pl.pallas_call(
        functools.partial(_ifft_outer_gate_kernel, nh=nh),
        grid=(cols // tn,),
        in_specs=[b_spec, b_spec, pl.BlockSpec(f1inv.shape, lambda j: (0, 0)), re, im, re, im,
                  pl.BlockSpec((1, tn), lambda j: (0, 0))],
        out_specs=pl.BlockSpec((2, nh, tn), lambda j: (0, 0, j)),
        out_shape=jax.ShapeDtypeStruct((2, nh, cols), F32),
        compiler_params=_params("parallel"),
        name="ifft_outer_gate",
    )(br, bi, f1inv, z, z, gate, gate, skip_row)


def _dft_tables(seq):
    n = 2 * seq
    n1 = n // LANES
    nh = n1 // 2
    k1 = jnp.arange(n1, dtype=jnp.int32)
    ang = ((k1[:, None] * jnp.arange(nh, dtype=jnp.int32)[None, :]) % n1).astype(F32) * (2.0 * math.pi / n1)
    fc, fs = jnp.cos(ang), jnp.sin(ang)
    f1 = jnp.concatenate([jnp.concatenate([fc, fs], 1), jnp.concatenate([-fs, fc], 1)], 0).astype(BF16)
    f1r = jnp.concatenate([fc, -fs], 0).astype(BF16)
    f1inv = jnp.concatenate([jnp.concatenate([fc.T, -fs.T], 1), jnp.concatenate([fs.T, fc.T], 1)], 0).astype(BF16)
    k2 = jnp.arange(LANES, dtype=jnp.int32)
    k = k1[:, None, None] + n1 * k2[None, :, None]
    th = ((k * k2[None, None, :]) % n).astype(F32) * (2.0 * math.pi / n)
    c, s = jnp.cos(th), jnp.sin(th)
    gf = jnp.concatenate([jnp.concatenate([c, s], 2), jnp.concatenate([-s, c], 2)], 1).astype(BF16)
    gi = jnp.swapaxes(gf, 1, 2)
    return f1, f1r, f1inv, gf, gi


def _filter_features(seq):
    t = jnp.linspace(0.0, 1.0, seq, dtype=F32)[:, None]
    bands = (FILTER_EMB - 1) // 2
    w_ang = 2.0 * math.pi * jnp.arange(seq, dtype=F32) / seq
    band_f = jnp.linspace(1e-4, bands - 1, bands, dtype=F32)
    ang = w_ang[:, None] * band_f[None, :]
    z = jnp.concatenate([t, jnp.cos(ang), -jnp.sin(ang)], axis=-1)
    return jnp.pad(z, ((0, 0), (0, FILTER_PAD - FILTER_EMB)))


def _hyena(hv, x1, x2, fw1, fb1, fw2, fb2, fw3, fb3, fw4, ffreq, fdeltas, skip_d):
    b, seq, c = hv.shape
    assert b == 2, "the two batch rows ride as the real and imaginary parts of one transform"
    n = 2 * seq
    n1 = n // LANES
    nh = n1 // 2
    cols = LANES * c
    f1, f1r, f1inv, gf, gi = _dft_tables(seq)

    zf = _filter_features(seq)
    w1p = jnp.pad(fw1, ((0, FILTER_PAD - FILTER_EMB), (0, 0)))
    taps, s = _filter_taps(zf, w1p, fb1[None], fw2, fb2[None], fw3, fb3[None], fw4, ffreq[None], fdeltas[None])
    tr, ti = _fft_outer_real(taps.reshape(4, nh, cols), f1r)
    hr, hi = _filter_spectrum(tr.reshape(2, 2, n1, LANES, c), ti.reshape(2, 2, n1, LANES, c), gf,
                              s.reshape(2, 2, 1, c), n)

    tn = min(FFT_COLS, cols)
    zz = hv.reshape(2, nh, cols)
    for order, gate in enumerate((x1, x2)):
        ar, ai = _fft_outer_complex(zz, f1)
        br, bi = _fft_inner(ar.reshape(n1, LANES, c), ai.reshape(n1, LANES, c), gf, gi, hr, hi, order)
        skip_row = jnp.tile(skip_d[order], tn // c)[None]
        zz = _ifft_outer_gate(br.reshape(n1, cols), bi.reshape(n1, cols), f1inv, zz, gate.reshape(2, nh, cols), skip_row)
    return zz.reshape(b, seq, c)


def _out_mlp_kernel(x_ref, att_ref, hy_ref, ga_ref, gh_ref, wo_ref, g2_ref, w1_ref, w2_ref, gf_ref, o_ref):
    a = _rms(att_ref[...], ga_ref[...]).astype(BF16)
    hyn = _rms(hy_ref[...], gh_ref[...]).astype(BF16)
    half = a.shape[1]
    h = x_ref[...] + _dot(a, wo_ref[:half, :]) + _dot(hyn, wo_ref[half:, :])
    m = _rms(h, g2_ref[...]).astype(BF16)
    t = jnp.square(jnp.maximum(_dot(m, w1_ref[...]), 0.0)).astype(BF16)
    h = h + _dot(t, w2_ref[...])
    o_ref[...] = _rms(h, gf_ref[...])


def _out_mlp(x, att, hy, ga, gh, w_out, g2, w1, w2, gfin):
    b, seq, d = x.shape
    tm = min(ROW_TILE, seq)
    row = lambda w: pl.BlockSpec((None, tm, w), lambda bi, i: (bi, i, 0))
    full = lambda a: pl.BlockSpec(a.shape, lambda bi, i: (0,) * a.ndim, pipeline_mode=pl.Buffered(1))
    return pl.pallas_call(
        _out_mlp_kernel,
        grid=(b, seq // tm),
        in_specs=[row(d), row(att.shape[-1]), row(hy.shape[-1]), full(ga), full(gh), full(w_out), full(g2),
                  full(w1), full(w2), full(gfin)],
        out_specs=row(d),
        out_shape=jax.ShapeDtypeStruct((b, seq, d), F32),
        compiler_params=_params("parallel", "parallel"),
        name="out_mlp",
    )(x, att, hy, ga, gh, w_out, g2, w1, w2, gfin)


def _rope_tables(seq):
    rows = seq // GRID_W
    row = jnp.repeat(jnp.arange(rows, dtype=F32), GRID_W)
    col = jnp.tile(jnp.arange(GRID_W, dtype=F32), rows)
    half = HEAD_DIM // 2
    inv_freq = ROPE_THETA ** (-jnp.arange(0, half, 2, dtype=F32) / half)
    ang_r = row[:, None] * inv_freq[None, :]
    ang_c = col[:, None] * inv_freq[None, :]
    cos = jnp.concatenate([jnp.cos(ang_r)] * 2 + [jnp.cos(ang_c)] * 2, axis=1)
    sin = jnp.concatenate([-jnp.sin(ang_r), jnp.sin(ang_r), -jnp.sin(ang_c), jnp.sin(ang_c)], axis=1)
    return jnp.tile(cos, (1, LANES // HEAD_DIM)), jnp.tile(sin, (1, LANES // HEAD_DIM))


def _head_mean_matrix(width):
    head = jnp.arange(width, dtype=jnp.int32) // HEAD_DIM
    return jnp.where(head[:, None] == head[None, :], 1.0 / HEAD_DIM, 0.0).astype(BF16)


def kernel(x, norm1_g, w_in, q_norm_g, k_norm_g, hy_conv_w, hy_conv_b, filt_w1, filt_b1, filt_w2, filt_b2, filt_w3, filt_b3, filt_w4, filt_freq, filt_deltas, hy_skip_d, attn_out_g, hy_out_g, w_out, norm2_g, w_mlp_in, w_mlp_out, final_g):
    seq = x.shape[1]
    cos_t, sin_t = _rope_tables(seq)
    bdq = _head_mean_matrix(Q_WIDTH)
    bdk = _head_mean_matrix(KV_WIDTH)
    h = x
    for i in range(norm1_g.shape[0]):
        w_qkv = w_in[i][:, :Q_WIDTH + 2 * KV_WIDTH].astype(BF16)
        w_u = w_in[i][:, Q_WIDTH + 2 * KV_WIDTH:].astype(BF16)
        bound = (1.01 * HEAD_DIM ** 0.5 * LOG2E) * jnp.max(jnp.abs(q_norm_g[i])) * jnp.max(jnp.abs(k_norm_g[i]))
        q, k, v, hv, x1, x2 = _in_proj(
            h, norm1_g[i][None], w_qkv, w_u, jnp.tile(q_norm_g[i], N_Q_HEADS)[None],
            jnp.tile(k_norm_g[i], N_KV_HEADS)[None], cos_t, sin_t, bdq, bdk, hy_conv_w[i], hy_conv_b[i][None],
            bound.reshape(1, 1))
        att = lax.cond(
            bound <= MAX_FIXED_SHIFT,
            functools.partial(_attention, _attn_bounded_kernel, KV_CHUNK_BOUNDED, "attention_bounded"),
            functools.partial(_attention, _attn_online_kernel, KV_CHUNK, "attention_online"),
            q, k, v)
        hy = _hyena(hv, x1, x2, filt_w1[i], filt_b1[i], filt_w2[i], filt_b2[i], filt_w3[i], filt_b3[i],
                    filt_w4[i], filt_freq[i], filt_deltas[i], hy_skip_d[i])
        last = i == norm1_g.shape[0] - 1
        assert last, "single-layer trunk"
        h = _out_mlp(h, att, hy, attn_out_g[i][None], hy_out_g[i][None], w_out[i].astype(BF16), norm2_g[i][None],
                     w_mlp_in[i].astype(BF16), w_mlp_out[i].astype(BF16), final_g[None])
    return h
```

```python
import functools
import math

import jax
import jax.numpy as jnp
from jax import lax
from jax.experimental import pallas as pl
from jax.experimental.pallas import tpu as pltpu

F32 = jnp.float32
BF16 = jnp.bfloat16

HEAD_DIM = 64
N_Q_HEADS = 8
N_KV_HEADS = 2
Q_WIDTH = N_Q_HEADS * HEAD_DIM
KV_WIDTH = N_KV_HEADS * HEAD_DIM
HY_WIDTH = 512
GRID_W = 64
ROPE_THETA = 10000.0
FILTER_EMB = 33
FILTER_PAD = 64
EPS = 1e-6
LOG2E = math.log2(math.e)
MAX_FIXED_SHIFT = 50.0

LANES = 128
SUBLANES = 8
TX_ROWS = SUBLANES * LANES
HALO = 16
VMEM_LIMIT = 56 * 1024 * 1024

ROW_TILE = TX_ROWS
MLP_ROWS = 512
Q_TILE = 128
KV_CHUNK = 512
KV_CHUNK_BOUNDED = 8192
FILT_TILE = 512


def _dot(a, b):
    return jnp.dot(a, b, preferred_element_type=F32)


def _dot3(a, b):
    ah = a.astype(BF16)
    al = (a - ah.astype(F32)).astype(BF16)
    bh = b.astype(BF16)
    bl = (b - bh.astype(F32)).astype(BF16)
    return _dot(ah, bh) + (_dot(ah, bl) + _dot(al, bh))


def _rms(v, g):
    return v * lax.rsqrt(jnp.mean(v * v, axis=-1, keepdims=True) + EPS) * g


def _params(*sem):
    return pltpu.CompilerParams(dimension_semantics=sem, vmem_limit_bytes=VMEM_LIMIT)


def _in_proj_kernel(x_ref, xp_ref, xn_ref, g1_ref, wqkv_ref, wu_ref, qg_ref, kg_ref, cos_ref, sin_ref,
                    bdq_ref, bdk_ref, cw_ref, cb_ref, bound_ref,
                    q_ref, k_ref, v_ref, hv_ref, x1_ref, x2_ref, a_scr, *, tm, n_tiles):
    i = pl.program_id(1)
    g1 = g1_ref[...]
    a_scr[HALO:HALO + tm, :] = _rms(x_ref[...], g1).astype(BF16)
    prev = jnp.where(i > 0, _rms(xp_ref[...], g1), 0.0)
    nxt = jnp.where(i < n_tiles - 1, _rms(xn_ref[...], g1), 0.0)
    a_scr[0:HALO, :] = prev.astype(BF16)
    a_scr[HALO + tm:, :] = nxt.astype(BF16)

    qkv = _dot(a_scr[HALO:HALO + tm, :], wqkv_ref[...])
    cos = cos_ref[...]
    sin = sin_ref[...]
    lane = lax.broadcasted_iota(jnp.int32, (tm, LANES), 1)
    first16 = (lane % 32) < 16
    low_half = lane < HEAD_DIM

    def norm_rope(blk, ms, gain):
        y = blk * lax.rsqrt(ms + EPS) * gain
        partner = jnp.where(first16, pltpu.roll(y, LANES - 16, 1), pltpu.roll(y, 16, 1))
        return y * cos + partner * sin

    q = qkv[:, :Q_WIDTH]
    q_ms = _dot((q * q).astype(BF16), bdq_ref[...])
    qg = qg_ref[...]
    neg_bound = -bound_ref[0, 0]
    extra = lane == HEAD_DIM
    for m in range(Q_WIDTH // LANES):
        sl = slice(m * LANES, (m + 1) * LANES)
        blk = norm_rope(q[:, sl], q_ms[:, sl], qg[:, sl]) * (HEAD_DIM ** -0.5 * LOG2E)
        q_ref[:, 2 * m * LANES:(2 * m + 1) * LANES] = jnp.where(extra, neg_bound, blk).astype(BF16)
        q_ref[:, (2 * m + 1) * LANES:(2 * m + 2) * LANES] = jnp.where(
            extra, neg_bound, pltpu.roll(blk, HEAD_DIM, 1)).astype(BF16)

    k = qkv[:, Q_WIDTH:Q_WIDTH + KV_WIDTH]
    k_ms = _dot((k * k).astype(BF16), bdk_ref[...])
    k = norm_rope(k, k_ms, kg_ref[...])
    v = qkv[:, Q_WIDTH + KV_WIDTH:]
    pad = jnp.where(extra, 1.0, 0.0)
    k_ref[0] = jnp.where(low_half, k, pad).astype(BF16)
    k_ref[1] = jnp.where(low_half, pltpu.roll(k, HEAD_DIM, 1), pad).astype(BF16)
    v_ref[0] = jnp.where(low_half, v, pad).astype(BF16)
    v_ref[1] = jnp.where(low_half, pltpu.roll(v, HEAD_DIM, 1), pad).astype(BF16)

    rows = tm + 2 * HALO
    for c, out in enumerate((hv_ref, x1_ref, x2_ref)):
        cs = slice(c * HY_WIDTH, (c + 1) * HY_WIDTH)
        u = _dot(a_scr[...], wu_ref[:, cs])
        w = cw_ref[:, cs]
        y = (pltpu.roll(u, 1, 0) * w[0:1] + u * w[1:2] + pltpu.roll(u, rows - 1, 0) * w[2:3]) + cb_ref[:, cs]
        for cq in range(HY_WIDTH // LANES):
            for p in range(SUBLANES):
                out[cq, pl.ds(p, LANES, stride=SUBLANES), :] = y[HALO + p * LANES:HALO + (p + 1) * LANES,
                                                                 cq * LANES:(cq + 1) * LANES]


def _in_proj(x, g1, w_qkv, w_u, qg, kg, cos_t, sin_t, bdq, bdk, conv_w, conv_b, bound):
    b, seq, d = x.shape
    tm = ROW_TILE
    n_tiles = seq // tm
    hb = tm // HALO
    cq = HY_WIDTH // LANES
    full = lambda shape: pl.BlockSpec(shape, lambda bi, i: (0,) * len(shape))
    row_out = lambda w: pl.BlockSpec((None, tm, w), lambda bi, i: (bi, i, 0))
    kv_out = pl.BlockSpec((None, N_KV_HEADS, tm, LANES), lambda bi, i: (bi, 0, i, 0))
    tx_out = pl.BlockSpec((None, cq, None, tm, LANES), lambda bi, i: (bi, 0, i, 0, 0))
    tx_shape = jax.ShapeDtypeStruct((b, cq, n_tiles, tm, LANES), F32)
    return pl.pallas_call(
        functools.partial(_in_proj_kernel, tm=tm, n_tiles=n_tiles),
        grid=(b, n_tiles),
        in_specs=[
            pl.BlockSpec((None, tm, d), lambda bi, i: (bi, i, 0)),
            pl.BlockSpec((None, HALO, d), lambda bi, i: (bi, jnp.maximum(i * hb - 1, 0), 0)),
            pl.BlockSpec((None, HALO, d), lambda bi, i: (bi, jnp.minimum((i + 1) * hb, seq // HALO - 1), 0)),
            full((1, d)),
            full(w_qkv.shape),
            full(w_u.shape),
            full((1, Q_WIDTH)),
            full((1, KV_WIDTH)),
            pl.BlockSpec((tm, LANES), lambda bi, i: (i, 0)),
            pl.BlockSpec((tm, LANES), lambda bi, i: (i, 0)),
            full(bdq.shape),
            full(bdk.shape),
            full(conv_w.shape),
            full(conv_b.shape),
            pl.BlockSpec(memory_space=pltpu.SMEM),
        ],
        out_specs=[row_out(2 * Q_WIDTH), kv_out, kv_out, tx_out, tx_out, tx_out],
        out_shape=[
            jax.ShapeDtypeStruct((b, seq, 2 * Q_WIDTH), BF16),
            jax.ShapeDtypeStruct((b, N_KV_HEADS, seq, LANES), BF16),
            jax.ShapeDtypeStruct((b, N_KV_HEADS, seq, LANES), BF16),
            tx_shape, tx_shape, tx_shape,
        ],
        scratch_shapes=[pltpu.VMEM((tm + 2 * HALO, d), BF16)],
        compiler_params=_params("parallel", "arbitrary"),
        name="in_proj",
    )(x, x, x, g1, w_qkv, w_u, qg, kg, cos_t, sin_t, bdq, bdk, conv_w, conv_b, bound)


def _stack_heads(q_ref):
    group = N_Q_HEADS // N_KV_HEADS
    return jnp.concatenate([q_ref[:, g * LANES:(g + 1) * LANES] for g in range(group)], axis=0)


def _store_heads(o_ref, o, tq):
    group = N_Q_HEADS // N_KV_HEADS
    pairs = [o[(2 * p) * tq:(2 * p + 1) * tq] + pltpu.roll(o[(2 * p + 1) * tq:(2 * p + 2) * tq], HEAD_DIM, 1)
             for p in range(group // 2)]
    o_ref[...] = jnp.concatenate(pairs, axis=1)


def _attn_bounded_kernel(q_ref, k_ref, v_ref, o_ref, *, tq, sc, n_chunks):
    q = _stack_heads(q_ref)
    rows = q.shape[0]

    def body(c, acc):
        off = pl.multiple_of(c * sc, sc)
        s = lax.dot_general(q, k_ref[pl.ds(off, sc), :], (((1,), (1,)), ((), ())), preferred_element_type=F32)
        return acc + _dot(jnp.exp2(s).astype(BF16), v_ref[pl.ds(off, sc), :])

    acc = lax.fori_loop(0, n_chunks, body, jnp.zeros((rows, LANES), F32))
    lane = lax.broadcasted_iota(jnp.int32, (rows, LANES), 1)
    _store_heads(o_ref, jnp.where(lane < HEAD_DIM, acc / acc[:, HEAD_DIM:HEAD_DIM + 1], 0.0), tq)


def _attn_online_kernel(q_ref, k_ref, v_ref, o_ref, *, tq, sc, n_chunks):
    q = _stack_heads(q_ref)
    rows = q.shape[0]

    def body(c, carry):
        m, l, acc = carry
        off = pl.multiple_of(c * sc, sc)
        s = lax.dot_general(q, k_ref[pl.ds(off, sc), :], (((1,), (1,)), ((), ())), preferred_element_type=F32)
        m_new = jnp.maximum(m, jnp.max(s, axis=1, keepdims=True))
        p = jnp.exp2(s - m_new)
        alpha = jnp.exp2(m - m_new)
        l = alpha * l + jnp.sum(p, axis=1, keepdims=True)
        acc = alpha * acc + _dot(p.astype(BF16), v_ref[pl.ds(off, sc), :])
        return m_new, l, acc

    init = (jnp.full((rows, 1), jnp.finfo(F32).min, F32), jnp.zeros((rows, 1), F32), jnp.zeros((rows, LANES), F32))
    _, l, acc = lax.fori_loop(0, n_chunks, body, init)
    lane = lax.broadcasted_iota(jnp.int32, (rows, LANES), 1)
    _store_heads(o_ref, jnp.where(lane < HEAD_DIM, acc / l, 0.0), tq)


def _attention(body, sc, name, q, k, v):
    b, seq, _ = q.shape
    tq = min(Q_TILE, seq)
    sc = min(sc, seq)
    group = N_Q_HEADS // N_KV_HEADS
    return pl.pallas_call(
        functools.partial(body, tq=tq, sc=sc, n_chunks=seq // sc),
        grid=(b, N_KV_HEADS, seq // tq),
        in_specs=[
            pl.BlockSpec((None, tq, group * LANES), lambda bi, j, i: (bi, i, j)),
            pl.BlockSpec((None, None, seq, LANES), lambda bi, j, i: (bi, j, 0, 0)),
            pl.BlockSpec((None, None, seq, LANES), lambda bi, j, i: (bi, j, 0, 0)),
        ],
        out_specs=pl.BlockSpec((None, tq, group * HEAD_DIM), lambda bi, j, i: (bi, i, j)),
        out_shape=jax.ShapeDtypeStruct((b, seq, Q_WIDTH), F32),
        compiler_params=_params("parallel", "parallel", "arbitrary"),
        name=name,
    )(q, k, v)


def _lane_blocks(y, rows, n_lead):
    return [y[rows, c * LANES:(c + 1) * LANES].reshape(n_lead, SUBLANES, LANES) for c in range(HY_WIDTH // LANES)]


def _strided_slab(ref, lead, start, rows):
    return jnp.concatenate([ref[lead + (c, pl.ds(start, rows, stride=SUBLANES), slice(None))]
                            for c in range(HY_WIDTH // LANES)], axis=1)


def _filter_kernel(z_ref, w1_ref, b1_ref, w2_ref, b2_ref, w3_ref, b3_ref, w4f_ref, w4b_ref, fr_ref, dlf_ref, dlb_ref,
                   taps_ref, s_ref, *, groups, nh):
    i = pl.program_id(0)
    z = z_ref[...]
    fr = fr_ref[...]
    h = jnp.sin(fr * (_dot3(z, w1_ref[...]) + b1_ref[...]))
    h = jnp.sin(fr * (_dot3(h, w2_ref[...]) + b2_ref[...]))
    h = jnp.sin(fr * (_dot3(h, w3_ref[...]) + b3_ref[...]))
    half = groups * nh
    t = z[:, 0:1]
    fwd = _dot3(h[:half], w4f_ref[...]) * jnp.exp(-t[:half] * jnp.abs(dlf_ref[...]))
    bwd = _dot3(h[half:], w4b_ref[...]) * jnp.exp(-t[half:] * jnp.abs(dlb_ref[...]))
    row = lax.broadcasted_iota(jnp.int32, bwd.shape, 0)
    bwd = jnp.where((row == 0) & (i == 0), 0.0, bwd)
    for o in range(2):
        for c in range(HY_WIDTH // LANES):
            cs = slice(o * HY_WIDTH + c * LANES, o * HY_WIDTH + (c + 1) * LANES)
            for g in range(groups):
                taps_ref[o, c, g, 0:nh, :] = fwd[g * nh:(g + 1) * nh, cs].astype(BF16)
                taps_ref[o, c, g, nh:, :] = bwd[g * nh:(g + 1) * nh, cs].astype(BF16)
    ssum = jnp.sum(jnp.abs(fwd), axis=0, keepdims=True) + jnp.sum(jnp.abs(bwd), axis=0, keepdims=True)

    @pl.when(i == 0)
    def _():
        s_ref[...] = ssum

    @pl.when(i > 0)
    def _():
        s_ref[...] += ssum


def _filter_taps(zp, w1, b1, w2, b2, w3, b3, w4f, w4b, freq, dlf, dlb, n1):
    nh = n1 // 2
    groups = FILT_TILE // n1
    cq = HY_WIDTH // LANES
    full = lambda a: pl.BlockSpec(a.shape, lambda i: (0,) * a.ndim)
    return pl.pallas_call(
        functools.partial(_filter_kernel, groups=groups, nh=nh),
        grid=(LANES // groups,),
        in_specs=[pl.BlockSpec((groups * n1, FILTER_PAD), lambda i: (i, 0))]
        + [full(a) for a in (w1, b1, w2, b2, w3, b3, w4f, w4b, freq, dlf, dlb)],
        out_specs=[pl.BlockSpec((2, cq, groups, n1, LANES), lambda i: (0, 0, i, 0, 0)),
                   pl.BlockSpec((1, 2 * HY_WIDTH), lambda i: (0, 0))],
        out_shape=[jax.ShapeDtypeStruct((2, cq, LANES, n1, LANES), BF16),
                   jax.ShapeDtypeStruct((1, 2 * HY_WIDTH), F32)],
        compiler_params=_params("arbitrary"),
        name="filter_taps",
    )(zp, w1, b1, w2, b2, w3, b3, w4f, w4b, freq, dlf, dlb)


def _store_spectrum(a_ref, y, j, n1):
    for ri in range(2):
        for c, blk in enumerate(_lane_blocks(y, slice(ri * n1, (ri + 1) * n1), n1 // SUBLANES)):
            a_ref[ri, c, :, j, :, :] = blk


def _fft_outer_filter_kernel(t_ref, f_ref, a_ref, *, jb, n1):
    for j in range(jb):
        x = jnp.concatenate([t_ref[c, j] for c in range(HY_WIDTH // LANES)], axis=1)
        _store_spectrum(a_ref, _dot(f_ref[...], x), j, n1)


def _fft_outer_filter(taps, f1r):
    n_ord, cq, _, n1, _ = taps.shape
    jb = SUBLANES
    kbn = n1 // SUBLANES
    return pl.pallas_call(
        functools.partial(_fft_outer_filter_kernel, jb=jb, n1=n1),
        grid=(n_ord, LANES // jb),
        in_specs=[pl.BlockSpec((None, cq, jb, n1, LANES), lambda o, j: (o, 0, j, 0, 0)),
                  pl.BlockSpec(f1r.shape, lambda o, j: (0, 0))],
        out_specs=pl.BlockSpec((None, 2, cq, kbn, jb, SUBLANES, LANES), lambda o, j: (o, 0, 0, 0, j, 0, 0)),
        out_shape=jax.ShapeDtypeStruct((n_ord, 2, cq, kbn, LANES, SUBLANES, LANES), F32),
        compiler_params=_params("parallel", "parallel"),
        name="fft_outer_filter",
    )(taps, f1r)


def _load_spectrum(a_ref, kk):
    return jnp.concatenate([_strided_slab(a_ref, (ri,), kk, a_ref.shape[-2] // SUBLANES) for ri in range(2)],
                           axis=0).astype(BF16)


def _filter_spectrum_kernel(a_ref, gf_ref, s_ref, h_ref, *, n_fft):
    scale = 1.0 / (s_ref[...] * n_fft)
    for kk in range(SUBLANES):
        h_ref[kk] = _dot(gf_ref[kk], _load_spectrum(a_ref, kk)) * scale


def _filter_spectrum(af, gf, s, n_fft):
    n_ord, _, cq, kbn, rows, _ = af.shape
    n1 = kbn * SUBLANES
    return pl.pallas_call(
        functools.partial(_filter_spectrum_kernel, n_fft=n_fft),
        grid=(n_ord, kbn),
        in_specs=[pl.BlockSpec((None, 2, cq, None, rows, LANES), lambda o, k: (o, 0, 0, k, 0, 0)),
                  pl.BlockSpec((SUBLANES, 2 * LANES, 2 * LANES), lambda o, k: (k, 0, 0)),
                  pl.BlockSpec((None, 1, HY_WIDTH), lambda o, k: (o, 0, 0))],
        out_specs=pl.BlockSpec((None, SUBLANES, 2 * LANES, HY_WIDTH), lambda o, k: (o, k, 0, 0)),
        out_shape=jax.ShapeDtypeStruct((n_ord, n1, 2 * LANES, HY_WIDTH), F32),
        compiler_params=_params("parallel", "parallel"),
        name="filter_spectrum",
    )(af, gf, s)


def _time_rows(z_ref, b, c, j, nh):
    return z_ref[b, c, :, j].reshape(nh, LANES)


def _fft_outer_kernel(z_ref, f_ref, a_ref, *, jb, n1):
    nh = n1 // 2
    for j in range(jb):
        x = jnp.concatenate(
            [jnp.concatenate([_time_rows(z_ref, b, c, j, nh) for c in range(HY_WIDTH // LANES)], axis=1)
             for b in range(2)], axis=0).astype(BF16)
        _store_spectrum(a_ref, _dot(f_ref[...], x), j, n1)


def _fft_outer(z, f1):
    _, cq, na, _, _, _ = z.shape
    n1 = f1.shape[0] // 2
    jb = SUBLANES
    kbn = n1 // SUBLANES
    return pl.pallas_call(
        functools.partial(_fft_outer_kernel, jb=jb, n1=n1),
        grid=(LANES // jb,),
        in_specs=[pl.BlockSpec((2, cq, na, jb, SUBLANES, LANES), lambda j: (0, 0, 0, j, 0, 0)),
                  pl.BlockSpec(f1.shape, lambda j: (0, 0))],
        out_specs=pl.BlockSpec((2, cq, kbn, jb, SUBLANES, LANES), lambda j: (0, 0, 0, j, 0, 0)),
        out_shape=jax.ShapeDtypeStruct((2, cq, kbn, LANES, SUBLANES, LANES), F32),
        compiler_params=_params("parallel"),
        name="fft_outer",
    )(z, f1)


def _fft_inner_kernel(a_ref, gf_ref, gi_ref, h_ref, b_ref):
    for kk in range(SUBLANES):
        c = _dot(gf_ref[kk], _load_spectrum(a_ref, kk))
        cr, ci = c[:LANES], c[LANES:]
        h = h_ref[kk]
        hr, hi = h[:LANES], h[LANES:]
        p = jnp.concatenate([cr * hr - ci * hi, cr * hi + ci * hr], axis=0).astype(BF16)
        y = _dot(gi_ref[kk], p)
        for ri in range(2):
            for cq, blk in enumerate(_lane_blocks(y, slice(ri * LANES, (ri + 1) * LANES), LANES // SUBLANES)):
                b_ref[ri, cq, :, kk, :, :] = blk


def _fft_inner(a, gf, gi, h, order):
    _, cq, kbn, rows, _ = a.shape
    n1 = kbn * SUBLANES
    jbn = LANES // SUBLANES
    g_spec = pl.BlockSpec((SUBLANES, 2 * LANES, 2 * LANES), lambda k: (k, 0, 0))
    return pl.pallas_call(
        _fft_inner_kernel,
        grid=(kbn,),
        in_specs=[pl.BlockSpec((2, cq, None, rows, LANES), lambda k: (0, 0, k, 0, 0)), g_spec, g_spec,
                  pl.BlockSpec((None, SUBLANES, 2 * LANES, HY_WIDTH), lambda k: (order, k, 0, 0))],
        out_specs=pl.BlockSpec((2, cq, jbn, SUBLANES, SUBLANES, LANES), lambda k: (0, 0, 0, k, 0, 0)),
        out_shape=jax.ShapeDtypeStruct((2, cq, jbn, n1, SUBLANES, LANES), F32),
        compiler_params=_params("parallel"),
        name="fft_inner",
    )(a, gf, gi, h)


def _ifft_outer_gate_kernel(b_ref, f_ref, z_ref, g_ref, skip_ref, o_ref, *, jb, nh):
    skip = skip_ref[...]
    for j in range(jb):
        y = _dot(f_ref[...], _load_spectrum(b_ref, j))
        for b in range(2):
            for c in range(HY_WIDTH // LANES):
                cs = slice(c * LANES, (c + 1) * LANES)
                val = _time_rows(g_ref, b, c, j, nh) * (y[b * nh:(b + 1) * nh, cs]
                                                        + skip[:, cs] * _time_rows(z_ref, b, c, j, nh))
                o_ref[b, c, :, j, :, :] = val.reshape(nh // SUBLANES, SUBLANES, LANES)


def _ifft_outer_gate(bsp, f1inv, z, gate, skip_row):
    _, cq, na, _, _, _ = z.shape
    nh = na * SUBLANES
    jb = SUBLANES
    t_spec = pl.BlockSpec((2, cq, na, jb, SUBLANES, LANES), lambda j: (0, 0, 0, j, 0, 0))
    return pl.pallas_call(
        functools.partial(_ifft_outer_gate_kernel, jb=jb, nh=nh),
        grid=(LANES // jb,),
        in_specs=[pl.BlockSpec((2, cq, None, bsp.shape[-2], LANES), lambda j: (0, 0, j, 0, 0)),
                  pl.BlockSpec(f1inv.shape, lambda j: (0, 0)), t_spec, t_spec,
                  pl.BlockSpec((1, HY_WIDTH), lambda j: (0, 0))],
        out_specs=t_spec,
        out_shape=jax.ShapeDtypeStruct(z.shape, F32),
        compiler_params=_params("parallel"),
        name="ifft_outer_gate",
    )(bsp, f1inv, z, gate, skip_row)


def _dft_tables(seq):
    n = 2 * seq
    n1 = n // LANES
    nh = n1 // 2
    k1 = jnp.arange(n1, dtype=jnp.int32)
    ang = ((k1[:, None] * k1[None, :]) % n1).astype(F32) * (2.0 * math.pi / n1)
    fc, fs = jnp.cos(ang), jnp.sin(ang)
    f1 = jnp.concatenate([jnp.concatenate([fc[:, :nh], fs[:, :nh]], 1),
                          jnp.concatenate([-fs[:, :nh], fc[:, :nh]], 1)], 0).astype(BF16)
    f1r = jnp.concatenate([fc, -fs], 0).astype(BF16)
    f1inv = jnp.concatenate([jnp.concatenate([fc[:nh], -fs[:nh]], 1),
                             jnp.concatenate([fs[:nh], fc[:nh]], 1)], 0).astype(BF16)
    k2 = jnp.arange(LANES, dtype=jnp.int32)
    k = k1[:, None, None] + n1 * k2[None, :, None]
    th = ((k * k2[None, None, :]) % n).astype(F32) * (2.0 * math.pi / n)
    c, s = jnp.cos(th), jnp.sin(th)
    gf = jnp.concatenate([jnp.concatenate([c, s], 2), jnp.concatenate([-s, c], 2)], 1).astype(BF16)
    gi = jnp.swapaxes(gf, 1, 2)
    return f1, f1r, f1inv, gf, gi


def _filter_features(seq, groups):
    t = jnp.linspace(0.0, 1.0, seq, dtype=F32)[:, None]
    bands = (FILTER_EMB - 1) // 2
    w_ang = 2.0 * math.pi * jnp.arange(seq, dtype=F32) / seq
    band_f = jnp.linspace(1e-4, bands - 1, bands, dtype=F32)
    ang = w_ang[:, None] * band_f[None, :]
    z = jnp.concatenate([t, jnp.cos(ang), -jnp.sin(ang)], axis=-1)
    z = jnp.pad(z, ((0, 0), (0, FILTER_PAD - FILTER_EMB)))
    z_all = jnp.concatenate([z, z[:1], z[:0:-1]], axis=0)
    n1 = 2 * seq // LANES
    z_all = z_all.reshape(2, n1 // 2, LANES // groups, groups, FILTER_PAD)
    return z_all.transpose(2, 0, 3, 1, 4).reshape(2 * seq, FILTER_PAD)


def _hyena(hv, x1, x2, fw1, fb1, fw2, fb2, fw3, fb3, fw4, ffreq, fdeltas, skip_d):
    b, cq, na, rows, _ = hv.shape
    assert b == 2, "the two batch rows ride as the real and imaginary parts of one transform"
    seq = na * rows
    n = 2 * seq
    n1 = n // LANES
    f1, f1r, f1inv, gf, gi = _dft_tables(seq)

    groups = FILT_TILE // n1
    zp = _filter_features(seq, groups)
    w1p = jnp.pad(fw1, ((0, FILTER_PAD - FILTER_EMB), (0, 0)))
    pick = lambda a, d: a.reshape(a.shape[0], 2, 2, HY_WIDTH)[:, :, d].reshape(a.shape[0], 2 * HY_WIDTH)
    dl = fdeltas[None]
    taps, s = _filter_taps(zp, w1p, fb1[None], fw2, fb2[None], fw3, fb3[None], pick(fw4, 0), pick(fw4, 1),
                           ffreq[None], pick(dl, 0), pick(dl, 1), n1)
    af = _fft_outer_filter(taps, f1r)
    h = _filter_spectrum(af.reshape(af.shape[:4] + (LANES * SUBLANES, LANES)), gf, s.reshape(2, 1, HY_WIDTH), n)

    split = lambda a: a.reshape(b, cq, na, LANES, SUBLANES, LANES)
    zz = split(hv)
    for order, gate in enumerate((x1, x2)):
        a = _fft_outer(zz, f1)
        bsp = _fft_inner(a.reshape(a.shape[:3] + (LANES * SUBLANES, LANES)), gf, gi, h, order)
        zz = _ifft_outer_gate(bsp.reshape(bsp.shape[:3] + (n1 * SUBLANES, LANES)), f1inv, zz, split(gate),
                              skip_d[order][None])
    return zz.reshape(hv.shape)


def _out_mlp_kernel(x_ref, att_ref, hy_ref, ga_ref, gh_ref, wo_ref, g2_ref, w1_ref, w2_ref, gf_ref, o_ref):
    half = att_ref.shape[1]
    per_pass = MLP_ROWS // LANES
    for r in range(x_ref.shape[0] // MLP_ROWS):
        rows = slice(r * MLP_ROWS, (r + 1) * MLP_ROWS)
        hy = jnp.concatenate([_strided_slab(hy_ref, (), p, LANES) for p in range(r * per_pass, (r + 1) * per_pass)],
                             axis=0)
        a = _rms(att_ref[rows, :], ga_ref[...]).astype(BF16)
        hyn = _rms(hy, gh_ref[...]).astype(BF16)
        h = x_ref[rows, :] + _dot(a, wo_ref[:half, :]) + _dot(hyn, wo_ref[half:, :])
        m = _rms(h, g2_ref[...]).astype(BF16)
        t = jnp.square(jnp.maximum(_dot(m, w1_ref[...]), 0.0)).astype(BF16)
        h = h + _dot(t, w2_ref[...])
        o_ref[rows, :] = _rms(h, gf_ref[...])


def _out_mlp(x, att, hy, ga, gh, w_out, g2, w1, w2, gfin):
    b, seq, d = x.shape
    tm = ROW_TILE
    row = lambda w: pl.BlockSpec((None, tm, w), lambda bi, i: (bi, i, 0))
    full = lambda a: pl.BlockSpec(a.shape, lambda bi, i: (0,) * a.ndim, pipeline_mode=pl.Buffered(1))
    return pl.pallas_call(
        _out_mlp_kernel,
        grid=(b, seq // tm),
        in_specs=[row(d), row(att.shape[-1]),
                  pl.BlockSpec((None, hy.shape[1], None, tm, LANES), lambda bi, i: (bi, 0, i, 0, 0)),
                  full(ga), full(gh), full(w_out), full(g2), full(w1), full(w2), full(gfin)],
        out_specs=row(d),
        out_shape=jax.ShapeDtypeStruct((b, seq, d), F32),
        compiler_params=_params("parallel", "parallel"),
        name="out_mlp",
    )(x, att, hy, ga, gh, w_out, g2, w1, w2, gfin)


def _rope_tables(seq):
    rows = seq // GRID_W
    row = jnp.repeat(jnp.arange(rows, dtype=F32), GRID_W)
    col = jnp.tile(jnp.arange(GRID_W, dtype=F32), rows)
    half = HEAD_DIM // 2
    inv_freq = ROPE_THETA ** (-jnp.arange(0, half, 2, dtype=F32) / half)
    ang_r = row[:, None] * inv_freq[None, :]
    ang_c = col[:, None] * inv_freq[None, :]
    cos = jnp.concatenate([jnp.cos(ang_r)] * 2 + [jnp.cos(ang_c)] * 2, axis=1)
    sin = jnp.concatenate([-jnp.sin(ang_r), jnp.sin(ang_r), -jnp.sin(ang_c), jnp.sin(ang_c)], axis=1)
    return jnp.tile(cos, (1, LANES // HEAD_DIM)), jnp.tile(sin, (1, LANES // HEAD_DIM))


def _head_mean_matrix(width):
    head = jnp.arange(width, dtype=jnp.int32) // HEAD_DIM
    return jnp.where(head[:, None] == head[None, :], 1.0 / HEAD_DIM, 0.0).astype(BF16)


def kernel(x, norm1_g, w_in, q_norm_g, k_norm_g, hy_conv_w, hy_conv_b, filt_w1, filt_b1, filt_w2, filt_b2, filt_w3, filt_b3, filt_w4, filt_freq, filt_deltas, hy_skip_d, attn_out_g, hy_out_g, w_out, norm2_g, w_mlp_in, w_mlp_out, final_g):
    seq = x.shape[1]
    cos_t, sin_t = _rope_tables(seq)
    bdq = _head_mean_matrix(Q_WIDTH)
    bdk = _head_mean_matrix(KV_WIDTH)
    h = x
    for i in range(norm1_g.shape[0]):
        w_qkv = w_in[i][:, :Q_WIDTH + 2 * KV_WIDTH].astype(BF16)
        w_u = w_in[i][:, Q_WIDTH + 2 * KV_WIDTH:].astype(BF16)
        bound = (1.01 * HEAD_DIM ** 0.5 * LOG2E) * jnp.max(jnp.abs(q_norm_g[i])) * jnp.max(jnp.abs(k_norm_g[i]))
        q, k, v, hv, x1, x2 = _in_proj(
            h, norm1_g[i][None], w_qkv, w_u, jnp.tile(q_norm_g[i], N_Q_HEADS)[None],
            jnp.tile(k_norm_g[i], N_KV_HEADS)[None], cos_t, sin_t, bdq, bdk, hy_conv_w[i], hy_conv_b[i][None],
            bound.reshape(1, 1))
        att = lax.cond(
            bound <= MAX_FIXED_SHIFT,
            functools.partial(_attention, _attn_bounded_kernel, KV_CHUNK_BOUNDED, "attention_bounded"),
            functools.partial(_attention, _attn_online_kernel, KV_CHUNK, "attention_online"),
            q, k, v)
        hy = _hyena(hv, x1, x2, filt_w1[i], filt_b1[i], filt_w2[i], filt_b2[i], filt_w3[i], filt_b3[i],
                    filt_w4[i], filt_freq[i], filt_deltas[i], hy_skip_d[i])
        last = i == norm1_g.shape[0] - 1
        assert last, "single-layer trunk"
        h = _out_mlp(h, att, hy, attn_out_g[i][None], hy_out_g[i][None], w_out[i].astype(BF16), norm2_g[i][None],
                     w_mlp_in[i].astype(BF16), w_mlp_out[i].astype(BF16), final_g[None])
    return h
```

```python
import functools
import math

import jax
import jax.numpy as jnp
from jax import lax
from jax.experimental import pallas as pl
from jax.experimental.pallas import tpu as pltpu

F32 = jnp.float32
BF16 = jnp.bfloat16

HEAD_DIM = 64
N_Q_HEADS = 8
N_KV_HEADS = 2
Q_WIDTH = N_Q_HEADS * HEAD_DIM
KV_WIDTH = N_KV_HEADS * HEAD_DIM
HY_WIDTH = 512
GRID_W = 64
ROPE_THETA = 10000.0
FILTER_EMB = 33
FILTER_PAD = 64
EPS = 1e-6
LOG2E = math.log2(math.e)
MAX_FIXED_SHIFT = 50.0

LANES = 128
SUBLANES = 8
TX_ROWS = SUBLANES * LANES
HALO = 16
VMEM_LIMIT = 56 * 1024 * 1024

ROW_TILE = TX_ROWS
MLP_ROWS = 512
Q_TILE = 128
KV_CHUNK = 512
KV_CHUNK_BOUNDED = 8192
FILT_TILE = 512


def _dot(a, b):
    return jnp.dot(a, b, preferred_element_type=F32)


def _dot3(a, b):
    ah = a.astype(BF16)
    al = (a - ah.astype(F32)).astype(BF16)
    bh = b.astype(BF16)
    bl = (b - bh.astype(F32)).astype(BF16)
    return _dot(ah, bh) + (_dot(ah, bl) + _dot(al, bh))


def _rms(v, g):
    return v * lax.rsqrt(jnp.mean(v * v, axis=-1, keepdims=True) + EPS) * g


def _params(*sem):
    return pltpu.CompilerParams(dimension_semantics=sem, vmem_limit_bytes=VMEM_LIMIT)


def _in_proj_kernel(x_ref, xp_ref, xn_ref, g1_ref, wqkv_ref, wu_ref, qg_ref, kg_ref, cos_ref, sin_ref,
                    bdq_ref, bdk_ref, cw_ref, cb_ref, bound_ref,
                    q_ref, k_ref, v_ref, hv_ref, x1_ref, x2_ref, a_scr, *, tm, n_tiles):
    i = pl.program_id(1)
    g1 = g1_ref[...]
    a_scr[HALO:HALO + tm, :] = _rms(x_ref[...], g1).astype(BF16)
    prev = jnp.where(i > 0, _rms(xp_ref[...], g1), 0.0)
    nxt = jnp.where(i < n_tiles - 1, _rms(xn_ref[...], g1), 0.0)
    a_scr[0:HALO, :] = prev.astype(BF16)
    a_scr[HALO + tm:, :] = nxt.astype(BF16)

    qkv = _dot(a_scr[HALO:HALO + tm, :], wqkv_ref[...])
    cos = cos_ref[...]
    sin = sin_ref[...]
    lane = lax.broadcasted_iota(jnp.int32, (tm, LANES), 1)
    first16 = (lane % 32) < 16
    low_half = lane < HEAD_DIM

    def norm_rope(blk, ms, gain):
        y = blk * lax.rsqrt(ms + EPS) * gain
        partner = jnp.where(first16, pltpu.roll(y, LANES - 16, 1), pltpu.roll(y, 16, 1))
        return y * cos + partner * sin

    q = qkv[:, :Q_WIDTH]
    q_ms = _dot((q * q).astype(BF16), bdq_ref[...])
    qg = qg_ref[...]
    neg_bound = -bound_ref[0, 0]
    extra = lane == HEAD_DIM
    for m in range(Q_WIDTH // LANES):
        sl = slice(m * LANES, (m + 1) * LANES)
        blk = norm_rope(q[:, sl], q_ms[:, sl], qg[:, sl]) * (HEAD_DIM ** -0.5 * LOG2E)
        q_ref[:, 2 * m * LANES:(2 * m + 1) * LANES] = jnp.where(extra, neg_bound, blk).astype(BF16)
        q_ref[:, (2 * m + 1) * LANES:(2 * m + 2) * LANES] = jnp.where(
            extra, neg_bound, pltpu.roll(blk, HEAD_DIM, 1)).astype(BF16)

    k = qkv[:, Q_WIDTH:Q_WIDTH + KV_WIDTH]
    k_ms = _dot((k * k).astype(BF16), bdk_ref[...])
    k = norm_rope(k, k_ms, kg_ref[...])
    v = qkv[:, Q_WIDTH + KV_WIDTH:]
    pad = jnp.where(extra, 1.0, 0.0)
    k_ref[0] = jnp.where(low_half, k, pad).astype(BF16)
    k_ref[1] = jnp.where(low_half, pltpu.roll(k, HEAD_DIM, 1), pad).astype(BF16)
    v_ref[0] = jnp.where(low_half, v, pad).astype(BF16)
    v_ref[1] = jnp.where(low_half, pltpu.roll(v, HEAD_DIM, 1), pad).astype(BF16)

    rows = tm + 2 * HALO
    for c, out in enumerate((hv_ref, x1_ref, x2_ref)):
        cs = slice(c * HY_WIDTH, (c + 1) * HY_WIDTH)
        u = _dot(a_scr[...], wu_ref[:, cs])
        w = cw_ref[:, cs]
        y = (pltpu.roll(u, 1, 0) * w[0:1] + u * w[1:2] + pltpu.roll(u, rows - 1, 0) * w[2:3]) + cb_ref[:, cs]
        for cq in range(HY_WIDTH // LANES):
            for p in range(SUBLANES):
                out[cq, pl.ds(p, LANES, stride=SUBLANES), :] = y[HALO + p * LANES:HALO + (p + 1) * LANES,
                                                                 cq * LANES:(cq + 1) * LANES]


def _in_proj(x, g1, w_qkv, w_u, qg, kg, cos_t, sin_t, bdq, bdk, conv_w, conv_b, bound):
    b, seq, d = x.shape
    tm = ROW_TILE
    n_tiles = seq // tm
    hb = tm // HALO
    cq = HY_WIDTH // LANES
    full = lambda shape: pl.BlockSpec(shape, lambda bi, i: (0,) * len(shape))
    row_out = lambda w: pl.BlockSpec((None, tm, w), lambda bi, i: (bi, i, 0))
    kv_out = pl.BlockSpec((None, N_KV_HEADS, tm, LANES), lambda bi, i: (bi, 0, i, 0))
    tx_out = pl.BlockSpec((None, cq, None, tm, LANES), lambda bi, i: (bi, 0, i, 0, 0))
    tx_shape = jax.ShapeDtypeStruct((b, cq, n_tiles, tm, LANES), F32)
    return pl.pallas_call(
        functools.partial(_in_proj_kernel, tm=tm, n_tiles=n_tiles),
        grid=(b, n_tiles),
        in_specs=[
            pl.BlockSpec((None, tm, d), lambda bi, i: (bi, i, 0)),
            pl.BlockSpec((None, HALO, d), lambda bi, i: (bi, jnp.maximum(i * hb - 1, 0), 0)),
            pl.BlockSpec((None, HALO, d), lambda bi, i: (bi, jnp.minimum((i + 1) * hb, seq // HALO - 1), 0)),
            full((1, d)),
            full(w_qkv.shape),
            full(w_u.shape),
            full((1, Q_WIDTH)),
            full((1, KV_WIDTH)),
            pl.BlockSpec((tm, LANES), lambda bi, i: (i, 0)),
            pl.BlockSpec((tm, LANES), lambda bi, i: (i, 0)),
            full(bdq.shape),
            full(bdk.shape),
            full(conv_w.shape),
            full(conv_b.shape),
            pl.BlockSpec(memory_space=pltpu.SMEM),
        ],
        out_specs=[row_out(2 * Q_WIDTH), kv_out, kv_out, tx_out, tx_out, tx_out],
        out_shape=[
            jax.ShapeDtypeStruct((b, seq, 2 * Q_WIDTH), BF16),
            jax.ShapeDtypeStruct((b, N_KV_HEADS, seq, LANES), BF16),
            jax.ShapeDtypeStruct((b, N_KV_HEADS, seq, LANES), BF16),
            tx_shape, tx_shape, tx_shape,
        ],
        scratch_shapes=[pltpu.VMEM((tm + 2 * HALO, d), BF16)],
        compiler_params=_params("parallel", "arbitrary"),
        name="in_proj",
    )(x, x, x, g1, w_qkv, w_u, qg, kg, cos_t, sin_t, bdq, bdk, conv_w, conv_b, bound)


def _stack_heads(q_ref):
    group = N_Q_HEADS // N_KV_HEADS
    return jnp.concatenate([q_ref[:, g * LANES:(g + 1) * LANES] for g in range(group)], axis=0)


def _store_heads(o_ref, o, tq):
    group = N_Q_HEADS // N_KV_HEADS
    pairs = [o[(2 * p) * tq:(2 * p + 1) * tq] + pltpu.roll(o[(2 * p + 1) * tq:(2 * p + 2) * tq], HEAD_DIM, 1)
             for p in range(group // 2)]
    o_ref[...] = jnp.concatenate(pairs, axis=1)


def _attn_bounded_kernel(q_ref, k_ref, v_ref, o_ref, *, tq, sc, n_chunks):
    q = _stack_heads(q_ref)
    rows = q.shape[0]

    def body(c, acc):
        off = pl.multiple_of(c * sc, sc)
        s = lax.dot_general(q, k_ref[pl.ds(off, sc), :], (((1,), (1,)), ((), ())), preferred_element_type=F32)
        return acc + _dot(jnp.exp2(s).astype(BF16), v_ref[pl.ds(off, sc), :])

    acc = lax.fori_loop(0, n_chunks, body, jnp.zeros((rows, LANES), F32))
    lane = lax.broadcasted_iota(jnp.int32, (rows, LANES), 1)
    _store_heads(o_ref, jnp.where(lane < HEAD_DIM, acc / acc[:, HEAD_DIM:HEAD_DIM + 1], 0.0), tq)


def _attn_online_kernel(q_ref, k_ref, v_ref, o_ref, *, tq, sc, n_chunks):
    q = _stack_heads(q_ref)
    rows = q.shape[0]

    def body(c, carry):
        m, l, acc = carry
        off = pl.multiple_of(c * sc, sc)
        s = lax.dot_general(q, k_ref[pl.ds(off, sc), :], (((1,), (1,)), ((), ())), preferred_element_type=F32)
        m_new = jnp.maximum(m, jnp.max(s, axis=1, keepdims=True))
        p = jnp.exp2(s - m_new)
        alpha = jnp.exp2(m - m_new)
        l = alpha * l + jnp.sum(p, axis=1, keepdims=True)
        acc = alpha * acc + _dot(p.astype(BF16), v_ref[pl.ds(off, sc), :])
        return m_new, l, acc

    init = (jnp.full((rows, 1), jnp.finfo(F32).min, F32), jnp.zeros((rows, 1), F32), jnp.zeros((rows, LANES), F32))
    _, l, acc = lax.fori_loop(0, n_chunks, body, init)
    lane = lax.broadcasted_iota(jnp.int32, (rows, LANES), 1)
    _store_heads(o_ref, jnp.where(lane < HEAD_DIM, acc / l, 0.0), tq)


def _attention(body, sc, name, q, k, v):
    b, seq, _ = q.shape
    tq = min(Q_TILE, seq)
    sc = min(sc, seq)
    group = N_Q_HEADS // N_KV_HEADS
    return pl.pallas_call(
        functools.partial(body, tq=tq, sc=sc, n_chunks=seq // sc),
        grid=(b, N_KV_HEADS, seq // tq),
        in_specs=[
            pl.BlockSpec((None, tq, group * LANES), lambda bi, j, i: (bi, i, j)),
            pl.BlockSpec((None, None, seq, LANES), lambda bi, j, i: (bi, j, 0, 0)),
            pl.BlockSpec((None, None, seq, LANES), lambda bi, j, i: (bi, j, 0, 0)),
        ],
        out_specs=pl.BlockSpec((None, tq, group * HEAD_DIM), lambda bi, j, i: (bi, i, j)),
        out_shape=jax.ShapeDtypeStruct((b, seq, Q_WIDTH), F32),
        compiler_params=_params("parallel", "parallel", "arbitrary"),
        name=name,
    )(q, k, v)


def _lane_blocks(y, rows, n_lead):
    return [y[rows, c * LANES:(c + 1) * LANES].reshape(n_lead, SUBLANES, LANES) for c in range(HY_WIDTH // LANES)]


def _strided_slab(ref, lead, start, rows):
    return jnp.concatenate([ref[lead + (c, pl.ds(start, rows, stride=SUBLANES), slice(None))]
                            for c in range(HY_WIDTH // LANES)], axis=1)


def _filter_kernel(z_ref, w1_ref, b1_ref, w2_ref, b2_ref, w3_ref, b3_ref, w4_ref, fr_ref, dl_ref,
                   taps_ref, s_ref, *, nh):
    i = pl.program_id(0)
    z = z_ref[...]
    fr = fr_ref[...]
    h = jnp.sin(fr * (_dot3(z, w1_ref[...]) + b1_ref[...]))
    h = jnp.sin(fr * (_dot3(h, w2_ref[...]) + b2_ref[...]))
    h = jnp.sin(fr * (_dot3(h, w3_ref[...]) + b3_ref[...]))
    h = _dot(h.astype(BF16), w4_ref[...].astype(BF16)) * jnp.exp(-z[:, 0:1] * jnp.abs(dl_ref[...]))
    first = (lax.broadcasted_iota(jnp.int32, (h.shape[0], HY_WIDTH), 0) == 0) & (i == 0)
    sums = []
    for od in range(4):
        hod = h[:, od * HY_WIDTH:(od + 1) * HY_WIDTH]
        if od % 2 == 1:
            hod = jnp.where(first, 0.0, hod)
        sums.append(jnp.sum(jnp.abs(hod), axis=0, keepdims=True))
        for c in range(HY_WIDTH // LANES):
            for g in range(h.shape[0] // nh):
                taps_ref[od // 2, od % 2, c, g] = hod[g * nh:(g + 1) * nh, c * LANES:(c + 1) * LANES].astype(BF16)
    ssum = jnp.concatenate(sums, axis=1)

    @pl.when(i == 0)
    def _():
        s_ref[...] = ssum

    @pl.when(i > 0)
    def _():
        s_ref[...] += ssum


def _filter_taps(zt, w1, b1, w2, b2, w3, b3, w4, freq, deltas):
    seq = zt.shape[0]
    nh = seq // LANES
    groups = FILT_TILE // nh
    cq = HY_WIDTH // LANES
    full = lambda a: pl.BlockSpec(a.shape, lambda i: (0,) * a.ndim)
    return pl.pallas_call(
        functools.partial(_filter_kernel, nh=nh),
        grid=(seq // FILT_TILE,),
        in_specs=[pl.BlockSpec((FILT_TILE, FILTER_PAD), lambda i: (i, 0))]
        + [full(a) for a in (w1, b1, w2, b2, w3, b3, w4, freq, deltas)],
        out_specs=[pl.BlockSpec((2, 2, cq, groups, nh, LANES), lambda i: (0, 0, 0, i, 0, 0)),
                   pl.BlockSpec((1, 4 * HY_WIDTH), lambda i: (0, 0))],
        out_shape=[jax.ShapeDtypeStruct((2, 2, cq, LANES, nh, LANES), BF16),
                   jax.ShapeDtypeStruct((1, 4 * HY_WIDTH), F32)],
        compiler_params=_params("arbitrary"),
        name="filter_taps",
    )(zt, w1, b1, w2, b2, w3, b3, w4, freq, deltas)


def _store_spectrum(a_ref, y, j, n1):
    for ri in range(2):
        for c, blk in enumerate(_lane_blocks(y, slice(ri * n1, (ri + 1) * n1), n1 // SUBLANES)):
            a_ref[ri, c, :, j, :, :] = blk


def _fft_outer_filter_kernel(fw_ref, bw_ref, bw0_ref, f_ref, f0_ref, a_ref, *, jb, n1):
    first_block = pl.program_id(1) == 0
    lanes = range(HY_WIDTH // LANES)
    for j in range(jb):
        bw = [bw0_ref[c, 0] for c in lanes] if j == 0 else [bw_ref[c, jb - j] for c in lanes]
        x = jnp.concatenate([jnp.concatenate([fw_ref[c, j] for c in lanes], axis=1),
                             jnp.concatenate(bw, axis=1)], axis=0)
        mat = jnp.where(first_block, f0_ref[...], f_ref[...]) if j == 0 else f_ref[...]
        _store_spectrum(a_ref, _dot(mat, x), j, n1)


def _fft_outer_filter(taps, f1t, f1t0):
    n_ord, _, cq, _, nh, _ = taps.shape
    n1 = 2 * nh
    jb = SUBLANES
    jbn = LANES // jb
    kbn = n1 // SUBLANES
    tap_spec = lambda d, blk: pl.BlockSpec((None, None, cq, jb, nh, LANES), lambda o, j: (o, d, 0, blk(j), 0, 0))
    return pl.pallas_call(
        functools.partial(_fft_outer_filter_kernel, jb=jb, n1=n1),
        grid=(n_ord, jbn),
        in_specs=[tap_spec(0, lambda j: j), tap_spec(1, lambda j: jbn - 1 - j), tap_spec(1, lambda j: (jbn - j) % jbn),
                  pl.BlockSpec(f1t.shape, lambda o, j: (0, 0)), pl.BlockSpec(f1t0.shape, lambda o, j: (0, 0))],
        out_specs=pl.BlockSpec((None, 2, cq, kbn, jb, SUBLANES, LANES), lambda o, j: (o, 0, 0, 0, j, 0, 0)),
        out_shape=jax.ShapeDtypeStruct((n_ord, 2, cq, kbn, LANES, SUBLANES, LANES), F32),
        compiler_params=_params("parallel", "parallel"),
        name="fft_outer_filter",
    )(taps, taps, taps, f1t, f1t0)


def _load_spectrum(a_ref, kk):
    return jnp.concatenate([_strided_slab(a_ref, (ri,), kk, a_ref.shape[-2] // SUBLANES) for ri in range(2)],
                           axis=0).astype(BF16)


def _time_rows(z_ref, b, c, j, nh):
    return z_ref[b, c, :, j].reshape(nh, LANES)


def _fft_outer_kernel(z_ref, f_ref, a_ref, *, jb, n1):
    nh = n1 // 2
    for j in range(jb):
        x = jnp.concatenate(
            [jnp.concatenate([_time_rows(z_ref, b, c, j, nh) for c in range(HY_WIDTH // LANES)], axis=1)
             for b in range(2)], axis=0).astype(BF16)
        _store_spectrum(a_ref, _dot(f_ref[...], x), j, n1)


def _fft_outer(z, f1):
    _, cq, na, _, _, _ = z.shape
    n1 = f1.shape[0] // 2
    jb = SUBLANES
    kbn = n1 // SUBLANES
    return pl.pallas_call(
        functools.partial(_fft_outer_kernel, jb=jb, n1=n1),
        grid=(LANES // jb,),
        in_specs=[pl.BlockSpec((2, cq, na, jb, SUBLANES, LANES), lambda j: (0, 0, 0, j, 0, 0)),
                  pl.BlockSpec(f1.shape, lambda j: (0, 0))],
        out_specs=pl.BlockSpec((2, cq, kbn, jb, SUBLANES, LANES), lambda j: (0, 0, 0, j, 0, 0)),
        out_shape=jax.ShapeDtypeStruct((2, cq, kbn, LANES, SUBLANES, LANES), F32),
        compiler_params=_params("parallel"),
        name="fft_outer",
    )(z, f1)


def _fft_inner_kernel(a_ref, af_ref, s_ref, gf_ref, gi_ref, b_ref, *, n_fft):
    scale = 1.0 / ((s_ref[0] + s_ref[1]) * n_fft)
    for kk in range(SUBLANES):
        h = _dot(gf_ref[kk], _load_spectrum(af_ref, kk)) * scale
        c = _dot(gf_ref[kk], _load_spectrum(a_ref, kk))
        cr, ci = c[:LANES], c[LANES:]
        hr, hi = h[:LANES], h[LANES:]
        p = jnp.concatenate([cr * hr - ci * hi, cr * hi + ci * hr], axis=0).astype(BF16)
        y = _dot(gi_ref[kk], p)
        for ri in range(2):
            for cq, blk in enumerate(_lane_blocks(y, slice(ri * LANES, (ri + 1) * LANES), LANES // SUBLANES)):
                b_ref[ri, cq, :, kk, :, :] = blk


def _fft_inner(a, af, s, gf, gi, order, n_fft):
    _, cq, kbn, rows, _ = a.shape
    n1 = kbn * SUBLANES
    jbn = LANES // SUBLANES
    g_spec = pl.BlockSpec((SUBLANES, 2 * LANES, 2 * LANES), lambda k: (k, 0, 0))
    return pl.pallas_call(
        functools.partial(_fft_inner_kernel, n_fft=n_fft),
        grid=(kbn,),
        in_specs=[pl.BlockSpec((2, cq, None, rows, LANES), lambda k: (0, 0, k, 0, 0)),
                  pl.BlockSpec((None, 2, cq, None, rows, LANES), lambda k: (order, 0, 0, k, 0, 0)),
                  pl.BlockSpec((None, 2, 1, HY_WIDTH), lambda k: (order, 0, 0, 0)),
                  g_spec, g_spec],
        out_specs=pl.BlockSpec((2, cq, jbn, SUBLANES, SUBLANES, LANES), lambda k: (0, 0, 0, k, 0, 0)),
        out_shape=jax.ShapeDtypeStruct((2, cq, jbn, n1, SUBLANES, LANES), F32),
        compiler_params=_params("parallel"),
        name="fft_inner",
    )(a, af, s, gf, gi)


def _ifft_outer_gate_kernel(b_ref, f_ref, z_ref, g_ref, skip_ref, o_ref, *, jb, nh):
    skip = skip_ref[...]
    for j in range(jb):
        y = _dot(f_ref[...], _load_spectrum(b_ref, j))
        for b in range(2):
            for c in range(HY_WIDTH // LANES):
                cs = slice(c * LANES, (c + 1) * LANES)
                val = _time_rows(g_ref, b, c, j, nh) * (y[b * nh:(b + 1) * nh, cs]
                                                        + skip[:, cs] * _time_rows(z_ref, b, c, j, nh))
                o_ref[b, c, :, j, :, :] = val.reshape(nh // SUBLANES, SUBLANES, LANES)


def _ifft_outer_gate(bsp, f1inv, z, gate, skip_row):
    _, cq, na, _, _, _ = z.shape
    nh = na * SUBLANES
    jb = SUBLANES
    t_spec = pl.BlockSpec((2, cq, na, jb, SUBLANES, LANES), lambda j: (0, 0, 0, j, 0, 0))
    return pl.pallas_call(
        functools.partial(_ifft_outer_gate_kernel, jb=jb, nh=nh),
        grid=(LANES // jb,),
        in_specs=[pl.BlockSpec((2, cq, None, bsp.shape[-2], LANES), lambda j: (0, 0, j, 0, 0)),
                  pl.BlockSpec(f1inv.shape, lambda j: (0, 0)), t_spec, t_spec,
                  pl.BlockSpec((1, HY_WIDTH), lambda j: (0, 0))],
        out_specs=t_spec,
        out_shape=jax.ShapeDtypeStruct(z.shape, F32),
        compiler_params=_params("parallel"),
        name="ifft_outer_gate",
    )(bsp, f1inv, z, gate, skip_row)


def _dft_tables(seq):
    n = 2 * seq
    n1 = n // LANES
    nh = n1 // 2
    k1 = jnp.arange(n1, dtype=jnp.int32)
    r = jnp.arange(nh, dtype=jnp.int32)

    def outer(cols):
        ang = ((k1[:, None] * cols[None, :]) % n1).astype(F32) * (2.0 * math.pi / n1)
        return jnp.cos(ang), jnp.sin(ang)

    fc, fs = outer(r)
    bc, bs = outer(n1 - 1 - r)
    zc, zs = outer((n1 - r) % n1)
    zc, zs = zc * (r > 0), zs * (r > 0)
    f1 = jnp.concatenate([jnp.concatenate([fc, fs], 1), jnp.concatenate([-fs, fc], 1)], 0).astype(BF16)
    f1t = jnp.concatenate([jnp.concatenate([fc, bc], 1), jnp.concatenate([-fs, -bs], 1)], 0).astype(BF16)
    f1t0 = jnp.concatenate([jnp.concatenate([fc, zc], 1), jnp.concatenate([-fs, -zs], 1)], 0).astype(BF16)
    f1inv = jnp.concatenate([jnp.concatenate([fc.T, -fs.T], 1), jnp.concatenate([fs.T, fc.T], 1)], 0).astype(BF16)
    k2 = jnp.arange(LANES, dtype=jnp.int32)
    k = k1[:, None, None] + n1 * k2[None, :, None]
    th = ((k * k2[None, None, :]) % n).astype(F32) * (2.0 * math.pi / n)
    c, s = jnp.cos(th), jnp.sin(th)
    gf = jnp.concatenate([jnp.concatenate([c, s], 2), jnp.concatenate([-s, c], 2)], 1).astype(BF16)
    gi = jnp.swapaxes(gf, 1, 2)
    return f1, f1t, f1t0, f1inv, gf, gi


def _filter_features(seq):
    nh = seq // LANES
    pos = (jnp.arange(LANES, dtype=jnp.int32)[:, None] + LANES * jnp.arange(nh, dtype=jnp.int32)[None, :]).reshape(seq)
    t = (pos.astype(F32) / (seq - 1))[:, None]
    bands = (FILTER_EMB - 1) // 2
    w_ang = 2.0 * math.pi * pos.astype(F32) / seq
    band_f = jnp.linspace(1e-4, bands - 1, bands, dtype=F32)
    ang = w_ang[:, None] * band_f[None, :]
    z = jnp.concatenate([t, jnp.cos(ang), -jnp.sin(ang)], axis=-1)
    return jnp.pad(z, ((0, 0), (0, FILTER_PAD - FILTER_EMB)))


def _hyena(hv, x1, x2, fw1, fb1, fw2, fb2, fw3, fb3, fw4, ffreq, fdeltas, skip_d):
    b, cq, na, rows, _ = hv.shape
    assert b == 2, "the two batch rows ride as the real and imaginary parts of one transform"
    seq = na * rows
    n = 2 * seq
    n1 = n // LANES
    f1, f1t, f1t0, f1inv, gf, gi = _dft_tables(seq)

    w1p = jnp.pad(fw1, ((0, FILTER_PAD - FILTER_EMB), (0, 0)))
    taps, s = _filter_taps(_filter_features(seq), w1p, fb1[None], fw2, fb2[None], fw3, fb3[None], fw4,
                           ffreq[None], fdeltas[None])
    af = _fft_outer_filter(taps, f1t, f1t0)
    af = af.reshape(af.shape[:4] + (LANES * SUBLANES, LANES))
    s = s.reshape(2, 2, 1, HY_WIDTH)

    split = lambda a: a.reshape(b, cq, na, LANES, SUBLANES, LANES)
    zz = split(hv)
    for order, gate in enumerate((x1, x2)):
        a = _fft_outer(zz, f1)
        bsp = _fft_inner(a.reshape(a.shape[:3] + (LANES * SUBLANES, LANES)), af, s, gf, gi, order, n)
        zz = _ifft_outer_gate(bsp.reshape(bsp.shape[:3] + (n1 * SUBLANES, LANES)), f1inv, zz, split(gate),
                              skip_d[order][None])
    return zz.reshape(hv.shape)


def _out_mlp_kernel(x_ref, att_ref, hy_ref, ga_ref, gh_ref, wo_ref, g2_ref, w1_ref, w2_ref, gf_ref, o_ref):
    half = att_ref.shape[1]
    per_pass = MLP_ROWS // LANES
    for r in range(x_ref.shape[0] // MLP_ROWS):
        rows = slice(r * MLP_ROWS, (r + 1) * MLP_ROWS)
        hy = jnp.concatenate([_strided_slab(hy_ref, (), p, LANES) for p in range(r * per_pass, (r + 1) * per_pass)],
                             axis=0)
        a = _rms(att_ref[rows, :], ga_ref[...]).astype(BF16)
        hyn = _rms(hy, gh_ref[...]).astype(BF16)
        h = x_ref[rows, :] + _dot(a, wo_ref[:half, :]) + _dot(hyn, wo_ref[half:, :])
        m = _rms(h, g2_ref[...]).astype(BF16)
        t = jnp.square(jnp.maximum(_dot(m, w1_ref[...]), 0.0)).astype(BF16)
        h = h + _dot(t, w2_ref[...])
        o_ref[rows, :] = _rms(h, gf_ref[...])


def _out_mlp(x, att, hy, ga, gh, w_out, g2, w1, w2, gfin):
    b, seq, d = x.shape
    tm = ROW_TILE
    row = lambda w: pl.BlockSpec((None, tm, w), lambda bi, i: (bi, i, 0))
    full = lambda a: pl.BlockSpec(a.shape, lambda bi, i: (0,) * a.ndim, pipeline_mode=pl.Buffered(1))
    return pl.pallas_call(
        _out_mlp_kernel,
        grid=(b, seq // tm),
        in_specs=[row(d), row(att.shape[-1]),
                  pl.BlockSpec((None, hy.shape[1], None, tm, LANES), lambda bi, i: (bi, 0, i, 0, 0)),
                  full(ga), full(gh), full(w_out), full(g2), full(w1), full(w2), full(gfin)],
        out_specs=row(d),
        out_shape=jax.ShapeDtypeStruct((b, seq, d), F32),
        compiler_params=_params("parallel", "parallel"),
        name="out_mlp",
    )(x, att, hy, ga, gh, w_out, g2, w1, w2, gfin)


def _rope_tables(seq):
    rows = seq // GRID_W
    row = jnp.repeat(jnp.arange(rows, dtype=F32), GRID_W)
    col = jnp.tile(jnp.arange(GRID_W, dtype=F32), rows)
    half = HEAD_DIM // 2
    inv_freq = ROPE_THETA ** (-jnp.arange(0, half, 2, dtype=F32) / half)
    ang_r = row[:, None] * inv_freq[None, :]
    ang_c = col[:, None] * inv_freq[None, :]
    cos = jnp.concatenate([jnp.cos(ang_r)] * 2 + [jnp.cos(ang_c)] * 2, axis=1)
    sin = jnp.concatenate([-jnp.sin(ang_r), jnp.sin(ang_r), -jnp.sin(ang_c), jnp.sin(ang_c)], axis=1)
    return jnp.tile(cos, (1, LANES // HEAD_DIM)), jnp.tile(sin, (1, LANES // HEAD_DIM))


def _head_mean_matrix(width):
    head = jnp.arange(width, dtype=jnp.int32) // HEAD_DIM
    return jnp.where(head[:, None] == head[None, :], 1.0 / HEAD_DIM, 0.0).astype(BF16)


def kernel(x, norm1_g, w_in, q_norm_g, k_norm_g, hy_conv_w, hy_conv_b, filt_w1, filt_b1, filt_w2, filt_b2, filt_w3, filt_b3, filt_w4, filt_freq, filt_deltas, hy_skip_d, attn_out_g, hy_out_g, w_out, norm2_g, w_mlp_in, w_mlp_out, final_g):
    seq = x.shape[1]
    cos_t, sin_t = _rope_tables(seq)
    bdq = _head_mean_matrix(Q_WIDTH)
    bdk = _head_mean_matrix(KV_WIDTH)
    h = x
    for i in range(norm1_g.shape[0]):
        w_qkv = w_in[i][:, :Q_WIDTH + 2 * KV_WIDTH].astype(BF16)
        w_u = w_in[i][:, Q_WIDTH + 2 * KV_WIDTH:].astype(BF16)
        bound = (1.01 * HEAD_DIM ** 0.5 * LOG2E) * jnp.max(jnp.abs(q_norm_g[i])) * jnp.max(jnp.abs(k_norm_g[i]))
        q, k, v, hv, x1, x2 = _in_proj(
            h, norm1_g[i][None], w_qkv, w_u, jnp.tile(q_norm_g[i], N_Q_HEADS)[None],
            jnp.tile(k_norm_g[i], N_KV_HEADS)[None], cos_t, sin_t, bdq, bdk, hy_conv_w[i], hy_conv_b[i][None],
            bound.reshape(1, 1))
        att = lax.cond(
            bound <= MAX_FIXED_SHIFT,
            functools.partial(_attention, _attn_bounded_kernel, KV_CHUNK_BOUNDED, "attention_bounded"),
            functools.partial(_attention, _attn_online_kernel, KV_CHUNK, "attention_online"),
            q, k, v)
        hy = _hyena(hv, x1, x2, filt_w1[i], filt_b1[i], filt_w2[i], filt_b2[i], filt_w3[i], filt_b3[i],
                    filt_w4[i], filt_freq[i], filt_deltas[i], hy_skip_d[i])
        last = i == norm1_g.shape[0] - 1
        assert last, "single-layer trunk"
        h = _out_mlp(h, att, hy, attn_out_g[i][None], hy_out_g[i][None], w_out[i].astype(BF16), norm2_g[i][None],
                     w_mlp_in[i].astype(BF16), w_mlp_out[i].astype(BF16), final_g[None])
    return h
```

```python
import functools
import math

import jax
import jax.numpy as jnp
from jax import lax
from jax.experimental import pallas as pl
from jax.experimental.pallas import tpu as pltpu

F32 = jnp.float32
BF16 = jnp.bfloat16

HEAD_DIM = 64
N_Q_HEADS = 8
N_KV_HEADS = 2
Q_WIDTH = N_Q_HEADS * HEAD_DIM
KV_WIDTH = N_KV_HEADS * HEAD_DIM
HY_WIDTH = 512
GRID_W = 64
ROPE_THETA = 10000.0
FILTER_EMB = 33
FILTER_PAD = 64
EPS = 1e-6
LOG2E = math.log2(math.e)
MAX_FIXED_SHIFT = 50.0

LANES = 128
SUBLANES = 8
TX_ROWS = SUBLANES * LANES
SWAP = 16
HALO = 16
VMEM_LIMIT = 56 * 1024 * 1024

ROW_TILE = TX_ROWS
MLP_ROWS = 512
Q_TILE = 128
KV_CHUNK = 512
KV_CHUNK_BOUNDED = 8192
FILT_TILE = 512


def _dot(a, b):
    return jnp.dot(a, b, preferred_element_type=F32)


def _dot3(a, b):
    ah = a.astype(BF16)
    al = (a - ah.astype(F32)).astype(BF16)
    bh = b.astype(BF16)
    bl = (b - bh.astype(F32)).astype(BF16)
    return _dot(ah, bh) + (_dot(ah, bl) + _dot(al, bh))


def _rms(v, g):
    return v * lax.rsqrt(jnp.mean(v * v, axis=-1, keepdims=True) + EPS) * g


def _params(*sem):
    return pltpu.CompilerParams(dimension_semantics=sem, vmem_limit_bytes=VMEM_LIMIT)


def _in_proj_kernel(x_ref, xp_ref, xn_ref, g1_ref, wqkv_ref, wu_ref, qg_ref, kg_ref, cos_ref, sin_ref,
                    bdq_ref, bdk_ref, cw_ref, cb_ref, bound_ref,
                    q_ref, k_ref, v_ref, hv_ref, x1_ref, x2_ref, a_scr, *, tm, n_tiles):
    i = pl.program_id(1)
    g1 = g1_ref[...]
    a_scr[HALO:HALO + tm, :] = _rms(x_ref[...], g1).astype(BF16)
    prev = jnp.where(i > 0, _rms(xp_ref[...], g1), 0.0)
    nxt = jnp.where(i < n_tiles - 1, _rms(xn_ref[...], g1), 0.0)
    a_scr[0:HALO, :] = prev.astype(BF16)
    a_scr[HALO + tm:, :] = nxt.astype(BF16)

    qkv = _dot(a_scr[HALO:HALO + tm, :], wqkv_ref[...])
    cos = cos_ref[...]
    sin = sin_ref[...]
    lane = lax.broadcasted_iota(jnp.int32, (tm, LANES), 1)
    first16 = (lane % 32) < 16
    low_half = lane < HEAD_DIM

    def norm_rope(blk, ms, gain):
        y = blk * lax.rsqrt(ms + EPS) * gain
        partner = jnp.where(first16, pltpu.roll(y, LANES - 16, 1), pltpu.roll(y, 16, 1))
        return y * cos + partner * sin

    q = qkv[:, :Q_WIDTH]
    q_ms = _dot((q * q).astype(BF16), bdq_ref[...])
    qg = qg_ref[...]
    neg_bound = -bound_ref[0, 0]
    extra = lane == HEAD_DIM
    for m in range(Q_WIDTH // LANES):
        sl = slice(m * LANES, (m + 1) * LANES)
        blk = norm_rope(q[:, sl], q_ms[:, sl], qg[:, sl]) * (HEAD_DIM ** -0.5 * LOG2E)
        q_ref[:, 2 * m * LANES:(2 * m + 1) * LANES] = jnp.where(extra, neg_bound, blk).astype(BF16)
        q_ref[:, (2 * m + 1) * LANES:(2 * m + 2) * LANES] = jnp.where(
            extra, neg_bound, pltpu.roll(blk, HEAD_DIM, 1)).astype(BF16)

    k = qkv[:, Q_WIDTH:Q_WIDTH + KV_WIDTH]
    k_ms = _dot((k * k).astype(BF16), bdk_ref[...])
    k = norm_rope(k, k_ms, kg_ref[...])
    v = qkv[:, Q_WIDTH + KV_WIDTH:]
    pad = jnp.where(extra, 1.0, 0.0)
    k_ref[0] = jnp.where(low_half, k, pad).astype(BF16)
    k_ref[1] = jnp.where(low_half, pltpu.roll(k, HEAD_DIM, 1), pad).astype(BF16)
    v_ref[0] = jnp.where(low_half, v, pad).astype(BF16)
    v_ref[1] = jnp.where(low_half, pltpu.roll(v, HEAD_DIM, 1), pad).astype(BF16)

    rows = tm + 2 * HALO
    for c, out in enumerate((hv_ref, x1_ref, x2_ref)):
        cs = slice(c * HY_WIDTH, (c + 1) * HY_WIDTH)
        u = _dot(a_scr[...], wu_ref[:, cs])
        w = cw_ref[:, cs]
        y = (pltpu.roll(u, 1, 0) * w[0:1] + u * w[1:2] + pltpu.roll(u, rows - 1, 0) * w[2:3]) + cb_ref[:, cs]
        for cq in range(HY_WIDTH // LANES):
            for p in range(SUBLANES):
                out[cq, pl.ds(p, LANES, stride=SUBLANES), :] = y[HALO + p * LANES:HALO + (p + 1) * LANES,
                                                                 cq * LANES:(cq + 1) * LANES]


def _in_proj(x, g1, w_qkv, w_u, qg, kg, cos_t, sin_t, bdq, bdk, conv_w, conv_b, bound):
    b, seq, d = x.shape
    tm = ROW_TILE
    n_tiles = seq // tm
    hb = tm // HALO
    cq = HY_WIDTH // LANES
    full = lambda shape: pl.BlockSpec(shape, lambda bi, i: (0,) * len(shape))
    row_out = lambda w: pl.BlockSpec((None, tm, w), lambda bi, i: (bi, i, 0))
    kv_out = pl.BlockSpec((None, N_KV_HEADS, tm, LANES), lambda bi, i: (bi, 0, i, 0))
    tx_out = pl.BlockSpec((None, cq, None, tm, LANES), lambda bi, i: (bi, 0, i, 0, 0))
    tx_shape = jax.ShapeDtypeStruct((b, cq, n_tiles, tm, LANES), F32)
    return pl.pallas_call(
        functools.partial(_in_proj_kernel, tm=tm, n_tiles=n_tiles),
        grid=(b, n_tiles),
        in_specs=[
            pl.BlockSpec((None, tm, d), lambda bi, i: (bi, i, 0)),
            pl.BlockSpec((None, HALO, d), lambda bi, i: (bi, jnp.maximum(i * hb - 1, 0), 0)),
            pl.BlockSpec((None, HALO, d), lambda bi, i: (bi, jnp.minimum((i + 1) * hb, seq // HALO - 1), 0)),
            full((1, d)),
            full(w_qkv.shape),
            full(w_u.shape),
            full((1, Q_WIDTH)),
            full((1, KV_WIDTH)),
            pl.BlockSpec((tm, LANES), lambda bi, i: (i, 0)),
            pl.BlockSpec((tm, LANES), lambda bi, i: (i, 0)),
            full(bdq.shape),
            full(bdk.shape),
            full(conv_w.shape),
            full(conv_b.shape),
            pl.BlockSpec(memory_space=pltpu.SMEM),
        ],
        out_specs=[row_out(2 * Q_WIDTH), kv_out, kv_out, tx_out, tx_out, tx_out],
        out_shape=[
            jax.ShapeDtypeStruct((b, seq, 2 * Q_WIDTH), BF16),
            jax.ShapeDtypeStruct((b, N_KV_HEADS, seq, LANES), BF16),
            jax.ShapeDtypeStruct((b, N_KV_HEADS, seq, LANES), BF16),
            tx_shape, tx_shape, tx_shape,
        ],
        scratch_shapes=[pltpu.VMEM((tm + 2 * HALO, d), BF16)],
        compiler_params=_params("parallel", "arbitrary"),
        name="in_proj",
    )(x, x, x, g1, w_qkv, w_u, qg, kg, cos_t, sin_t, bdq, bdk, conv_w, conv_b, bound)


def _stack_heads(q_ref):
    group = N_Q_HEADS // N_KV_HEADS
    return jnp.concatenate([q_ref[:, g * LANES:(g + 1) * LANES] for g in range(group)], axis=0)


def _store_heads(o_ref, o, tq):
    group = N_Q_HEADS // N_KV_HEADS
    pairs = [o[(2 * p) * tq:(2 * p + 1) * tq] + pltpu.roll(o[(2 * p + 1) * tq:(2 * p + 2) * tq], HEAD_DIM, 1)
             for p in range(group // 2)]
    o_ref[...] = jnp.concatenate(pairs, axis=1)


def _attn_bounded_kernel(q_ref, k_ref, v_ref, o_ref, *, tq, sc, n_chunks):
    q = _stack_heads(q_ref)
    rows = q.shape[0]

    def body(c, acc):
        off = pl.multiple_of(c * sc, sc)
        s = lax.dot_general(q, k_ref[pl.ds(off, sc), :], (((1,), (1,)), ((), ())), preferred_element_type=F32)
        return acc + _dot(jnp.exp2(s).astype(BF16), v_ref[pl.ds(off, sc), :])

    acc = lax.fori_loop(0, n_chunks, body, jnp.zeros((rows, LANES), F32))
    lane = lax.broadcasted_iota(jnp.int32, (rows, LANES), 1)
    _store_heads(o_ref, jnp.where(lane < HEAD_DIM, acc / acc[:, HEAD_DIM:HEAD_DIM + 1], 0.0), tq)


def _attn_online_kernel(q_ref, k_ref, v_ref, o_ref, *, tq, sc, n_chunks):
    q = _stack_heads(q_ref)
    rows = q.shape[0]

    def body(c, carry):
        m, l, acc = carry
        off = pl.multiple_of(c * sc, sc)
        s = lax.dot_general(q, k_ref[pl.ds(off, sc), :], (((1,), (1,)), ((), ())), preferred_element_type=F32)
        m_new = jnp.maximum(m, jnp.max(s, axis=1, keepdims=True))
        p = jnp.exp2(s - m_new)
        alpha = jnp.exp2(m - m_new)
        l = alpha * l + jnp.sum(p, axis=1, keepdims=True)
        acc = alpha * acc + _dot(p.astype(BF16), v_ref[pl.ds(off, sc), :])
        return m_new, l, acc

    init = (jnp.full((rows, 1), jnp.finfo(F32).min, F32), jnp.zeros((rows, 1), F32), jnp.zeros((rows, LANES), F32))
    _, l, acc = lax.fori_loop(0, n_chunks, body, init)
    lane = lax.broadcasted_iota(jnp.int32, (rows, LANES), 1)
    _store_heads(o_ref, jnp.where(lane < HEAD_DIM, acc / l, 0.0), tq)


def _attention(body, sc, name, q, k, v):
    b, seq, _ = q.shape
    tq = min(Q_TILE, seq)
    sc = min(sc, seq)
    group = N_Q_HEADS // N_KV_HEADS
    return pl.pallas_call(
        functools.partial(body, tq=tq, sc=sc, n_chunks=seq // sc),
        grid=(b, N_KV_HEADS, seq // tq),
        in_specs=[
            pl.BlockSpec((None, tq, group * LANES), lambda bi, j, i: (bi, i, j)),
            pl.BlockSpec((None, None, seq, LANES), lambda bi, j, i: (bi, j, 0, 0)),
            pl.BlockSpec((None, None, seq, LANES), lambda bi, j, i: (bi, j, 0, 0)),
        ],
        out_specs=pl.BlockSpec((None, tq, group * HEAD_DIM), lambda bi, j, i: (bi, i, j)),
        out_shape=jax.ShapeDtypeStruct((b, seq, Q_WIDTH), F32),
        compiler_params=_params("parallel", "parallel", "arbitrary"),
        name=name,
    )(q, k, v)


def _strided_slab(ref, lead, start, rows):
    return jnp.concatenate([ref[lead + (c, pl.ds(start, rows, stride=SUBLANES), slice(None))]
                            for c in range(HY_WIDTH // LANES)], axis=1)


def _swap_store(o_ref, slabs, rows):
    t = pltpu.einshape("jkc->kjc", jnp.stack(slabs)).astype(BF16)
    o_ref[:, 0] = t[:rows]
    o_ref[:, 1] = t[rows:]


def _filter_kernel(z_ref, w1_ref, b1_ref, w2_ref, b2_ref, w3_ref, b3_ref, w4_ref, fr_ref, dl_ref,
                   taps_ref, s_ref, *, nh):
    i = pl.program_id(0)
    z = z_ref[...]
    fr = fr_ref[...]
    h = jnp.sin(fr * (_dot3(z, w1_ref[...]) + b1_ref[...]))
    h = jnp.sin(fr * (_dot3(h, w2_ref[...]) + b2_ref[...]))
    h = jnp.sin(fr * (_dot3(h, w3_ref[...]) + b3_ref[...]))
    h = _dot(h.astype(BF16), w4_ref[...].astype(BF16)) * jnp.exp(-z[:, 0:1] * jnp.abs(dl_ref[...]))
    first = (lax.broadcasted_iota(jnp.int32, (h.shape[0], HY_WIDTH), 0) == 0) & (i == 0)
    sums = []
    for od in range(4):
        hod = h[:, od * HY_WIDTH:(od + 1) * HY_WIDTH]
        if od % 2 == 1:
            hod = jnp.where(first, 0.0, hod)
        sums.append(jnp.sum(jnp.abs(hod), axis=0, keepdims=True))
        for c in range(HY_WIDTH // LANES):
            for g in range(h.shape[0] // nh):
                taps_ref[od // 2, od % 2, c, g] = hod[g * nh:(g + 1) * nh, c * LANES:(c + 1) * LANES].astype(BF16)
    ssum = jnp.concatenate(sums, axis=1)

    @pl.when(i == 0)
    def _():
        s_ref[...] = ssum

    @pl.when(i > 0)
    def _():
        s_ref[...] += ssum


def _filter_taps(zt, w1, b1, w2, b2, w3, b3, w4, freq, deltas):
    seq = zt.shape[0]
    nh = seq // LANES
    groups = FILT_TILE // nh
    cq = HY_WIDTH // LANES
    full = lambda a: pl.BlockSpec(a.shape, lambda i: (0,) * a.ndim)
    return pl.pallas_call(
        functools.partial(_filter_kernel, nh=nh),
        grid=(seq // FILT_TILE,),
        in_specs=[pl.BlockSpec((FILT_TILE, FILTER_PAD), lambda i: (i, 0))]
        + [full(a) for a in (w1, b1, w2, b2, w3, b3, w4, freq, deltas)],
        out_specs=[pl.BlockSpec((2, 2, cq, groups, nh, LANES), lambda i: (0, 0, 0, i, 0, 0)),
                   pl.BlockSpec((1, 4 * HY_WIDTH), lambda i: (0, 0))],
        out_shape=[jax.ShapeDtypeStruct((2, 2, cq, LANES, nh, LANES), BF16),
                   jax.ShapeDtypeStruct((1, 4 * HY_WIDTH), F32)],
        compiler_params=_params("arbitrary"),
        name="filter_taps",
    )(zt, w1, b1, w2, b2, w3, b3, w4, freq, deltas)


def _fft_outer_filter_kernel(fw_ref, bw_ref, bw0_ref, f_ref, f0_ref, a_ref, *, jb, n1):
    first_block = pl.program_id(1) == 0
    lanes = range(HY_WIDTH // LANES)
    slabs = []
    for j in range(jb):
        bw = [bw0_ref[c, 0] for c in lanes] if j == 0 else [bw_ref[c, jb - j] for c in lanes]
        x = jnp.concatenate([jnp.concatenate([fw_ref[c, j] for c in lanes], axis=1),
                             jnp.concatenate(bw, axis=1)], axis=0)
        mat = jnp.where(first_block, f0_ref[...], f_ref[...]) if j == 0 else f_ref[...]
        slabs.append(_dot(mat, x))
    _swap_store(a_ref, slabs, n1)


def _fft_outer_filter(taps, f1t, f1t0):
    n_ord, _, cq, _, nh, _ = taps.shape
    n1 = 2 * nh
    jb = SWAP
    jbn = LANES // jb
    tap_spec = lambda d, blk: pl.BlockSpec((None, None, cq, jb, nh, LANES), lambda o, j: (o, d, 0, blk(j), 0, 0))
    return pl.pallas_call(
        functools.partial(_fft_outer_filter_kernel, jb=jb, n1=n1),
        grid=(n_ord, jbn),
        in_specs=[tap_spec(0, lambda j: j), tap_spec(1, lambda j: jbn - 1 - j), tap_spec(1, lambda j: (jbn - j) % jbn),
                  pl.BlockSpec(f1t.shape, lambda o, j: (0, 0)), pl.BlockSpec(f1t0.shape, lambda o, j: (0, 0))],
        out_specs=pl.BlockSpec((None, n1, 2, jb, HY_WIDTH), lambda o, j: (o, 0, 0, j, 0)),
        out_shape=jax.ShapeDtypeStruct((n_ord, n1, 2, LANES, HY_WIDTH), BF16),
        compiler_params=_params("parallel", "parallel"),
        name="fft_outer_filter",
    )(taps, taps, taps, f1t, f1t0)


def _time_rows(z_ref, b, c, j, nh):
    return z_ref[b, c, :, j].reshape(nh, LANES)


def _fft_outer_kernel(z_ref, f_ref, a_ref, *, jb, n1):
    nh = n1 // 2
    slabs = []
    for j in range(jb):
        x = jnp.concatenate(
            [jnp.concatenate([_time_rows(z_ref, b, c, j, nh) for c in range(HY_WIDTH // LANES)], axis=1)
             for b in range(2)], axis=0).astype(BF16)
        slabs.append(_dot(f_ref[...], x))
    _swap_store(a_ref, slabs, n1)


def _fft_outer(z, f1):
    _, cq, na, _, _, _ = z.shape
    n1 = f1.shape[0] // 2
    jb = SWAP
    return pl.pallas_call(
        functools.partial(_fft_outer_kernel, jb=jb, n1=n1),
        grid=(LANES // jb,),
        in_specs=[pl.BlockSpec((2, cq, na, jb, SUBLANES, LANES), lambda j: (0, 0, 0, j, 0, 0)),
                  pl.BlockSpec(f1.shape, lambda j: (0, 0))],
        out_specs=pl.BlockSpec((n1, 2, jb, HY_WIDTH), lambda j: (0, 0, j, 0)),
        out_shape=jax.ShapeDtypeStruct((n1, 2, LANES, HY_WIDTH), BF16),
        compiler_params=_params("parallel"),
        name="fft_outer",
    )(z, f1)


def _fft_inner_kernel(a_ref, af_ref, s_ref, g_ref, b_ref, *, kb, n_fft):
    scale = 1.0 / ((s_ref[0] + s_ref[1]) * n_fft)
    slabs = []
    for kk in range(kb):
        g = g_ref[kk]
        h = _dot(g, af_ref[kk].reshape(2 * LANES, HY_WIDTH)) * scale
        c = _dot(g, a_ref[kk].reshape(2 * LANES, HY_WIDTH))
        cr, ci = c[:LANES], c[LANES:]
        hr, hi = h[:LANES], h[LANES:]
        p = jnp.concatenate([cr * hr - ci * hi, cr * hi + ci * hr], axis=0).astype(BF16)
        slabs.append(lax.dot_general(g, p, (((0,), (0,)), ((), ())), preferred_element_type=F32))
    _swap_store(b_ref, slabs, LANES)


def _fft_inner(a, af, s, gf, order, n_fft):
    n1 = a.shape[0]
    kb = SWAP
    a_blk = (kb, 2, LANES, HY_WIDTH)
    return pl.pallas_call(
        functools.partial(_fft_inner_kernel, kb=kb, n_fft=n_fft),
        grid=(n1 // kb,),
        in_specs=[pl.BlockSpec(a_blk, lambda k: (k, 0, 0, 0)),
                  pl.BlockSpec((None,) + a_blk, lambda k: (order, k, 0, 0, 0)),
                  pl.BlockSpec((None, 2, 1, HY_WIDTH), lambda k: (order, 0, 0, 0)),
                  pl.BlockSpec((kb, 2 * LANES, 2 * LANES), lambda k: (k, 0, 0))],
        out_specs=pl.BlockSpec((LANES, 2, kb, HY_WIDTH), lambda k: (0, 0, k, 0)),
        out_shape=jax.ShapeDtypeStruct((LANES, 2, n1, HY_WIDTH), BF16),
        compiler_params=_params("parallel"),
        name="fft_inner",
    )(a, af, s, gf)


def _ifft_outer_gate_kernel(b_ref, f_ref, z_ref, g_ref, skip_ref, o_ref, *, jb, nh):
    skip = skip_ref[...]
    for j in range(jb):
        y = _dot(f_ref[...], b_ref[j].reshape(4 * nh, HY_WIDTH))
        for b in range(2):
            for c in range(HY_WIDTH // LANES):
                cs = slice(c * LANES, (c + 1) * LANES)
                val = _time_rows(g_ref, b, c, j, nh) * (y[b * nh:(b + 1) * nh, cs]
                                                        + skip[:, cs] * _time_rows(z_ref, b, c, j, nh))
                o_ref[b, c, :, j, :, :] = val.reshape(nh // SUBLANES, SUBLANES, LANES)


def _ifft_outer_gate(bsp, f1inv, z, gate, skip_row):
    _, cq, na, _, _, _ = z.shape
    nh = na * SUBLANES
    jb = SUBLANES
    t_spec = pl.BlockSpec((2, cq, na, jb, SUBLANES, LANES), lambda j: (0, 0, 0, j, 0, 0))
    return pl.pallas_call(
        functools.partial(_ifft_outer_gate_kernel, jb=jb, nh=nh),
        grid=(LANES // jb,),
        in_specs=[pl.BlockSpec((jb,) + bsp.shape[1:], lambda j: (j, 0, 0, 0)),
                  pl.BlockSpec(f1inv.shape, lambda j: (0, 0)), t_spec, t_spec,
                  pl.BlockSpec((1, HY_WIDTH), lambda j: (0, 0))],
        out_specs=t_spec,
        out_shape=jax.ShapeDtypeStruct(z.shape, F32),
        compiler_params=_params("parallel"),
        name="ifft_outer_gate",
    )(bsp, f1inv, z, gate, skip_row)


def _dft_tables(seq):
    n = 2 * seq
    n1 = n // LANES
    nh = n1 // 2
    k1 = jnp.arange(n1, dtype=jnp.int32)
    r = jnp.arange(nh, dtype=jnp.int32)

    def outer(cols):
        ang = ((k1[:, None] * cols[None, :]) % n1).astype(F32) * (2.0 * math.pi / n1)
        return jnp.cos(ang), jnp.sin(ang)

    fc, fs = outer(r)
    bc, bs = outer(n1 - 1 - r)
    zc, zs = outer((n1 - r) % n1)
    zc, zs = zc * (r > 0), zs * (r > 0)
    f1 = jnp.concatenate([jnp.concatenate([fc, fs], 1), jnp.concatenate([-fs, fc], 1)], 0).astype(BF16)
    f1t = jnp.concatenate([jnp.concatenate([fc, bc], 1), jnp.concatenate([-fs, -bs], 1)], 0).astype(BF16)
    f1t0 = jnp.concatenate([jnp.concatenate([fc, zc], 1), jnp.concatenate([-fs, -zs], 1)], 0).astype(BF16)
    f1inv = jnp.concatenate([jnp.concatenate([fc.T, -fs.T], 1), jnp.concatenate([fs.T, fc.T], 1)], 0).astype(BF16)
    k2 = jnp.arange(LANES, dtype=jnp.int32)
    k = k1[:, None, None] + n1 * k2[None, :, None]
    th = ((k * k2[None, None, :]) % n).astype(F32) * (2.0 * math.pi / n)
    c, s = jnp.cos(th), jnp.sin(th)
    gf = jnp.concatenate([jnp.concatenate([c, s], 2), jnp.concatenate([-s, c], 2)], 1).astype(BF16)
    return f1, f1t, f1t0, f1inv, gf


def _filter_features(seq):
    nh = seq // LANES
    pos = (jnp.arange(LANES, dtype=jnp.int32)[:, None] + LANES * jnp.arange(nh, dtype=jnp.int32)[None, :]).reshape(seq)
    t = (pos.astype(F32) / (seq - 1))[:, None]
    bands = (FILTER_EMB - 1) // 2
    w_ang = 2.0 * math.pi * pos.astype(F32) / seq
    band_f = jnp.linspace(1e-4, bands - 1, bands, dtype=F32)
    ang = w_ang[:, None] * band_f[None, :]
    z = jnp.concatenate([t, jnp.cos(ang), -jnp.sin(ang)], axis=-1)
    return jnp.pad(z, ((0, 0), (0, FILTER_PAD - FILTER_EMB)))


def _hyena(hv, x1, x2, fw1, fb1, fw2, fb2, fw3, fb3, fw4, ffreq, fdeltas, skip_d):
    b, cq, na, rows, _ = hv.shape
    assert b == 2, "the two batch rows ride as the real and imaginary parts of one transform"
    seq = na * rows
    n = 2 * seq
    f1, f1t, f1t0, f1inv, gf = _dft_tables(seq)

    w1p = jnp.pad(fw1, ((0, FILTER_PAD - FILTER_EMB), (0, 0)))
    taps, s = _filter_taps(_filter_features(seq), w1p, fb1[None], fw2, fb2[None], fw3, fb3[None], fw4,
                           ffreq[None], fdeltas[None])
    af = _fft_outer_filter(taps, f1t, f1t0)
    s = s.reshape(2, 2, 1, HY_WIDTH)

    split = lambda a: a.reshape(b, cq, na, LANES, SUBLANES, LANES)
    zz = split(hv)
    for order, gate in enumerate((x1, x2)):
        bsp = _fft_inner(_fft_outer(zz, f1), af, s, gf, order, n)
        zz = _ifft_outer_gate(bsp, f1inv, zz, split(gate), skip_d[order][None])
    return zz.reshape(hv.shape)


def _out_mlp_kernel(x_ref, att_ref, hy_ref, ga_ref, gh_ref, wo_ref, g2_ref, w1_ref, w2_ref, gf_ref, o_ref):
    half = att_ref.shape[1]
    per_pass = MLP_ROWS // LANES
    for r in range(x_ref.shape[0] // MLP_ROWS):
        rows = slice(r * MLP_ROWS, (r + 1) * MLP_ROWS)
        hy = jnp.concatenate([_strided_slab(hy_ref, (), p, LANES) for p in range(r * per_pass, (r + 1) * per_pass)],
                             axis=0)
        a = _rms(att_ref[rows, :], ga_ref[...]).astype(BF16)
        hyn = _rms(hy, gh_ref[...]).astype(BF16)
        h = x_ref[rows, :] + _dot(a, wo_ref[:half, :]) + _dot(hyn, wo_ref[half:, :])
        m = _rms(h, g2_ref[...]).astype(BF16)
        t = jnp.square(jnp.maximum(_dot(m, w1_ref[...]), 0.0)).astype(BF16)
        h = h + _dot(t, w2_ref[...])
        o_ref[rows, :] = _rms(h, gf_ref[...])


def _out_mlp(x, att, hy, ga, gh, w_out, g2, w1, w2, gfin):
    b, seq, d = x.shape
    tm = ROW_TILE
    row = lambda w: pl.BlockSpec((None, tm, w), lambda bi, i: (bi, i, 0))
    full = lambda a: pl.BlockSpec(a.shape, lambda bi, i: (0,) * a.ndim, pipeline_mode=pl.Buffered(1))
    return pl.pallas_call(
        _out_mlp_kernel,
        grid=(b, seq // tm),
        in_specs=[row(d), row(att.shape[-1]),
                  pl.BlockSpec((None, hy.shape[1], None, tm, LANES), lambda bi, i: (bi, 0, i, 0, 0)),
                  full(ga), full(gh), full(w_out), full(g2), full(w1), full(w2), full(gfin)],
        out_specs=row(d),
        out_shape=jax.ShapeDtypeStruct((b, seq, d), F32),
        compiler_params=_params("parallel", "parallel"),
        name="out_mlp",
    )(x, att, hy, ga, gh, w_out, g2, w1, w2, gfin)


def _rope_tables(seq):
    rows = seq // GRID_W
    row = jnp.repeat(jnp.arange(rows, dtype=F32), GRID_W)
    col = jnp.tile(jnp.arange(GRID_W, dtype=F32), rows)
    half = HEAD_DIM // 2
    inv_freq = ROPE_THETA ** (-jnp.arange(0, half, 2, dtype=F32) / half)
    ang_r = row[:, None] * inv_freq[None, :]
    ang_c = col[:, None] * inv_freq[None, :]
    cos = jnp.concatenate([jnp.cos(ang_r)] * 2 + [jnp.cos(ang_c)] * 2, axis=1)
    sin = jnp.concatenate([-jnp.sin(ang_r), jnp.sin(ang_r), -jnp.sin(ang_c), jnp.sin(ang_c)], axis=1)
    return jnp.tile(cos, (1, LANES // HEAD_DIM)), jnp.tile(sin, (1, LANES // HEAD_DIM))


def _head_mean_matrix(width):
    head = jnp.arange(width, dtype=jnp.int32) // HEAD_DIM
    return jnp.where(head[:, None] == head[None, :], 1.0 / HEAD_DIM, 0.0).astype(BF16)


def kernel(x, norm1_g, w_in, q_norm_g, k_norm_g, hy_conv_w, hy_conv_b, filt_w1, filt_b1, filt_w2, filt_b2, filt_w3, filt_b3, filt_w4, filt_freq, filt_deltas, hy_skip_d, attn_out_g, hy_out_g, w_out, norm2_g, w_mlp_in, w_mlp_out, final_g):
    seq = x.shape[1]
    cos_t, sin_t = _rope_tables(seq)
    bdq = _head_mean_matrix(Q_WIDTH)
    bdk = _head_mean_matrix(KV_WIDTH)
    h = x
    for i in range(norm1_g.shape[0]):
        w_qkv = w_in[i][:, :Q_WIDTH + 2 * KV_WIDTH].astype(BF16)
        w_u = w_in[i][:, Q_WIDTH + 2 * KV_WIDTH:].astype(BF16)
        bound = (1.01 * HEAD_DIM ** 0.5 * LOG2E) * jnp.max(jnp.abs(q_norm_g[i])) * jnp.max(jnp.abs(k_norm_g[i]))
        q, k, v, hv, x1, x2 = _in_proj(
            h, norm1_g[i][None], w_qkv, w_u, jnp.tile(q_norm_g[i], N_Q_HEADS)[None],
            jnp.tile(k_norm_g[i], N_KV_HEADS)[None], cos_t, sin_t, bdq, bdk, hy_conv_w[i], hy_conv_b[i][None],
            bound.reshape(1, 1))
        att = lax.cond(
            bound <= MAX_FIXED_SHIFT,
            functools.partial(_attention, _attn_bounded_kernel, KV_CHUNK_BOUNDED, "attention_bounded"),
            functools.partial(_attention, _attn_online_kernel, KV_CHUNK, "attention_online"),
            q, k, v)
        hy = _hyena(hv, x1, x2, filt_w1[i], filt_b1[i], filt_w2[i], filt_b2[i], filt_w3[i], filt_b3[i],
                    filt_w4[i], filt_freq[i], filt_deltas[i], hy_skip_d[i])
        last = i == norm1_g.shape[0] - 1
        assert last, "single-layer trunk"
        h = _out_mlp(h, att, hy, attn_out_g[i][None], hy_out_g[i][None], w_out[i].astype(BF16), norm2_g[i][None],
                     w_mlp_in[i].astype(BF16), w_mlp_out[i].astype(BF16), final_g[None])
    return h
```

```python
import functools
import math

import jax
import jax.numpy as jnp
from jax import lax
from jax.experimental import pallas as pl
from jax.experimental.pallas import tpu as pltpu

F32 = jnp.float32
BF16 = jnp.bfloat16

HEAD_DIM = 64
N_Q_HEADS = 8
N_KV_HEADS = 2
Q_WIDTH = N_Q_HEADS * HEAD_DIM
KV_WIDTH = N_KV_HEADS * HEAD_DIM
HY_WIDTH = 512
GRID_W = 64
ROPE_THETA = 10000.0
FILTER_EMB = 33
FILTER_PAD = 64
EPS = 1e-6
LOG2E = math.log2(math.e)
MAX_FIXED_SHIFT = 50.0

LANES = 128
SUBLANES = 8
TX_ROWS = SUBLANES * LANES
SWAP = 16
HALO = 16
VMEM_LIMIT = 56 * 1024 * 1024

ROW_TILE = TX_ROWS
MLP_ROWS = 512
Q_TILE = 128
KV_CHUNK = 512
KV_CHUNK_BOUNDED = 8192
FILT_TILE = 512


def _dot(a, b):
    return jnp.dot(a, b, preferred_element_type=F32)


def _dot3(a, b):
    ah = a.astype(BF16)
    al = (a - ah.astype(F32)).astype(BF16)
    bh = b.astype(BF16)
    bl = (b - bh.astype(F32)).astype(BF16)
    return _dot(ah, bh) + (_dot(ah, bl) + _dot(al, bh))


def _rms(v, g):
    return v * lax.rsqrt(jnp.mean(v * v, axis=-1, keepdims=True) + EPS) * g


_PI_A = 3.140625
_PI_B = 9.67502593994140625e-4
_PI_C = 1.509957990978376432e-7
_SIN_SMALL = 8192.0
_SIN_TAYLOR = (-1.0 / 6.0, 1.0 / 120.0, -1.0 / 5040.0, 1.0 / 362880.0, -1.0 / 39916800.0, 1.0 / 6227020800.0)


def _sin_small(x):
    k = jnp.floor(x * (1.0 / math.pi) + 0.5)
    r = ((x - k * _PI_A) - k * _PI_B) - k * _PI_C
    r2 = r * r
    poly = _SIN_TAYLOR[-1]
    for coef in _SIN_TAYLOR[-2::-1]:
        poly = poly * r2 + coef
    s = r + r * r2 * poly
    return jnp.where((k.astype(jnp.int32) & 1) == 1, -s, s)


def _sin(x):
    return lax.cond(jnp.max(jnp.abs(x)) <= _SIN_SMALL, _sin_small, jnp.sin, x)


def _params(*sem):
    return pltpu.CompilerParams(dimension_semantics=sem, vmem_limit_bytes=VMEM_LIMIT)


def _in_proj_kernel(x_ref, xp_ref, xn_ref, g1_ref, wqkv_ref, wu_ref, qg_ref, kg_ref, cos_ref, sin_ref,
                    bdq_ref, bdk_ref, cw_ref, cb_ref, bound_ref,
                    q_ref, k_ref, v_ref, hv_ref, x1_ref, x2_ref, a_scr, *, tm, n_tiles):
    i = pl.program_id(1)
    g1 = g1_ref[...]
    a_scr[HALO:HALO + tm, :] = _rms(x_ref[...], g1).astype(BF16)
    prev = jnp.where(i > 0, _rms(xp_ref[...], g1), 0.0)
    nxt = jnp.where(i < n_tiles - 1, _rms(xn_ref[...], g1), 0.0)
    a_scr[0:HALO, :] = prev.astype(BF16)
    a_scr[HALO + tm:, :] = nxt.astype(BF16)

    qkv = _dot(a_scr[HALO:HALO + tm, :], wqkv_ref[...])
    cos = cos_ref[...]
    sin = sin_ref[...]
    lane = lax.broadcasted_iota(jnp.int32, (tm, LANES), 1)
    first16 = (lane % 32) < 16
    low_half = lane < HEAD_DIM

    def norm_rope(blk, ms, gain):
        y = blk * lax.rsqrt(ms + EPS) * gain
        partner = jnp.where(first16, pltpu.roll(y, LANES - 16, 1), pltpu.roll(y, 16, 1))
        return y * cos + partner * sin

    q = qkv[:, :Q_WIDTH]
    q_ms = _dot((q * q).astype(BF16), bdq_ref[...])
    qg = qg_ref[...]
    neg_bound = -bound_ref[0, 0]
    extra = lane == HEAD_DIM
    for m in range(Q_WIDTH // LANES):
        sl = slice(m * LANES, (m + 1) * LANES)
        blk = norm_rope(q[:, sl], q_ms[:, sl], qg[:, sl]) * (HEAD_DIM ** -0.5 * LOG2E)
        q_ref[:, 2 * m * LANES:(2 * m + 1) * LANES] = jnp.where(extra, neg_bound, blk).astype(BF16)
        q_ref[:, (2 * m + 1) * LANES:(2 * m + 2) * LANES] = jnp.where(
            extra, neg_bound, pltpu.roll(blk, HEAD_DIM, 1)).astype(BF16)

    k = qkv[:, Q_WIDTH:Q_WIDTH + KV_WIDTH]
    k_ms = _dot((k * k).astype(BF16), bdk_ref[...])
    k = norm_rope(k, k_ms, kg_ref[...])
    v = qkv[:, Q_WIDTH + KV_WIDTH:]
    pad = jnp.where(extra, 1.0, 0.0)
    k_ref[0] = jnp.where(low_half, k, pad).astype(BF16)
    k_ref[1] = jnp.where(low_half, pltpu.roll(k, HEAD_DIM, 1), pad).astype(BF16)
    v_ref[0] = jnp.where(low_half, v, pad).astype(BF16)
    v_ref[1] = jnp.where(low_half, pltpu.roll(v, HEAD_DIM, 1), pad).astype(BF16)

    rows = tm + 2 * HALO
    for c, out in enumerate((hv_ref, x1_ref, x2_ref)):
        cs = slice(c * HY_WIDTH, (c + 1) * HY_WIDTH)
        u = _dot(a_scr[...], wu_ref[:, cs])
        w = cw_ref[:, cs]
        y = (pltpu.roll(u, 1, 0) * w[0:1] + u * w[1:2] + pltpu.roll(u, rows - 1, 0) * w[2:3]) + cb_ref[:, cs]
        for cq in range(HY_WIDTH // LANES):
            for p in range(SUBLANES):
                out[cq, pl.ds(p, LANES, stride=SUBLANES), :] = y[HALO + p * LANES:HALO + (p + 1) * LANES,
                                                                 cq * LANES:(cq + 1) * LANES]


def _in_proj(x, g1, w_qkv, w_u, qg, kg, cos_t, sin_t, bdq, bdk, conv_w, conv_b, bound):
    b, seq, d = x.shape
    tm = ROW_TILE
    n_tiles = seq // tm
    hb = tm // HALO
    cq = HY_WIDTH // LANES
    full = lambda shape: pl.BlockSpec(shape, lambda bi, i: (0,) * len(shape))
    row_out = lambda w: pl.BlockSpec((None, tm, w), lambda bi, i: (bi, i, 0))
    kv_out = pl.BlockSpec((None, N_KV_HEADS, tm, LANES), lambda bi, i: (bi, 0, i, 0))
    tx_out = pl.BlockSpec((None, cq, None, tm, LANES), lambda bi, i: (bi, 0, i, 0, 0))
    tx_shape = jax.ShapeDtypeStruct((b, cq, n_tiles, tm, LANES), F32)
    return pl.pallas_call(
        functools.partial(_in_proj_kernel, tm=tm, n_tiles=n_tiles),
        grid=(b, n_tiles),
        in_specs=[
            pl.BlockSpec((None, tm, d), lambda bi, i: (bi, i, 0)),
            pl.BlockSpec((None, HALO, d), lambda bi, i: (bi, jnp.maximum(i * hb - 1, 0), 0)),
            pl.BlockSpec((None, HALO, d), lambda bi, i: (bi, jnp.minimum((i + 1) * hb, seq // HALO - 1), 0)),
            full((1, d)),
            full(w_qkv.shape),
            full(w_u.shape),
            full((1, Q_WIDTH)),
            full((1, KV_WIDTH)),
            pl.BlockSpec((tm, LANES), lambda bi, i: (i, 0)),
            pl.BlockSpec((tm, LANES), lambda bi, i: (i, 0)),
            full(bdq.shape),
            full(bdk.shape),
            full(conv_w.shape),
            full(conv_b.shape),
            pl.BlockSpec(memory_space=pltpu.SMEM),
        ],
        out_specs=[row_out(2 * Q_WIDTH), kv_out, kv_out, tx_out, tx_out, tx_out],
        out_shape=[
            jax.ShapeDtypeStruct((b, seq, 2 * Q_WIDTH), BF16),
            jax.ShapeDtypeStruct((b, N_KV_HEADS, seq, LANES), BF16),
            jax.ShapeDtypeStruct((b, N_KV_HEADS, seq, LANES), BF16),
            tx_shape, tx_shape, tx_shape,
        ],
        scratch_shapes=[pltpu.VMEM((tm + 2 * HALO, d), BF16)],
        compiler_params=_params("parallel", "arbitrary"),
        name="in_proj",
    )(x, x, x, g1, w_qkv, w_u, qg, kg, cos_t, sin_t, bdq, bdk, conv_w, conv_b, bound)


def _stack_heads(q_ref):
    group = N_Q_HEADS // N_KV_HEADS
    return jnp.concatenate([q_ref[:, g * LANES:(g + 1) * LANES] for g in range(group)], axis=0)


def _store_heads(o_ref, o, tq):
    group = N_Q_HEADS // N_KV_HEADS
    pairs = [o[(2 * p) * tq:(2 * p + 1) * tq] + pltpu.roll(o[(2 * p + 1) * tq:(2 * p + 2) * tq], HEAD_DIM, 1)
             for p in range(group // 2)]
    o_ref[...] = jnp.concatenate(pairs, axis=1)


def _attn_bounded_kernel(q_ref, k_ref, v_ref, o_ref, *, tq, sc, n_chunks):
    q = _stack_heads(q_ref)
    rows = q.shape[0]

    def body(c, acc):
        off = pl.multiple_of(c * sc, sc)
        s = lax.dot_general(q, k_ref[pl.ds(off, sc), :], (((1,), (1,)), ((), ())), preferred_element_type=F32)
        return acc + _dot(jnp.exp2(s).astype(BF16), v_ref[pl.ds(off, sc), :])

    acc = lax.fori_loop(0, n_chunks, body, jnp.zeros((rows, LANES), F32))
    lane = lax.broadcasted_iota(jnp.int32, (rows, LANES), 1)
    _store_heads(o_ref, jnp.where(lane < HEAD_DIM, acc / acc[:, HEAD_DIM:HEAD_DIM + 1], 0.0), tq)


def _attn_online_kernel(q_ref, k_ref, v_ref, o_ref, *, tq, sc, n_chunks):
    q = _stack_heads(q_ref)
    rows = q.shape[0]

    def body(c, carry):
        m, l, acc = carry
        off = pl.multiple_of(c * sc, sc)
        s = lax.dot_general(q, k_ref[pl.ds(off, sc), :], (((1,), (1,)), ((), ())), preferred_element_type=F32)
        m_new = jnp.maximum(m, jnp.max(s, axis=1, keepdims=True))
        p = jnp.exp2(s - m_new)
        alpha = jnp.exp2(m - m_new)
        l = alpha * l + jnp.sum(p, axis=1, keepdims=True)
        acc = alpha * acc + _dot(p.astype(BF16), v_ref[pl.ds(off, sc), :])
        return m_new, l, acc

    init = (jnp.full((rows, 1), jnp.finfo(F32).min, F32), jnp.zeros((rows, 1), F32), jnp.zeros((rows, LANES), F32))
    _, l, acc = lax.fori_loop(0, n_chunks, body, init)
    lane = lax.broadcasted_iota(jnp.int32, (rows, LANES), 1)
    _store_heads(o_ref, jnp.where(lane < HEAD_DIM, acc / l, 0.0), tq)


def _attention(body, sc, name, q, k, v):
    b, seq, _ = q.shape
    tq = min(Q_TILE, seq)
    sc = min(sc, seq)
    group = N_Q_HEADS // N_KV_HEADS
    return pl.pallas_call(
        functools.partial(body, tq=tq, sc=sc, n_chunks=seq // sc),
        grid=(b, N_KV_HEADS, seq // tq),
        in_specs=[
            pl.BlockSpec((None, tq, group * LANES), lambda bi, j, i: (bi, i, j)),
            pl.BlockSpec((None, None, seq, LANES), lambda bi, j, i: (bi, j, 0, 0)),
            pl.BlockSpec((None, None, seq, LANES), lambda bi, j, i: (bi, j, 0, 0)),
        ],
        out_specs=pl.BlockSpec((None, tq, group * HEAD_DIM), lambda bi, j, i: (bi, i, j)),
        out_shape=jax.ShapeDtypeStruct((b, seq, Q_WIDTH), F32),
        compiler_params=_params("parallel", "parallel", "arbitrary"),
        name=name,
    )(q, k, v)


def _strided_slab(ref, lead, start, rows):
    return jnp.concatenate([ref[lead + (c, pl.ds(start, rows, stride=SUBLANES), slice(None))]
                            for c in range(HY_WIDTH // LANES)], axis=1)


def _swap_store(o_ref, slabs, rows):
    t = jnp.swapaxes(jnp.stack(slabs), 0, 1).astype(BF16)
    o_ref[:, 0] = t[:rows]
    o_ref[:, 1] = t[rows:]


def _filter_kernel(z_ref, w1_ref, b1_ref, w2_ref, b2_ref, w3_ref, b3_ref, w4_ref, fr_ref, dl_ref,
                   taps_ref, s_ref, *, nh):
    i = pl.program_id(0)
    z = z_ref[...]
    fr = fr_ref[...]
    h = _sin(fr * (_dot3(z, w1_ref[...]) + b1_ref[...]))
    h = _sin(fr * (_dot3(h, w2_ref[...]) + b2_ref[...]))
    h = _sin(fr * (_dot3(h, w3_ref[...]) + b3_ref[...]))
    h = _dot(h.astype(BF16), w4_ref[...].astype(BF16)) * jnp.exp(-z[:, 0:1] * jnp.abs(dl_ref[...]))
    first = (lax.broadcasted_iota(jnp.int32, (h.shape[0], HY_WIDTH), 0) == 0) & (i == 0)
    sums = []
    for od in range(4):
        hod = h[:, od * HY_WIDTH:(od + 1) * HY_WIDTH]
        if od % 2 == 1:
            hod = jnp.where(first, 0.0, hod)
        sums.append(jnp.sum(jnp.abs(hod), axis=0, keepdims=True))
        for c in range(HY_WIDTH // LANES):
            for g in range(h.shape[0] // nh):
                taps_ref[od // 2, od % 2, c, g] = hod[g * nh:(g + 1) * nh, c * LANES:(c + 1) * LANES].astype(BF16)
    ssum = jnp.concatenate(sums, axis=1)

    @pl.when(i == 0)
    def _():
        s_ref[...] = ssum

    @pl.when(i > 0)
    def _():
        s_ref[...] += ssum


def _filter_taps(zt, w1, b1, w2, b2, w3, b3, w4, freq, deltas):
    seq = zt.shape[0]
    nh = seq // LANES
    groups = FILT_TILE // nh
    cq = HY_WIDTH // LANES
    full = lambda a: pl.BlockSpec(a.shape, lambda i: (0,) * a.ndim)
    return pl.pallas_call(
        functools.partial(_filter_kernel, nh=nh),
        grid=(seq // FILT_TILE,),
        in_specs=[pl.BlockSpec((FILT_TILE, FILTER_PAD), lambda i: (i, 0))]
        + [full(a) for a in (w1, b1, w2, b2, w3, b3, w4, freq, deltas)],
        out_specs=[pl.BlockSpec((2, 2, cq, groups, nh, LANES), lambda i: (0, 0, 0, i, 0, 0)),
                   pl.BlockSpec((1, 4 * HY_WIDTH), lambda i: (0, 0))],
        out_shape=[jax.ShapeDtypeStruct((2, 2, cq, LANES, nh, LANES), BF16),
                   jax.ShapeDtypeStruct((1, 4 * HY_WIDTH), F32)],
        compiler_params=_params("arbitrary"),
        name="filter_taps",
    )(zt, w1, b1, w2, b2, w3, b3, w4, freq, deltas)


def _fft_outer_filter_kernel(fw_ref, bw_ref, bw0_ref, f_ref, f0_ref, a_ref, *, jb, n1):
    first_block = pl.program_id(1) == 0
    lanes = range(HY_WIDTH // LANES)
    slabs = []
    for j in range(jb):
        bw = [bw0_ref[c, 0] for c in lanes] if j == 0 else [bw_ref[c, jb - j] for c in lanes]
        x = jnp.concatenate([jnp.concatenate([fw_ref[c, j] for c in lanes], axis=1),
                             jnp.concatenate(bw, axis=1)], axis=0)
        mat = jnp.where(first_block, f0_ref[...], f_ref[...]) if j == 0 else f_ref[...]
        slabs.append(_dot(mat, x))
    _swap_store(a_ref, slabs, n1)


def _fft_outer_filter(taps, f1t, f1t0):
    n_ord, _, cq, _, nh, _ = taps.shape
    n1 = 2 * nh
    jb = SWAP
    jbn = LANES // jb
    tap_spec = lambda d, blk: pl.BlockSpec((None, None, cq, jb, nh, LANES), lambda o, j: (o, d, 0, blk(j), 0, 0))
    return pl.pallas_call(
        functools.partial(_fft_outer_filter_kernel, jb=jb, n1=n1),
        grid=(n_ord, jbn),
        in_specs=[tap_spec(0, lambda j: j), tap_spec(1, lambda j: jbn - 1 - j), tap_spec(1, lambda j: (jbn - j) % jbn),
                  pl.BlockSpec(f1t.shape, lambda o, j: (0, 0)), pl.BlockSpec(f1t0.shape, lambda o, j: (0, 0))],
        out_specs=pl.BlockSpec((None, n1, 2, jb, HY_WIDTH), lambda o, j: (o, 0, 0, j, 0)),
        out_shape=jax.ShapeDtypeStruct((n_ord, n1, 2, LANES, HY_WIDTH), BF16),
        compiler_params=_params("parallel", "parallel"),
        name="fft_outer_filter",
    )(taps, taps, taps, f1t, f1t0)


def _time_rows(z_ref, b, c, j, nh):
    return z_ref[b, c, :, j].reshape(nh, LANES)


def _fft_outer_kernel(z_ref, f_ref, a_ref, *, jb, n1):
    nh = n1 // 2
    slabs = []
    for j in range(jb):
        x = jnp.concatenate(
            [jnp.concatenate([_time_rows(z_ref, b, c, j, nh) for c in range(HY_WIDTH // LANES)], axis=1)
             for b in range(2)], axis=0).astype(BF16)
        slabs.append(_dot(f_ref[...], x))
    _swap_store(a_ref, slabs, n1)


def _fft_outer(z, f1):
    _, cq, na, _, _, _ = z.shape
    n1 = f1.shape[0] // 2
    jb = SWAP
    return pl.pallas_call(
        functools.partial(_fft_outer_kernel, jb=jb, n1=n1),
        grid=(LANES // jb,),
        in_specs=[pl.BlockSpec((2, cq, na, jb, SUBLANES, LANES), lambda j: (0, 0, 0, j, 0, 0)),
                  pl.BlockSpec(f1.shape, lambda j: (0, 0))],
        out_specs=pl.BlockSpec((n1, 2, jb, HY_WIDTH), lambda j: (0, 0, j, 0)),
        out_shape=jax.ShapeDtypeStruct((n1, 2, LANES, HY_WIDTH), BF16),
        compiler_params=_params("parallel"),
        name="fft_outer",
    )(z, f1)


def _fft_inner_kernel(a_ref, af_ref, s_ref, g_ref, b_ref, *, kb, n_fft):
    scale = 1.0 / ((s_ref[0] + s_ref[1]) * n_fft)
    hs = [_dot(g_ref[kk], af_ref[kk].reshape(2 * LANES, HY_WIDTH)) for kk in range(kb)]
    cs = [_dot(g_ref[kk], a_ref[kk].reshape(2 * LANES, HY_WIDTH)) for kk in range(kb)]
    ps = []
    for h, c in zip(hs, cs):
        hr, hi = h[:LANES] * scale, h[LANES:] * scale
        cr, ci = c[:LANES], c[LANES:]
        ps.append(jnp.concatenate([cr * hr - ci * hi, cr * hi + ci * hr], axis=0).astype(BF16))
    _swap_store(b_ref, [lax.dot_general(g_ref[kk], p, (((0,), (0,)), ((), ())), preferred_element_type=F32)
                        for kk, p in enumerate(ps)], LANES)


def _fft_inner(a, af, s, gf, order, n_fft):
    n1 = a.shape[0]
    kb = SWAP
    a_blk = (kb, 2, LANES, HY_WIDTH)
    return pl.pallas_call(
        functools.partial(_fft_inner_kernel, kb=kb, n_fft=n_fft),
        grid=(n1 // kb,),
        in_specs=[pl.BlockSpec(a_blk, lambda k: (k, 0, 0, 0)),
                  pl.BlockSpec((None,) + a_blk, lambda k: (order, k, 0, 0, 0)),
                  pl.BlockSpec((None, 2, 1, HY_WIDTH), lambda k: (order, 0, 0, 0)),
                  pl.BlockSpec((kb, 2 * LANES, 2 * LANES), lambda k: (k, 0, 0))],
        out_specs=pl.BlockSpec((LANES, 2, kb, HY_WIDTH), lambda k: (0, 0, k, 0)),
        out_shape=jax.ShapeDtypeStruct((LANES, 2, n1, HY_WIDTH), BF16),
        compiler_params=_params("parallel"),
        name="fft_inner",
    )(a, af, s, gf)


def _ifft_outer_gate_kernel(b_ref, f_ref, z_ref, g_ref, skip_ref, o_ref, *, jb, nh):
    skip = skip_ref[...]
    for j in range(jb):
        y = _dot(f_ref[...], b_ref[j].reshape(4 * nh, HY_WIDTH))
        for b in range(2):
            for c in range(HY_WIDTH // LANES):
                cs = slice(c * LANES, (c + 1) * LANES)
                val = _time_rows(g_ref, b, c, j, nh) * (y[b * nh:(b + 1) * nh, cs]
                                                        + skip[:, cs] * _time_rows(z_ref, b, c, j, nh))
                o_ref[b, c, :, j, :, :] = val.reshape(nh // SUBLANES, SUBLANES, LANES)


def _ifft_outer_gate(bsp, f1inv, z, gate, skip_row):
    _, cq, na, _, _, _ = z.shape
    nh = na * SUBLANES
    jb = SUBLANES
    t_spec = pl.BlockSpec((2, cq, na, jb, SUBLANES, LANES), lambda j: (0, 0, 0, j, 0, 0))
    return pl.pallas_call(
        functools.partial(_ifft_outer_gate_kernel, jb=jb, nh=nh),
        grid=(LANES // jb,),
        in_specs=[pl.BlockSpec((jb,) + bsp.shape[1:], lambda j: (j, 0, 0, 0)),
                  pl.BlockSpec(f1inv.shape, lambda j: (0, 0)), t_spec, t_spec,
                  pl.BlockSpec((1, HY_WIDTH), lambda j: (0, 0))],
        out_specs=t_spec,
        out_shape=jax.ShapeDtypeStruct(z.shape, F32),
        compiler_params=_params("parallel"),
        name="ifft_outer_gate",
    )(bsp, f1inv, z, gate, skip_row)


def _dft_tables(seq):
    n = 2 * seq
    n1 = n // LANES
    nh = n1 // 2
    k1 = jnp.arange(n1, dtype=jnp.int32)
    r = jnp.arange(nh, dtype=jnp.int32)

    def outer(cols):
        ang = ((k1[:, None] * cols[None, :]) % n1).astype(F32) * (2.0 * math.pi / n1)
        return jnp.cos(ang), jnp.sin(ang)

    fc, fs = outer(r)
    bc, bs = outer(n1 - 1 - r)
    zc, zs = outer((n1 - r) % n1)
    zc, zs = zc * (r > 0), zs * (r > 0)
    f1 = jnp.concatenate([jnp.concatenate([fc, fs], 1), jnp.concatenate([-fs, fc], 1)], 0).astype(BF16)
    f1t = jnp.concatenate([jnp.concatenate([fc, bc], 1), jnp.concatenate([-fs, -bs], 1)], 0).astype(BF16)
    f1t0 = jnp.concatenate([jnp.concatenate([fc, zc], 1), jnp.concatenate([-fs, -zs], 1)], 0).astype(BF16)
    f1inv = jnp.concatenate([jnp.concatenate([fc.T, -fs.T], 1), jnp.concatenate([fs.T, fc.T], 1)], 0).astype(BF16)
    n2 = jnp.arange(LANES, dtype=jnp.int32)
    alpha = ((k1[:, None] * n2[None, :]) % n).astype(F32) * (2.0 * math.pi / n)
    beta = ((n2[:, None] * n2[None, :]) % LANES).astype(F32) * (2.0 * math.pi / LANES)
    ca, sa = jnp.cos(alpha)[:, None, :], jnp.sin(alpha)[:, None, :]
    cb, sb = jnp.cos(beta)[None], jnp.sin(beta)[None]
    c, s = ca * cb - sa * sb, sa * cb + ca * sb
    gf = jnp.concatenate([jnp.concatenate([c, s], 2), jnp.concatenate([-s, c], 2)], 1).astype(BF16)
    return f1, f1t, f1t0, f1inv, gf


def _filter_features(seq):
    nh = seq // LANES
    pos = (jnp.arange(LANES, dtype=jnp.int32)[:, None] + LANES * jnp.arange(nh, dtype=jnp.int32)[None, :]).reshape(seq)
    t = (pos.astype(F32) / (seq - 1))[:, None]
    bands = (FILTER_EMB - 1) // 2
    w_ang = 2.0 * math.pi * pos.astype(F32) / seq
    band_f = jnp.linspace(1e-4, bands - 1, bands, dtype=F32)
    ang = w_ang[:, None] * band_f[None, :]
    z = jnp.concatenate([t, jnp.cos(ang), -jnp.sin(ang)], axis=-1)
    return jnp.pad(z, ((0, 0), (0, FILTER_PAD - FILTER_EMB)))


def _hyena(hv, x1, x2, fw1, fb1, fw2, fb2, fw3, fb3, fw4, ffreq, fdeltas, skip_d):
    b, cq, na, rows, _ = hv.shape
    assert b == 2, "the two batch rows ride as the real and imaginary parts of one transform"
    seq = na * rows
    n = 2 * seq
    f1, f1t, f1t0, f1inv, gf = _dft_tables(seq)

    w1p = jnp.pad(fw1, ((0, FILTER_PAD - FILTER_EMB), (0, 0)))
    taps, s = _filter_taps(_filter_features(seq), w1p, fb1[None], fw2, fb2[None], fw3, fb3[None], fw4,
                           ffreq[None], fdeltas[None])
    af = _fft_outer_filter(taps, f1t, f1t0)
    s = s.reshape(2, 2, 1, HY_WIDTH)

    split = lambda a: a.reshape(b, cq, na, LANES, SUBLANES, LANES)
    zz = split(hv)
    for order, gate in enumerate((x1, x2)):
        bsp = _fft_inner(_fft_outer(zz, f1), af, s, gf, order, n)
        zz = _ifft_outer_gate(bsp, f1inv, zz, split(gate), skip_d[order][None])
    return zz.reshape(hv.shape)


def _out_mlp_kernel(x_ref, att_ref, hy_ref, ga_ref, gh_ref, wo_ref, g2_ref, w1_ref, w2_ref, gf_ref, o_ref):
    half = att_ref.shape[1]
    per_pass = MLP_ROWS // LANES
    for r in range(x_ref.shape[0] // MLP_ROWS):
        rows = slice(r * MLP_ROWS, (r + 1) * MLP_ROWS)
        hy = jnp.concatenate([_strided_slab(hy_ref, (), p, LANES) for p in range(r * per_pass, (r + 1) * per_pass)],
                             axis=0)
        a = _rms(att_ref[rows, :], ga_ref[...]).astype(BF16)
        hyn = _rms(hy, gh_ref[...]).astype(BF16)
        h = x_ref[rows, :] + _dot(a, wo_ref[:half, :]) + _dot(hyn, wo_ref[half:, :])
        m = _rms(h, g2_ref[...]).astype(BF16)
        t = jnp.square(jnp.maximum(_dot(m, w1_ref[...]), 0.0)).astype(BF16)
        h = h + _dot(t, w2_ref[...])
        o_ref[rows, :] = _rms(h, gf_ref[...])


def _out_mlp(x, att, hy, ga, gh, w_out, g2, w1, w2, gfin):
    b, seq, d = x.shape
    tm = ROW_TILE
    row = lambda w: pl.BlockSpec((None, tm, w), lambda bi, i: (bi, i, 0))
    full = lambda a: pl.BlockSpec(a.shape, lambda bi, i: (0,) * a.ndim, pipeline_mode=pl.Buffered(1))
    return pl.pallas_call(
        _out_mlp_kernel,
        grid=(b, seq // tm),
        in_specs=[row(d), row(att.shape[-1]),
                  pl.BlockSpec((None, hy.shape[1], None, tm, LANES), lambda bi, i: (bi, 0, i, 0, 0)),
                  full(ga), full(gh), full(w_out), full(g2), full(w1), full(w2), full(gfin)],
        out_specs=row(d),
        out_shape=jax.ShapeDtypeStruct((b, seq, d), F32),
        compiler_params=_params("parallel", "parallel"),
        name="out_mlp",
    )(x, att, hy, ga, gh, w_out, g2, w1, w2, gfin)


def _rope_tables(seq):
    rows = seq // GRID_W
    row = jnp.repeat(jnp.arange(rows, dtype=F32), GRID_W)
    col = jnp.tile(jnp.arange(GRID_W, dtype=F32), rows)
    half = HEAD_DIM // 2
    inv_freq = ROPE_THETA ** (-jnp.arange(0, half, 2, dtype=F32) / half)
    ang_r = row[:, None] * inv_freq[None, :]
    ang_c = col[:, None] * inv_freq[None, :]
    cos = jnp.concatenate([jnp.cos(ang_r)] * 2 + [jnp.cos(ang_c)] * 2, axis=1)
    sin = jnp.concatenate([-jnp.sin(ang_r), jnp.sin(ang_r), -jnp.sin(ang_c), jnp.sin(ang_c)], axis=1)
    return jnp.tile(cos, (1, LANES // HEAD_DIM)), jnp.tile(sin, (1, LANES // HEAD_DIM))


def _head_mean_matrix(width):
    head = jnp.arange(width, dtype=jnp.int32) // HEAD_DIM
    return jnp.where(head[:, None] == head[None, :], 1.0 / HEAD_DIM, 0.0).astype(BF16)


def kernel(x, norm1_g, w_in, q_norm_g, k_norm_g, hy_conv_w, hy_conv_b, filt_w1, filt_b1, filt_w2, filt_b2, filt_w3, filt_b3, filt_w4, filt_freq, filt_deltas, hy_skip_d, attn_out_g, hy_out_g, w_out, norm2_g, w_mlp_in, w_mlp_out, final_g):
    seq = x.shape[1]
    cos_t, sin_t = _rope_tables(seq)
    bdq = _head_mean_matrix(Q_WIDTH)
    bdk = _head_mean_matrix(KV_WIDTH)
    h = x
    for i in range(norm1_g.shape[0]):
        w_qkv = w_in[i][:, :Q_WIDTH + 2 * KV_WIDTH].astype(BF16)
        w_u = w_in[i][:, Q_WIDTH + 2 * KV_WIDTH:].astype(BF16)
        bound = (1.01 * HEAD_DIM ** 0.5 * LOG2E) * jnp.max(jnp.abs(q_norm_g[i])) * jnp.max(jnp.abs(k_norm_g[i]))
        q, k, v, hv, x1, x2 = _in_proj(
            h, norm1_g[i][None], w_qkv, w_u, jnp.tile(q_norm_g[i], N_Q_HEADS)[None],
            jnp.tile(k_norm_g[i], N_KV_HEADS)[None], cos_t, sin_t, bdq, bdk, hy_conv_w[i], hy_conv_b[i][None],
            bound.reshape(1, 1))
        att = lax.cond(
            bound <= MAX_FIXED_SHIFT,
            functools.partial(_attention, _attn_bounded_kernel, KV_CHUNK_BOUNDED, "attention_bounded"),
            functools.partial(_attention, _attn_online_kernel, KV_CHUNK, "attention_online"),
            q, k, v)
        hy = _hyena(hv, x1, x2, filt_w1[i], filt_b1[i], filt_w2[i], filt_b2[i], filt_w3[i], filt_b3[i],
                    filt_w4[i], filt_freq[i], filt_deltas[i], hy_skip_d[i])
        last = i == norm1_g.shape[0] - 1
        assert last, "single-layer trunk"
        h = _out_mlp(h, att, hy, attn_out_g[i][None], hy_out_g[i][None], w_out[i].astype(BF16), norm2_g[i][None],
                     w_mlp_in[i].astype(BF16), w_mlp_out[i].astype(BF16), final_g[None])
    return h
```

```python
import functools
import math

import jax
import jax.numpy as jnp
from jax import lax
from jax.experimental import pallas as pl
from jax.experimental.pallas import tpu as pltpu

F32 = jnp.float32
BF16 = jnp.bfloat16

HEAD_DIM = 64
N_Q_HEADS = 8
N_KV_HEADS = 2
Q_WIDTH = N_Q_HEADS * HEAD_DIM
KV_WIDTH = N_KV_HEADS * HEAD_DIM
HY_WIDTH = 512
GRID_W = 64
ROPE_THETA = 10000.0
FILTER_EMB = 33
FILTER_PAD = 64
EPS = 1e-6
LOG2E = math.log2(math.e)
MAX_FIXED_SHIFT = 50.0

LANES = 128
SUBLANES = 8
TX_ROWS = SUBLANES * LANES
SWAP = 16
HALO = 16
VMEM_LIMIT = 56 * 1024 * 1024

ROW_TILE = TX_ROWS
MLP_ROWS = 512
Q_TILE = 128
KV_CHUNK = 512
KV_CHUNK_BOUNDED = 8192
FILT_TILE = 512


def _dot(a, b):
    return jnp.dot(a, b, preferred_element_type=F32)


def _dot3(a, b):
    ah = a.astype(BF16)
    al = (a - ah.astype(F32)).astype(BF16)
    bh = b.astype(BF16)
    bl = (b - bh.astype(F32)).astype(BF16)
    return _dot(ah, bh) + (_dot(ah, bl) + _dot(al, bh))


def _rms(v, g):
    return v * lax.rsqrt(jnp.mean(v * v, axis=-1, keepdims=True) + EPS) * g


_PI_A = 3.140625
_PI_B = 9.67502593994140625e-4
_PI_C = 1.509957990978376432e-7
_SIN_SMALL = 8192.0
_SIN_TAYLOR = (-1.0 / 6.0, 1.0 / 120.0, -1.0 / 5040.0, 1.0 / 362880.0, -1.0 / 39916800.0, 1.0 / 6227020800.0)


def _sin_small(x):
    k = jnp.floor(x * (1.0 / math.pi) + 0.5)
    r = ((x - k * _PI_A) - k * _PI_B) - k * _PI_C
    r2 = r * r
    poly = _SIN_TAYLOR[-1]
    for coef in _SIN_TAYLOR[-2::-1]:
        poly = poly * r2 + coef
    s = r + r * r2 * poly
    return jnp.where((k.astype(jnp.int32) & 1) == 1, -s, s)


def _sin(x):
    return lax.cond(jnp.max(jnp.abs(x)) <= _SIN_SMALL, _sin_small, jnp.sin, x)


def _params(*sem):
    return pltpu.CompilerParams(dimension_semantics=sem, vmem_limit_bytes=VMEM_LIMIT)


def _in_proj_kernel(x_ref, xp_ref, xn_ref, g1_ref, wqkv_ref, wu_ref, qg_ref, kg_ref, cos_ref, sin_ref,
                    bdq_ref, bdk_ref, cw_ref, cb_ref, bound_ref,
                    q_ref, k_ref, v_ref, hv_ref, x1_ref, x2_ref, a_scr, *, tm, n_tiles):
    i = pl.program_id(1)
    g1 = g1_ref[...]
    a_scr[HALO:HALO + tm, :] = _rms(x_ref[...], g1).astype(BF16)
    prev = jnp.where(i > 0, _rms(xp_ref[...], g1), 0.0)
    nxt = jnp.where(i < n_tiles - 1, _rms(xn_ref[...], g1), 0.0)
    a_scr[0:HALO, :] = prev.astype(BF16)
    a_scr[HALO + tm:, :] = nxt.astype(BF16)

    qkv = _dot(a_scr[HALO:HALO + tm, :], wqkv_ref[...])
    cos = cos_ref[...]
    sin = sin_ref[...]
    lane = lax.broadcasted_iota(jnp.int32, (tm, LANES), 1)
    first16 = (lane % 32) < 16
    low_half = lane < HEAD_DIM

    def norm_rope(blk, ms, gain):
        y = blk * lax.rsqrt(ms + EPS) * gain
        partner = jnp.where(first16, pltpu.roll(y, LANES - 16, 1), pltpu.roll(y, 16, 1))
        return y * cos + partner * sin

    q = qkv[:, :Q_WIDTH]
    q_ms = _dot((q * q).astype(BF16), bdq_ref[...])
    qg = qg_ref[...]
    neg_bound = -bound_ref[0, 0]
    extra = lane == HEAD_DIM
    for m in range(Q_WIDTH // LANES):
        sl = slice(m * LANES, (m + 1) * LANES)
        blk = norm_rope(q[:, sl], q_ms[:, sl], qg[:, sl]) * (HEAD_DIM ** -0.5 * LOG2E)
        q_ref[:, 2 * m * LANES:(2 * m + 1) * LANES] = jnp.where(extra, neg_bound, blk).astype(BF16)
        q_ref[:, (2 * m + 1) * LANES:(2 * m + 2) * LANES] = jnp.where(
            extra, neg_bound, pltpu.roll(blk, HEAD_DIM, 1)).astype(BF16)

    k = qkv[:, Q_WIDTH:Q_WIDTH + KV_WIDTH]
    k_ms = _dot((k * k).astype(BF16), bdk_ref[...])
    k = norm_rope(k, k_ms, kg_ref[...])
    v = qkv[:, Q_WIDTH + KV_WIDTH:]
    pad = jnp.where(extra, 1.0, 0.0)
    k_ref[0] = jnp.where(low_half, k, pad).astype(BF16)
    k_ref[1] = jnp.where(low_half, pltpu.roll(k, HEAD_DIM, 1), pad).astype(BF16)
    v_ref[0] = jnp.where(low_half, v, pad).astype(BF16)
    v_ref[1] = jnp.where(low_half, pltpu.roll(v, HEAD_DIM, 1), pad).astype(BF16)

    rows = tm + 2 * HALO
    for c, out in enumerate((hv_ref, x1_ref, x2_ref)):
        cs = slice(c * HY_WIDTH, (c + 1) * HY_WIDTH)
        u = _dot(a_scr[...], wu_ref[:, cs])
        w = cw_ref[:, cs]
        y = (pltpu.roll(u, 1, 0) * w[0:1] + u * w[1:2] + pltpu.roll(u, rows - 1, 0) * w[2:3]) + cb_ref[:, cs]
        yt = jnp.swapaxes(y[HALO:HALO + tm].reshape(SUBLANES, LANES, HY_WIDTH), 0, 1)
        out[...] = yt.reshape(tm, HY_WIDTH).astype(BF16)


def _in_proj(x, g1, w_qkv, w_u, qg, kg, cos_t, sin_t, bdq, bdk, conv_w, conv_b, bound):
    b, seq, d = x.shape
    tm = ROW_TILE
    n_tiles = seq // tm
    hb = tm // HALO
    full = lambda shape: pl.BlockSpec(shape, lambda bi, i: (0,) * len(shape))
    row_out = lambda w: pl.BlockSpec((None, tm, w), lambda bi, i: (bi, i, 0))
    kv_out = pl.BlockSpec((None, N_KV_HEADS, tm, LANES), lambda bi, i: (bi, 0, i, 0))
    tx_out = pl.BlockSpec((None, None, tm, HY_WIDTH), lambda bi, i: (bi, i, 0, 0))
    tx_shape = jax.ShapeDtypeStruct((b, n_tiles, tm, HY_WIDTH), BF16)
    return pl.pallas_call(
        functools.partial(_in_proj_kernel, tm=tm, n_tiles=n_tiles),
        grid=(b, n_tiles),
        in_specs=[
            pl.BlockSpec((None, tm, d), lambda bi, i: (bi, i, 0)),
            pl.BlockSpec((None, HALO, d), lambda bi, i: (bi, jnp.maximum(i * hb - 1, 0), 0)),
            pl.BlockSpec((None, HALO, d), lambda bi, i: (bi, jnp.minimum((i + 1) * hb, seq // HALO - 1), 0)),
            full((1, d)),
            full(w_qkv.shape),
            full(w_u.shape),
            full((1, Q_WIDTH)),
            full((1, KV_WIDTH)),
            pl.BlockSpec((tm, LANES), lambda bi, i: (i, 0)),
            pl.BlockSpec((tm, LANES), lambda bi, i: (i, 0)),
            full(bdq.shape),
            full(bdk.shape),
            full(conv_w.shape),
            full(conv_b.shape),
            pl.BlockSpec(memory_space=pltpu.SMEM),
        ],
        out_specs=[row_out(2 * Q_WIDTH), kv_out, kv_out, tx_out, tx_out, tx_out],
        out_shape=[
            jax.ShapeDtypeStruct((b, seq, 2 * Q_WIDTH), BF16),
            jax.ShapeDtypeStruct((b, N_KV_HEADS, seq, LANES), BF16),
            jax.ShapeDtypeStruct((b, N_KV_HEADS, seq, LANES), BF16),
            tx_shape, tx_shape, tx_shape,
        ],
        scratch_shapes=[pltpu.VMEM((tm + 2 * HALO, d), BF16)],
        compiler_params=_params("parallel", "arbitrary"),
        name="in_proj",
    )(x, x, x, g1, w_qkv, w_u, qg, kg, cos_t, sin_t, bdq, bdk, conv_w, conv_b, bound)


def _stack_heads(q_ref):
    group = N_Q_HEADS // N_KV_HEADS
    return jnp.concatenate([q_ref[:, g * LANES:(g + 1) * LANES] for g in range(group)], axis=0)


def _store_heads(o_ref, o, tq):
    group = N_Q_HEADS // N_KV_HEADS
    pairs = [o[(2 * p) * tq:(2 * p + 1) * tq] + pltpu.roll(o[(2 * p + 1) * tq:(2 * p + 2) * tq], HEAD_DIM, 1)
             for p in range(group // 2)]
    o_ref[...] = jnp.concatenate(pairs, axis=1)


def _attn_bounded_kernel(q_ref, k_ref, v_ref, o_ref, *, tq, sc, n_chunks):
    q = _stack_heads(q_ref)
    rows = q.shape[0]

    def body(c, acc):
        off = pl.multiple_of(c * sc, sc)
        s = lax.dot_general(q, k_ref[pl.ds(off, sc), :], (((1,), (1,)), ((), ())), preferred_element_type=F32)
        return acc + _dot(jnp.exp2(s).astype(BF16), v_ref[pl.ds(off, sc), :])

    acc = lax.fori_loop(0, n_chunks, body, jnp.zeros((rows, LANES), F32))
    lane = lax.broadcasted_iota(jnp.int32, (rows, LANES), 1)
    _store_heads(o_ref, jnp.where(lane < HEAD_DIM, acc / acc[:, HEAD_DIM:HEAD_DIM + 1], 0.0), tq)


def _attn_online_kernel(q_ref, k_ref, v_ref, o_ref, *, tq, sc, n_chunks):
    q = _stack_heads(q_ref)
    rows = q.shape[0]

    def body(c, carry):
        m, l, acc = carry
        off = pl.multiple_of(c * sc, sc)
        s = lax.dot_general(q, k_ref[pl.ds(off, sc), :], (((1,), (1,)), ((), ())), preferred_element_type=F32)
        m_new = jnp.maximum(m, jnp.max(s, axis=1, keepdims=True))
        p = jnp.exp2(s - m_new)
        alpha = jnp.exp2(m - m_new)
        l = alpha * l + jnp.sum(p, axis=1, keepdims=True)
        acc = alpha * acc + _dot(p.astype(BF16), v_ref[pl.ds(off, sc), :])
        return m_new, l, acc

    init = (jnp.full((rows, 1), jnp.finfo(F32).min, F32), jnp.zeros((rows, 1), F32), jnp.zeros((rows, LANES), F32))
    _, l, acc = lax.fori_loop(0, n_chunks, body, init)
    lane = lax.broadcasted_iota(jnp.int32, (rows, LANES), 1)
    _store_heads(o_ref, jnp.where(lane < HEAD_DIM, acc / l, 0.0), tq)


def _attention(body, sc, name, q, k, v):
    b, seq, _ = q.shape
    tq = min(Q_TILE, seq)
    sc = min(sc, seq)
    group = N_Q_HEADS // N_KV_HEADS
    return pl.pallas_call(
        functools.partial(body, tq=tq, sc=sc, n_chunks=seq // sc),
        grid=(b, N_KV_HEADS, seq // tq),
        in_specs=[
            pl.BlockSpec((None, tq, group * LANES), lambda bi, j, i: (bi, i, j)),
            pl.BlockSpec((None, None, seq, LANES), lambda bi, j, i: (bi, j, 0, 0)),
            pl.BlockSpec((None, None, seq, LANES), lambda bi, j, i: (bi, j, 0, 0)),
        ],
        out_specs=pl.BlockSpec((None, tq, group * HEAD_DIM), lambda bi, j, i: (bi, i, j)),
        out_shape=jax.ShapeDtypeStruct((b, seq, Q_WIDTH), F32),
        compiler_params=_params("parallel", "parallel", "arbitrary"),
        name=name,
    )(q, k, v)


def _swap_store(o_ref, slabs, rows):
    t = jnp.swapaxes(jnp.stack(slabs), 0, 1).astype(BF16)
    o_ref[:, 0] = t[:rows]
    o_ref[:, 1] = t[rows:]


def _filter_kernel(z_ref, w1_ref, b1_ref, w2_ref, b2_ref, w3_ref, b3_ref, w4_ref, fr_ref, dl_ref,
                   taps_ref, s_ref, *, nh):
    i = pl.program_id(0)
    z = z_ref[...]
    fr = fr_ref[...]
    h = _sin(fr * (_dot3(z, w1_ref[...]) + b1_ref[...]))
    h = _sin(fr * (_dot3(h, w2_ref[...]) + b2_ref[...]))
    h = _sin(fr * (_dot3(h, w3_ref[...]) + b3_ref[...]))
    h = _dot(h.astype(BF16), w4_ref[...].astype(BF16)) * jnp.exp(-z[:, 0:1] * jnp.abs(dl_ref[...]))
    first = (lax.broadcasted_iota(jnp.int32, (h.shape[0], HY_WIDTH), 0) == 0) & (i == 0)
    sums = []
    for od in range(4):
        hod = h[:, od * HY_WIDTH:(od + 1) * HY_WIDTH]
        if od % 2 == 1:
            hod = jnp.where(first, 0.0, hod)
        sums.append(jnp.sum(jnp.abs(hod), axis=0, keepdims=True))
        for c in range(HY_WIDTH // LANES):
            for g in range(h.shape[0] // nh):
                taps_ref[od // 2, od % 2, c, g] = hod[g * nh:(g + 1) * nh, c * LANES:(c + 1) * LANES].astype(BF16)
    ssum = jnp.concatenate(sums, axis=1)

    @pl.when(i == 0)
    def _():
        s_ref[...] = ssum

    @pl.when(i > 0)
    def _():
        s_ref[...] += ssum


def _filter_taps(zt, w1, b1, w2, b2, w3, b3, w4, freq, deltas):
    seq = zt.shape[0]
    nh = seq // LANES
    groups = FILT_TILE // nh
    cq = HY_WIDTH // LANES
    full = lambda a: pl.BlockSpec(a.shape, lambda i: (0,) * a.ndim)
    return pl.pallas_call(
        functools.partial(_filter_kernel, nh=nh),
        grid=(seq // FILT_TILE,),
        in_specs=[pl.BlockSpec((FILT_TILE, FILTER_PAD), lambda i: (i, 0))]
        + [full(a) for a in (w1, b1, w2, b2, w3, b3, w4, freq, deltas)],
        out_specs=[pl.BlockSpec((2, 2, cq, groups, nh, LANES), lambda i: (0, 0, 0, i, 0, 0)),
                   pl.BlockSpec((1, 4 * HY_WIDTH), lambda i: (0, 0))],
        out_shape=[jax.ShapeDtypeStruct((2, 2, cq, LANES, nh, LANES), BF16),
                   jax.ShapeDtypeStruct((1, 4 * HY_WIDTH), F32)],
        compiler_params=_params("arbitrary"),
        name="filter_taps",
    )(zt, w1, b1, w2, b2, w3, b3, w4, freq, deltas)


def _fft_outer_filter_kernel(fw_ref, bw_ref, bw0_ref, f_ref, f0_ref, a_ref, *, jb, n1):
    first_block = pl.program_id(1) == 0
    lanes = range(HY_WIDTH // LANES)
    slabs = []
    for j in range(jb):
        bw = [bw0_ref[c, 0] for c in lanes] if j == 0 else [bw_ref[c, jb - j] for c in lanes]
        x = jnp.concatenate([jnp.concatenate([fw_ref[c, j] for c in lanes], axis=1),
                             jnp.concatenate(bw, axis=1)], axis=0)
        mat = jnp.where(first_block, f0_ref[...], f_ref[...]) if j == 0 else f_ref[...]
        slabs.append(_dot(mat, x))
    _swap_store(a_ref, slabs, n1)


def _fft_outer_filter(taps, f1t, f1t0):
    n_ord, _, cq, _, nh, _ = taps.shape
    n1 = 2 * nh
    jb = SWAP
    jbn = LANES // jb
    tap_spec = lambda d, blk: pl.BlockSpec((None, None, cq, jb, nh, LANES), lambda o, j: (o, d, 0, blk(j), 0, 0))
    return pl.pallas_call(
        functools.partial(_fft_outer_filter_kernel, jb=jb, n1=n1),
        grid=(n_ord, jbn),
        in_specs=[tap_spec(0, lambda j: j), tap_spec(1, lambda j: jbn - 1 - j), tap_spec(1, lambda j: (jbn - j) % jbn),
                  pl.BlockSpec(f1t.shape, lambda o, j: (0, 0)), pl.BlockSpec(f1t0.shape, lambda o, j: (0, 0))],
        out_specs=pl.BlockSpec((None, n1, 2, jb, HY_WIDTH), lambda o, j: (o, 0, 0, j, 0)),
        out_shape=jax.ShapeDtypeStruct((n_ord, n1, 2, LANES, HY_WIDTH), BF16),
        compiler_params=_params("parallel", "parallel"),
        name="fft_outer_filter",
    )(taps, taps, taps, f1t, f1t0)


def _time_slabs(z_ref, jb):
    z = z_ref[...].astype(F32)
    two, na, _, width = z.shape
    z = z.reshape(two, na, jb, SUBLANES, width)
    return [z[:, :, j].reshape(two * na * SUBLANES, width) for j in range(jb)]


def _fft_outer_kernel(z_ref, f_ref, a_ref, *, jb, n1):
    _swap_store(a_ref, [_dot(f_ref[...], x.astype(BF16)) for x in _time_slabs(z_ref, jb)], n1)


def _fft_outer(z, f1):
    _, na, _, width = z.shape
    n1 = f1.shape[0] // 2
    jb = SWAP
    return pl.pallas_call(
        functools.partial(_fft_outer_kernel, jb=jb, n1=n1),
        grid=(LANES // jb,),
        in_specs=[pl.BlockSpec((2, na, jb * SUBLANES, width), lambda j: (0, 0, j, 0)),
                  pl.BlockSpec(f1.shape, lambda j: (0, 0))],
        out_specs=pl.BlockSpec((n1, 2, jb, width), lambda j: (0, 0, j, 0)),
        out_shape=jax.ShapeDtypeStruct((n1, 2, LANES, width), BF16),
        compiler_params=_params("parallel"),
        name="fft_outer",
    )(z, f1)


def _fft_inner_kernel(a_ref, af_ref, s_ref, g_ref, b_ref, *, kb, n_fft):
    scale = 1.0 / ((s_ref[0] + s_ref[1]) * n_fft)
    hs = [_dot(g_ref[kk], af_ref[kk].reshape(2 * LANES, HY_WIDTH)) for kk in range(kb)]
    cs = [_dot(g_ref[kk], a_ref[kk].reshape(2 * LANES, HY_WIDTH)) for kk in range(kb)]
    ps = []
    for h, c in zip(hs, cs):
        hr, hi = h[:LANES] * scale, h[LANES:] * scale
        cr, ci = c[:LANES], c[LANES:]
        ps.append(jnp.concatenate([cr * hr - ci * hi, cr * hi + ci * hr], axis=0).astype(BF16))
    _swap_store(b_ref, [lax.dot_general(g_ref[kk], p, (((0,), (0,)), ((), ())), preferred_element_type=F32)
                        for kk, p in enumerate(ps)], LANES)


def _fft_inner(a, af, s, gf, order, n_fft):
    n1 = a.shape[0]
    kb = SWAP
    a_blk = (kb, 2, LANES, HY_WIDTH)
    return pl.pallas_call(
        functools.partial(_fft_inner_kernel, kb=kb, n_fft=n_fft),
        grid=(n1 // kb,),
        in_specs=[pl.BlockSpec(a_blk, lambda k: (k, 0, 0, 0)),
                  pl.BlockSpec((None,) + a_blk, lambda k: (order, k, 0, 0, 0)),
                  pl.BlockSpec((None, 2, 1, HY_WIDTH), lambda k: (order, 0, 0, 0)),
                  pl.BlockSpec((kb, 2 * LANES, 2 * LANES), lambda k: (k, 0, 0))],
        out_specs=pl.BlockSpec((LANES, 2, kb, HY_WIDTH), lambda k: (0, 0, k, 0)),
        out_shape=jax.ShapeDtypeStruct((LANES, 2, n1, HY_WIDTH), BF16),
        compiler_params=_params("parallel"),
        name="fft_inner",
    )(a, af, s, gf)


def _ifft_outer_gate_kernel(b_ref, f_ref, z_ref, g_ref, skip_ref, o_ref, *, jb, nh, token_major):
    skip = skip_ref[...]
    vals = [g * (_dot(f_ref[...], b_ref[j].reshape(4 * nh, b_ref.shape[-1])) + skip * z)
            for j, (z, g) in enumerate(zip(_time_slabs(z_ref, jb), _time_slabs(g_ref, jb)))]
    if token_major:
        t = jnp.swapaxes(jnp.stack(vals), 0, 1).astype(BF16)
        o_ref[0] = t[:nh]
        o_ref[1] = t[nh:]
    else:
        for b in range(2):
            for a in range(nh // SUBLANES):
                rows = slice(b * nh + a * SUBLANES, b * nh + (a + 1) * SUBLANES)
                o_ref[b, a] = jnp.concatenate([v[rows] for v in vals], axis=0).astype(BF16)


def _ifft_outer_gate(bsp, f1inv, z, gate, skip_row, token_major):
    _, na, _, width = z.shape
    nh = na * SUBLANES
    jb = SWAP
    t_spec = pl.BlockSpec((2, na, jb * SUBLANES, width), lambda j: (0, 0, j, 0))
    if token_major:
        out_spec = pl.BlockSpec((2, nh, jb, width), lambda j: (0, 0, j, 0))
        out_shape = jax.ShapeDtypeStruct((2, nh, LANES, width), BF16)
    else:
        out_spec, out_shape = t_spec, jax.ShapeDtypeStruct(z.shape, BF16)
    return pl.pallas_call(
        functools.partial(_ifft_outer_gate_kernel, jb=jb, nh=nh, token_major=token_major),
        grid=(LANES // jb,),
        in_specs=[pl.BlockSpec((jb,) + bsp.shape[1:], lambda j: (j, 0, 0, 0)),
                  pl.BlockSpec(f1inv.shape, lambda j: (0, 0)), t_spec, t_spec,
                  pl.BlockSpec((1, width), lambda j: (0, 0))],
        out_specs=out_spec,
        out_shape=out_shape,
        compiler_params=_params("parallel"),
        name="ifft_outer_gate",
    )(bsp, f1inv, z, gate, skip_row)


def _dft_tables(seq):
    n = 2 * seq
    n1 = n // LANES
    nh = n1 // 2
    k1 = jnp.arange(n1, dtype=jnp.int32)
    r = jnp.arange(nh, dtype=jnp.int32)

    def outer(cols):
        ang = ((k1[:, None] * cols[None, :]) % n1).astype(F32) * (2.0 * math.pi / n1)
        return jnp.cos(ang), jnp.sin(ang)

    fc, fs = outer(r)
    bc, bs = outer(n1 - 1 - r)
    zc, zs = outer((n1 - r) % n1)
    zc, zs = zc * (r > 0), zs * (r > 0)
    f1 = jnp.concatenate([jnp.concatenate([fc, fs], 1), jnp.concatenate([-fs, fc], 1)], 0).astype(BF16)
    f1t = jnp.concatenate([jnp.concatenate([fc, bc], 1), jnp.concatenate([-fs, -bs], 1)], 0).astype(BF16)
    f1t0 = jnp.concatenate([jnp.concatenate([fc, zc], 1), jnp.concatenate([-fs, -zs], 1)], 0).astype(BF16)
    f1inv = jnp.concatenate([jnp.concatenate([fc.T, -fs.T], 1), jnp.concatenate([fs.T, fc.T], 1)], 0).astype(BF16)
    n2 = jnp.arange(LANES, dtype=jnp.int32)
    alpha = ((k1[:, None] * n2[None, :]) % n).astype(F32) * (2.0 * math.pi / n)
    beta = ((n2[:, None] * n2[None, :]) % LANES).astype(F32) * (2.0 * math.pi / LANES)
    ca, sa = jnp.cos(alpha)[:, None, :], jnp.sin(alpha)[:, None, :]
    cb, sb = jnp.cos(beta)[None], jnp.sin(beta)[None]
    c, s = ca * cb - sa * sb, sa * cb + ca * sb
    gf = jnp.concatenate([jnp.concatenate([c, s], 2), jnp.concatenate([-s, c], 2)], 1).astype(BF16)
    return f1, f1t, f1t0, f1inv, gf


def _filter_features(seq):
    nh = seq // LANES
    pos = (jnp.arange(LANES, dtype=jnp.int32)[:, None] + LANES * jnp.arange(nh, dtype=jnp.int32)[None, :]).reshape(seq)
    t = (pos.astype(F32) / (seq - 1))[:, None]
    bands = (FILTER_EMB - 1) // 2
    w_ang = 2.0 * math.pi * pos.astype(F32) / seq
    band_f = jnp.linspace(1e-4, bands - 1, bands, dtype=F32)
    ang = w_ang[:, None] * band_f[None, :]
    z = jnp.concatenate([t, jnp.cos(ang), -jnp.sin(ang)], axis=-1)
    return jnp.pad(z, ((0, 0), (0, FILTER_PAD - FILTER_EMB)))


def _hyena(hv, x1, x2, fw1, fb1, fw2, fb2, fw3, fb3, fw4, ffreq, fdeltas, skip_d):
    b, na, rows, width = hv.shape
    assert b == 2, "the two batch rows ride as the real and imaginary parts of one transform"
    seq = na * rows
    n = 2 * seq
    f1, f1t, f1t0, f1inv, gf = _dft_tables(seq)

    w1p = jnp.pad(fw1, ((0, FILTER_PAD - FILTER_EMB), (0, 0)))
    taps, s = _filter_taps(_filter_features(seq), w1p, fb1[None], fw2, fb2[None], fw3, fb3[None], fw4,
                           ffreq[None], fdeltas[None])
    af = _fft_outer_filter(taps, f1t, f1t0)
    s = s.reshape(2, 2, 1, HY_WIDTH)

    zz = hv
    for order, gate in enumerate((x1, x2)):
        bsp = _fft_inner(_fft_outer(zz, f1), af, s, gf, order, n)
        zz = _ifft_outer_gate(bsp, f1inv, zz, gate, skip_d[order][None], token_major=order == 1)
    return zz.reshape(b, seq, width)


def _out_mlp_kernel(x_ref, att_ref, hy_ref, ga_ref, gh_ref, wo_ref, g2_ref, w1_ref, w2_ref, gf_ref, o_ref):
    half = att_ref.shape[1]
    for r in range(x_ref.shape[0] // MLP_ROWS):
        rows = slice(r * MLP_ROWS, (r + 1) * MLP_ROWS)
        a = _rms(att_ref[rows, :], ga_ref[...]).astype(BF16)
        hyn = _rms(hy_ref[rows, :].astype(F32), gh_ref[...]).astype(BF16)
        h = x_ref[rows, :] + _dot(a, wo_ref[:half, :]) + _dot(hyn, wo_ref[half:, :])
        m = _rms(h, g2_ref[...]).astype(BF16)
        t = jnp.square(jnp.maximum(_dot(m, w1_ref[...]), 0.0)).astype(BF16)
        h = h + _dot(t, w2_ref[...])
        o_ref[rows, :] = _rms(h, gf_ref[...])


def _out_mlp(x, att, hy, ga, gh, w_out, g2, w1, w2, gfin):
    b, seq, d = x.shape
    tm = ROW_TILE
    row = lambda w: pl.BlockSpec((None, tm, w), lambda bi, i: (bi, i, 0))
    full = lambda a: pl.BlockSpec(a.shape, lambda bi, i: (0,) * a.ndim, pipeline_mode=pl.Buffered(1))
    return pl.pallas_call(
        _out_mlp_kernel,
        grid=(b, seq // tm),
        in_specs=[row(d), row(att.shape[-1]), row(hy.shape[-1]),
                  full(ga), full(gh), full(w_out), full(g2), full(w1), full(w2), full(gfin)],
        out_specs=row(d),
        out_shape=jax.ShapeDtypeStruct((b, seq, d), F32),
        compiler_params=_params("parallel", "parallel"),
        name="out_mlp",
    )(x, att, hy, ga, gh, w_out, g2, w1, w2, gfin)


def _rope_tables(seq):
    rows = seq // GRID_W
    row = jnp.repeat(jnp.arange(rows, dtype=F32), GRID_W)
    col = jnp.tile(jnp.arange(GRID_W, dtype=F32), rows)
    half = HEAD_DIM // 2
    inv_freq = ROPE_THETA ** (-jnp.arange(0, half, 2, dtype=F32) / half)
    ang_r = row[:, None] * inv_freq[None, :]
    ang_c = col[:, None] * inv_freq[None, :]
    cos = jnp.concatenate([jnp.cos(ang_r)] * 2 + [jnp.cos(ang_c)] * 2, axis=1)
    sin = jnp.concatenate([-jnp.sin(ang_r), jnp.sin(ang_r), -jnp.sin(ang_c), jnp.sin(ang_c)], axis=1)
    return jnp.tile(cos, (1, LANES // HEAD_DIM)), jnp.tile(sin, (1, LANES // HEAD_DIM))


def _head_mean_matrix(width):
    head = jnp.arange(width, dtype=jnp.int32) // HEAD_DIM
    return jnp.where(head[:, None] == head[None, :], 1.0 / HEAD_DIM, 0.0).astype(BF16)


def kernel(x, norm1_g, w_in, q_norm_g, k_norm_g, hy_conv_w, hy_conv_b, filt_w1, filt_b1, filt_w2, filt_b2, filt_w3, filt_b3, filt_w4, filt_freq, filt_deltas, hy_skip_d, attn_out_g, hy_out_g, w_out, norm2_g, w_mlp_in, w_mlp_out, final_g):
    seq = x.shape[1]
    cos_t, sin_t = _rope_tables(seq)
    bdq = _head_mean_matrix(Q_WIDTH)
    bdk = _head_mean_matrix(KV_WIDTH)
    h = x
    for i in range(norm1_g.shape[0]):
        w_qkv = w_in[i][:, :Q_WIDTH + 2 * KV_WIDTH].astype(BF16)
        w_u = w_in[i][:, Q_WIDTH + 2 * KV_WIDTH:].astype(BF16)
        bound = (1.01 * HEAD_DIM ** 0.5 * LOG2E) * jnp.max(jnp.abs(q_norm_g[i])) * jnp.max(jnp.abs(k_norm_g[i]))
        q, k, v, hv, x1, x2 = _in_proj(
            h, norm1_g[i][None], w_qkv, w_u, jnp.tile(q_norm_g[i], N_Q_HEADS)[None],
            jnp.tile(k_norm_g[i], N_KV_HEADS)[None], cos_t, sin_t, bdq, bdk, hy_conv_w[i], hy_conv_b[i][None],
            bound.reshape(1, 1))
        att = lax.cond(
            bound <= MAX_FIXED_SHIFT,
            functools.partial(_attention, _attn_bounded_kernel, KV_CHUNK_BOUNDED, "attention_bounded"),
            functools.partial(_attention, _attn_online_kernel, KV_CHUNK, "attention_online"),
            q, k, v)
        hy = _hyena(hv, x1, x2, filt_w1[i], filt_b1[i], filt_w2[i], filt_b2[i], filt_w3[i], filt_b3[i],
                    filt_w4[i], filt_freq[i], filt_deltas[i], hy_skip_d[i])
        last = i == norm1_g.shape[0] - 1
        assert last, "single-layer trunk"
        h = _out_mlp(h, att, hy, attn_out_g[i][None], hy_out_g[i][None], w_out[i].astype(BF16), norm2_g[i][None],
                     w_mlp_in[i].astype(BF16), w_mlp_out[i].astype(BF16), final_g[None])
    return h
```

```python
import functools
import math

import jax
import jax.numpy as jnp
import numpy as np
from jax import lax
from jax.experimental import pallas as pl
from jax.experimental.pallas import tpu as pltpu

F32 = jnp.float32
BF16 = jnp.bfloat16

HEAD_DIM = 64
N_Q_HEADS = 8
N_KV_HEADS = 2
Q_WIDTH = N_Q_HEADS * HEAD_DIM
KV_WIDTH = N_KV_HEADS * HEAD_DIM
HY_WIDTH = 512
GRID_W = 64
ROPE_THETA = 10000.0
FILTER_EMB = 33
FILTER_PAD = 64
EPS = 1e-6
LOG2E = math.log2(math.e)
MAX_FIXED_SHIFT = 50.0

LANES = 128
SUBLANES = 8
TX_ROWS = SUBLANES * LANES
SWAP = 16
HALO = 16
VMEM_LIMIT = 56 * 1024 * 1024

ROW_TILE = TX_ROWS
MLP_ROWS = 512
Q_TILE = 128
KV_CHUNK = 512
KV_CHUNK_BOUNDED = 8192
FILT_TILE = 512


def _dot(a, b):
    return jnp.dot(a, b, preferred_element_type=F32)


def _dot3(a, b):
    ah = a.astype(BF16)
    al = (a - ah.astype(F32)).astype(BF16)
    bh = b.astype(BF16)
    bl = (b - bh.astype(F32)).astype(BF16)
    return _dot(ah, bh) + (_dot(ah, bl) + _dot(al, bh))


def _rms(v, g):
    return v * lax.rsqrt(jnp.mean(v * v, axis=-1, keepdims=True) + EPS) * g


_PI_A = 3.140625
_PI_B = 9.67502593994140625e-4
_PI_C = 1.509957990978376432e-7
_SIN_SMALL = 8192.0
_SIN_TAYLOR = (-1.0 / 6.0, 1.0 / 120.0, -1.0 / 5040.0, 1.0 / 362880.0, -1.0 / 39916800.0, 1.0 / 6227020800.0)


def _sin_small(x):
    k = jnp.floor(x * (1.0 / math.pi) + 0.5)
    r = ((x - k * _PI_A) - k * _PI_B) - k * _PI_C
    r2 = r * r
    poly = _SIN_TAYLOR[-1]
    for coef in _SIN_TAYLOR[-2::-1]:
        poly = poly * r2 + coef
    s = r + r * r2 * poly
    return jnp.where((k.astype(jnp.int32) & 1) == 1, -s, s)


def _sin(x):
    return lax.cond(jnp.max(jnp.abs(x)) <= _SIN_SMALL, _sin_small, jnp.sin, x)


def _params(*sem):
    return pltpu.CompilerParams(dimension_semantics=sem, vmem_limit_bytes=VMEM_LIMIT)


def _in_proj_kernel(x_ref, xp_ref, xn_ref, g1_ref, wqkv_ref, wu_ref, qg_ref, kg_ref, cos_ref, sin_ref,
                    bdq_ref, bdk_ref, cw_ref, cb_ref, bound_ref,
                    q_ref, k_ref, v_ref, hv_ref, x1_ref, x2_ref, a_scr, *, tm, n_tiles):
    i = pl.program_id(1)
    g1 = g1_ref[...]
    a_scr[HALO:HALO + tm, :] = _rms(x_ref[...], g1).astype(BF16)
    prev = jnp.where(i > 0, _rms(xp_ref[...], g1), 0.0)
    nxt = jnp.where(i < n_tiles - 1, _rms(xn_ref[...], g1), 0.0)
    a_scr[0:HALO, :] = prev.astype(BF16)
    a_scr[HALO + tm:, :] = nxt.astype(BF16)

    qkv = _dot(a_scr[HALO:HALO + tm, :], wqkv_ref[...])
    cos = cos_ref[...]
    sin = sin_ref[...]
    lane = lax.broadcasted_iota(jnp.int32, (tm, LANES), 1)
    first16 = (lane % 32) < 16
    low_half = lane < HEAD_DIM

    def norm_rope(blk, ms, gain):
        y = blk * lax.rsqrt(ms + EPS) * gain
        partner = jnp.where(first16, pltpu.roll(y, LANES - 16, 1), pltpu.roll(y, 16, 1))
        return y * cos + partner * sin

    q = qkv[:, :Q_WIDTH]
    q_ms = _dot((q * q).astype(BF16), bdq_ref[...])
    qg = qg_ref[...]
    neg_bound = -bound_ref[0, 0]
    extra = lane == HEAD_DIM
    for m in range(Q_WIDTH // LANES):
        sl = slice(m * LANES, (m + 1) * LANES)
        blk = norm_rope(q[:, sl], q_ms[:, sl], qg[:, sl]) * (HEAD_DIM ** -0.5 * LOG2E)
        q_ref[:, 2 * m * LANES:(2 * m + 1) * LANES] = jnp.where(extra, neg_bound, blk).astype(BF16)
        q_ref[:, (2 * m + 1) * LANES:(2 * m + 2) * LANES] = jnp.where(
            extra, neg_bound, pltpu.roll(blk, HEAD_DIM, 1)).astype(BF16)

    k = qkv[:, Q_WIDTH:Q_WIDTH + KV_WIDTH]
    k_ms = _dot((k * k).astype(BF16), bdk_ref[...])
    k = norm_rope(k, k_ms, kg_ref[...])
    v = qkv[:, Q_WIDTH + KV_WIDTH:]
    pad = jnp.where(extra, 1.0, 0.0)
    k_ref[0] = jnp.where(low_half, k, pad).astype(BF16)
    k_ref[1] = jnp.where(low_half, pltpu.roll(k, HEAD_DIM, 1), pad).astype(BF16)
    v_ref[0] = jnp.where(low_half, v, pad).astype(BF16)
    v_ref[1] = jnp.where(low_half, pltpu.roll(v, HEAD_DIM, 1), pad).astype(BF16)

    first_p = lax.broadcasted_iota(jnp.int32, (SUBLANES, HY_WIDTH), 0) == 0
    last_p = lax.broadcasted_iota(jnp.int32, (SUBLANES, HY_WIDTH), 0) == SUBLANES - 1
    for c, out in enumerate((hv_ref, x1_ref, x2_ref)):
        cs = slice(c * HY_WIDTH, (c + 1) * HY_WIDTH)
        u = _dot(a_scr[...], wu_ref[:, cs])
        w = cw_ref[:, cs]
        ut = jnp.swapaxes(u[HALO:HALO + tm].reshape(SUBLANES, LANES, HY_WIDTH), 0, 1)
        before = jnp.where(first_p, u[HALO - 1:HALO], pltpu.roll(ut[LANES - 1], 1, 0))
        after = jnp.where(last_p, u[HALO + tm:HALO + tm + 1], pltpu.roll(ut[0], SUBLANES - 1, 0))
        prev = jnp.concatenate([before[None], ut[:-1]], axis=0)
        nxt = jnp.concatenate([ut[1:], after[None]], axis=0)
        y = (prev * w[0:1] + ut * w[1:2] + nxt * w[2:3]) + cb_ref[:, cs]
        out[...] = y.reshape(tm, HY_WIDTH).astype(BF16)


def _in_proj(x, g1, w_qkv, w_u, qg, kg, cos_t, sin_t, bdq, bdk, conv_w, conv_b, bound):
    b, seq, d = x.shape
    tm = ROW_TILE
    n_tiles = seq // tm
    hb = tm // HALO
    full = lambda shape: pl.BlockSpec(shape, lambda bi, i: (0,) * len(shape))
    row_out = lambda w: pl.BlockSpec((None, tm, w), lambda bi, i: (bi, i, 0))
    kv_out = pl.BlockSpec((None, N_KV_HEADS, tm, LANES), lambda bi, i: (bi, 0, i, 0))
    tx_out = pl.BlockSpec((None, None, tm, HY_WIDTH), lambda bi, i: (bi, i, 0, 0))
    tx_shape = jax.ShapeDtypeStruct((b, n_tiles, tm, HY_WIDTH), BF16)
    return pl.pallas_call(
        functools.partial(_in_proj_kernel, tm=tm, n_tiles=n_tiles),
        grid=(b, n_tiles),
        in_specs=[
            pl.BlockSpec((None, tm, d), lambda bi, i: (bi, i, 0)),
            pl.BlockSpec((None, HALO, d), lambda bi, i: (bi, jnp.maximum(i * hb - 1, 0), 0)),
            pl.BlockSpec((None, HALO, d), lambda bi, i: (bi, jnp.minimum((i + 1) * hb, seq // HALO - 1), 0)),
            full((1, d)),
            full(w_qkv.shape),
            full(w_u.shape),
            full((1, Q_WIDTH)),
            full((1, KV_WIDTH)),
            pl.BlockSpec((tm, LANES), lambda bi, i: (i, 0)),
            pl.BlockSpec((tm, LANES), lambda bi, i: (i, 0)),
            full(bdq.shape),
            full(bdk.shape),
            full(conv_w.shape),
            full(conv_b.shape),
            pl.BlockSpec(memory_space=pltpu.SMEM),
        ],
        out_specs=[row_out(2 * Q_WIDTH), kv_out, kv_out, tx_out, tx_out, tx_out],
        out_shape=[
            jax.ShapeDtypeStruct((b, seq, 2 * Q_WIDTH), BF16),
            jax.ShapeDtypeStruct((b, N_KV_HEADS, seq, LANES), BF16),
            jax.ShapeDtypeStruct((b, N_KV_HEADS, seq, LANES), BF16),
            tx_shape, tx_shape, tx_shape,
        ],
        scratch_shapes=[pltpu.VMEM((tm + 2 * HALO, d), BF16)],
        compiler_params=_params("parallel", "arbitrary"),
        name="in_proj",
    )(x, x, x, g1, w_qkv, w_u, qg, kg, cos_t, sin_t, bdq, bdk, conv_w, conv_b, bound)


def _stack_heads(q_ref):
    group = N_Q_HEADS // N_KV_HEADS
    return jnp.concatenate([q_ref[:, g * LANES:(g + 1) * LANES] for g in range(group)], axis=0)


def _store_heads(o_ref, o, tq):
    group = N_Q_HEADS // N_KV_HEADS
    pairs = [o[(2 * p) * tq:(2 * p + 1) * tq] + pltpu.roll(o[(2 * p + 1) * tq:(2 * p + 2) * tq], HEAD_DIM, 1)
             for p in range(group // 2)]
    o_ref[...] = jnp.concatenate(pairs, axis=1)


def _attn_bounded_kernel(q_ref, k_ref, v_ref, o_ref, *, tq, sc, n_chunks):
    q = _stack_heads(q_ref)
    rows = q.shape[0]

    def body(c, acc):
        off = pl.multiple_of(c * sc, sc)
        s = lax.dot_general(q, k_ref[pl.ds(off, sc), :], (((1,), (1,)), ((), ())), preferred_element_type=F32)
        return acc + _dot(jnp.exp2(s).astype(BF16), v_ref[pl.ds(off, sc), :])

    acc = lax.fori_loop(0, n_chunks, body, jnp.zeros((rows, LANES), F32))
    lane = lax.broadcasted_iota(jnp.int32, (rows, LANES), 1)
    _store_heads(o_ref, jnp.where(lane < HEAD_DIM, acc / acc[:, HEAD_DIM:HEAD_DIM + 1], 0.0), tq)


def _attn_online_kernel(q_ref, k_ref, v_ref, o_ref, *, tq, sc, n_chunks):
    q = _stack_heads(q_ref)
    rows = q.shape[0]

    def body(c, carry):
        m, l, acc = carry
        off = pl.multiple_of(c * sc, sc)
        s = lax.dot_general(q, k_ref[pl.ds(off, sc), :], (((1,), (1,)), ((), ())), preferred_element_type=F32)
        m_new = jnp.maximum(m, jnp.max(s, axis=1, keepdims=True))
        p = jnp.exp2(s - m_new)
        alpha = jnp.exp2(m - m_new)
        l = alpha * l + jnp.sum(p, axis=1, keepdims=True)
        acc = alpha * acc + _dot(p.astype(BF16), v_ref[pl.ds(off, sc), :])
        return m_new, l, acc

    init = (jnp.full((rows, 1), jnp.finfo(F32).min, F32), jnp.zeros((rows, 1), F32), jnp.zeros((rows, LANES), F32))
    _, l, acc = lax.fori_loop(0, n_chunks, body, init)
    lane = lax.broadcasted_iota(jnp.int32, (rows, LANES), 1)
    _store_heads(o_ref, jnp.where(lane < HEAD_DIM, acc / l, 0.0), tq)


def _attention(body, sc, name, q, k, v):
    b, seq, _ = q.shape
    tq = min(Q_TILE, seq)
    sc = min(sc, seq)
    group = N_Q_HEADS // N_KV_HEADS
    return pl.pallas_call(
        functools.partial(body, tq=tq, sc=sc, n_chunks=seq // sc),
        grid=(b, N_KV_HEADS, seq // tq),
        in_specs=[
            pl.BlockSpec((None, tq, group * LANES), lambda bi, j, i: (bi, i, j)),
            pl.BlockSpec((None, None, seq, LANES), lambda bi, j, i: (bi, j, 0, 0)),
            pl.BlockSpec((None, None, seq, LANES), lambda bi, j, i: (bi, j, 0, 0)),
        ],
        out_specs=pl.BlockSpec((None, tq, group * HEAD_DIM), lambda bi, j, i: (bi, i, j)),
        out_shape=jax.ShapeDtypeStruct((b, seq, Q_WIDTH), F32),
        compiler_params=_params("parallel", "parallel", "arbitrary"),
        name=name,
    )(q, k, v)


def _swap_store(o_ref, slabs, rows):
    t = jnp.swapaxes(jnp.stack(slabs), 0, 1).astype(BF16)
    o_ref[:, 0] = t[:rows]
    o_ref[:, 1] = t[rows:]


def _filter_kernel(z_ref, w1_ref, b1_ref, w2_ref, b2_ref, w3_ref, b3_ref, w4_ref, fr_ref, dl_ref,
                   taps_ref, s_ref, *, nh):
    i = pl.program_id(0)
    z = z_ref[...]
    fr = fr_ref[...]
    h = _sin(fr * (_dot3(z, w1_ref[...]) + b1_ref[...]))
    h = _sin(fr * (_dot3(h, w2_ref[...]) + b2_ref[...]))
    h = _sin(fr * (_dot3(h, w3_ref[...]) + b3_ref[...]))
    h = _dot(h.astype(BF16), w4_ref[...].astype(BF16)) * jnp.exp(-z[:, 0:1] * jnp.abs(dl_ref[...]))
    first = (lax.broadcasted_iota(jnp.int32, (h.shape[0], HY_WIDTH), 0) == 0) & (i == 0)
    sums = []
    for od in range(4):
        hod = h[:, od * HY_WIDTH:(od + 1) * HY_WIDTH]
        if od % 2 == 1:
            hod = jnp.where(first, 0.0, hod)
        sums.append(jnp.sum(jnp.abs(hod), axis=0, keepdims=True))
        for c in range(HY_WIDTH // LANES):
            for g in range(h.shape[0] // nh):
                taps_ref[od // 2, od % 2, c, g] = hod[g * nh:(g + 1) * nh, c * LANES:(c + 1) * LANES].astype(BF16)
    ssum = jnp.concatenate(sums, axis=1)

    @pl.when(i == 0)
    def _():
        s_ref[...] = ssum

    @pl.when(i > 0)
    def _():
        s_ref[...] += ssum


def _filter_taps(zt, w1, b1, w2, b2, w3, b3, w4, freq, deltas):
    seq = zt.shape[0]
    nh = seq // LANES
    groups = FILT_TILE // nh
    cq = HY_WIDTH // LANES
    full = lambda a: pl.BlockSpec(a.shape, lambda i: (0,) * a.ndim)
    return pl.pallas_call(
        functools.partial(_filter_kernel, nh=nh),
        grid=(seq // FILT_TILE,),
        in_specs=[pl.BlockSpec((FILT_TILE, FILTER_PAD), lambda i: (i, 0))]
        + [full(a) for a in (w1, b1, w2, b2, w3, b3, w4, freq, deltas)],
        out_specs=[pl.BlockSpec((2, 2, cq, groups, nh, LANES), lambda i: (0, 0, 0, i, 0, 0)),
                   pl.BlockSpec((1, 4 * HY_WIDTH), lambda i: (0, 0))],
        out_shape=[jax.ShapeDtypeStruct((2, 2, cq, LANES, nh, LANES), BF16),
                   jax.ShapeDtypeStruct((1, 4 * HY_WIDTH), F32)],
        compiler_params=_params("arbitrary"),
        name="filter_taps",
    )(zt, w1, b1, w2, b2, w3, b3, w4, freq, deltas)


def _fft_outer_filter_kernel(fw_ref, bw_ref, bw0_ref, f_ref, f0_ref, a_ref, *, jb, n1):
    first_block = pl.program_id(1) == 0
    lanes = range(HY_WIDTH // LANES)
    slabs = []
    for j in range(jb):
        bw = [bw0_ref[c, 0] for c in lanes] if j == 0 else [bw_ref[c, jb - j] for c in lanes]
        x = jnp.concatenate([jnp.concatenate([fw_ref[c, j] for c in lanes], axis=1),
                             jnp.concatenate(bw, axis=1)], axis=0)
        mat = jnp.where(first_block, f0_ref[...], f_ref[...]) if j == 0 else f_ref[...]
        slabs.append(_dot(mat, x))
    _swap_store(a_ref, slabs, n1)


def _fft_outer_filter(taps, f1t, f1t0):
    n_ord, _, cq, _, nh, _ = taps.shape
    n1 = 2 * nh
    jb = SWAP
    jbn = LANES // jb
    tap_spec = lambda d, blk: pl.BlockSpec((None, None, cq, jb, nh, LANES), lambda o, j: (o, d, 0, blk(j), 0, 0))
    return pl.pallas_call(
        functools.partial(_fft_outer_filter_kernel, jb=jb, n1=n1),
        grid=(n_ord, jbn),
        in_specs=[tap_spec(0, lambda j: j), tap_spec(1, lambda j: jbn - 1 - j), tap_spec(1, lambda j: (jbn - j) % jbn),
                  pl.BlockSpec(f1t.shape, lambda o, j: (0, 0)), pl.BlockSpec(f1t0.shape, lambda o, j: (0, 0))],
        out_specs=pl.BlockSpec((None, n1, 2, jb, HY_WIDTH), lambda o, j: (o, 0, 0, j, 0)),
        out_shape=jax.ShapeDtypeStruct((n_ord, n1, 2, LANES, HY_WIDTH), BF16),
        compiler_params=_params("parallel", "parallel"),
        name="fft_outer_filter",
    )(taps, taps, taps, f1t, f1t0)


def _time_slabs(z_ref, jb):
    z = z_ref[...].astype(F32)
    two, na, _, width = z.shape
    z = z.reshape(two, na, jb, SUBLANES, width)
    return [z[:, :, j].reshape(two * na * SUBLANES, width) for j in range(jb)]


def _fft_outer_kernel(z_ref, f_ref, a_ref, *, jb, n1):
    _swap_store(a_ref, [_dot(f_ref[...], x.astype(BF16)) for x in _time_slabs(z_ref, jb)], n1)


def _fft_outer(z, f1):
    _, na, _, width = z.shape
    n1 = f1.shape[0] // 2
    jb = SWAP
    return pl.pallas_call(
        functools.partial(_fft_outer_kernel, jb=jb, n1=n1),
        grid=(LANES // jb,),
        in_specs=[pl.BlockSpec((2, na, jb * SUBLANES, width), lambda j: (0, 0, j, 0)),
                  pl.BlockSpec(f1.shape, lambda j: (0, 0))],
        out_specs=pl.BlockSpec((n1, 2, jb, width), lambda j: (0, 0, j, 0)),
        out_shape=jax.ShapeDtypeStruct((n1, 2, LANES, width), BF16),
        compiler_params=_params("parallel"),
        name="fft_outer",
    )(z, f1)


def _fft_inner_kernel(a_ref, af_ref, s_ref, g_ref, b_ref, *, kb, n_fft):
    scale = 1.0 / ((s_ref[0] + s_ref[1]) * n_fft)
    hs = [_dot(g_ref[kk], af_ref[kk].reshape(2 * LANES, HY_WIDTH)) for kk in range(kb)]
    cs = [_dot(g_ref[kk], a_ref[kk].reshape(2 * LANES, HY_WIDTH)) for kk in range(kb)]
    ps = []
    for h, c in zip(hs, cs):
        hr, hi = h[:LANES] * scale, h[LANES:] * scale
        cr, ci = c[:LANES], c[LANES:]
        ps.append(jnp.concatenate([cr * hr - ci * hi, cr * hi + ci * hr], axis=0).astype(BF16))
    _swap_store(b_ref, [lax.dot_general(g_ref[kk], p, (((0,), (0,)), ((), ())), preferred_element_type=F32)
                        for kk, p in enumerate(ps)], LANES)


def _fft_inner(a, af, s, gf, order, n_fft):
    n1 = a.shape[0]
    kb = SWAP
    a_blk = (kb, 2, LANES, HY_WIDTH)
    return pl.pallas_call(
        functools.partial(_fft_inner_kernel, kb=kb, n_fft=n_fft),
        grid=(n1 // kb,),
        in_specs=[pl.BlockSpec(a_blk, lambda k: (k, 0, 0, 0)),
                  pl.BlockSpec((None,) + a_blk, lambda k: (order, k, 0, 0, 0)),
                  pl.BlockSpec((None, 2, 1, HY_WIDTH), lambda k: (order, 0, 0, 0)),
                  pl.BlockSpec((kb, 2 * LANES, 2 * LANES), lambda k: (k, 0, 0))],
        out_specs=pl.BlockSpec((LANES, 2, kb, HY_WIDTH), lambda k: (0, 0, k, 0)),
        out_shape=jax.ShapeDtypeStruct((LANES, 2, n1, HY_WIDTH), BF16),
        compiler_params=_params("parallel"),
        name="fft_inner",
    )(a, af, s, gf)


def _ifft_outer_gate_kernel(b_ref, f_ref, z_ref, g_ref, skip_ref, o_ref, *, jb, nh, token_major):
    skip = skip_ref[...]
    vals = [g * (_dot(f_ref[...], b_ref[j].reshape(4 * nh, b_ref.shape[-1])) + skip * z)
            for j, (z, g) in enumerate(zip(_time_slabs(z_ref, jb), _time_slabs(g_ref, jb)))]
    if token_major:
        t = jnp.swapaxes(jnp.stack(vals), 0, 1).astype(BF16)
        o_ref[0] = t[:nh]
        o_ref[1] = t[nh:]
    else:
        for b in range(2):
            for a in range(nh // SUBLANES):
                rows = slice(b * nh + a * SUBLANES, b * nh + (a + 1) * SUBLANES)
                o_ref[b, a] = jnp.concatenate([v[rows] for v in vals], axis=0).astype(BF16)


def _ifft_outer_gate(bsp, f1inv, z, gate, skip_row, token_major):
    _, na, _, width = z.shape
    nh = na * SUBLANES
    jb = SWAP
    t_spec = pl.BlockSpec((2, na, jb * SUBLANES, width), lambda j: (0, 0, j, 0))
    if token_major:
        out_spec = pl.BlockSpec((2, nh, jb, width), lambda j: (0, 0, j, 0))
        out_shape = jax.ShapeDtypeStruct((2, nh, LANES, width), BF16)
    else:
        out_spec, out_shape = t_spec, jax.ShapeDtypeStruct(z.shape, BF16)
    return pl.pallas_call(
        functools.partial(_ifft_outer_gate_kernel, jb=jb, nh=nh, token_major=token_major),
        grid=(LANES // jb,),
        in_specs=[pl.BlockSpec((jb,) + bsp.shape[1:], lambda j: (j, 0, 0, 0)),
                  pl.BlockSpec(f1inv.shape, lambda j: (0, 0)), t_spec, t_spec,
                  pl.BlockSpec((1, width), lambda j: (0, 0))],
        out_specs=out_spec,
        out_shape=out_shape,
        compiler_params=_params("parallel"),
        name="ifft_outer_gate",
    )(bsp, f1inv, z, gate, skip_row)


def _dft_tables(seq):
    n = 2 * seq
    n1 = n // LANES
    nh = n1 // 2
    k1 = np.arange(n1)
    r = np.arange(nh)

    def trig(rows, cols, period):
        ang = ((rows[:, None] * cols[None, :]) % period) * (2.0 * math.pi / period)
        return np.cos(ang), np.sin(ang)

    fc, fs = trig(k1, r, n1)
    bc, bs = trig(k1, n1 - 1 - r, n1)
    zc, zs = trig(k1, (n1 - r) % n1, n1)
    zc, zs = zc * (r > 0), zs * (r > 0)
    block = lambda a, b, c, d: jnp.asarray(np.block([[a, b], [c, d]]), F32).astype(BF16)
    f1 = block(fc, fs, -fs, fc)
    f1t = block(fc, bc, -fs, -bs)
    f1t0 = block(fc, zc, -fs, -zs)
    f1inv = block(fc.T, -fs.T, fs.T, fc.T)
    n2 = np.arange(LANES)
    ca, sa = (jnp.asarray(t, F32)[:, None, :] for t in trig(k1, n2, n))
    cb, sb = (jnp.asarray(t, F32)[None] for t in trig(n2, n2, LANES))
    c, s = ca * cb - sa * sb, sa * cb + ca * sb
    gf = jnp.concatenate([jnp.concatenate([c, s], 2), jnp.concatenate([-s, c], 2)], 1).astype(BF16)
    return f1, f1t, f1t0, f1inv, gf


def _filter_features(seq):
    nh = seq // LANES
    pos = (np.arange(LANES)[:, None] + LANES * np.arange(nh)[None, :]).reshape(seq).astype(np.float64)
    bands = (FILTER_EMB - 1) // 2
    ang = (2.0 * math.pi * pos / seq)[:, None] * np.linspace(1e-4, bands - 1, bands)[None, :]
    z = np.concatenate([(pos / (seq - 1))[:, None], np.cos(ang), -np.sin(ang)], axis=-1)
    return jnp.asarray(np.pad(z, ((0, 0), (0, FILTER_PAD - FILTER_EMB))), F32)


def _hyena(hv, x1, x2, fw1, fb1, fw2, fb2, fw3, fb3, fw4, ffreq, fdeltas, skip_d):
    b, na, rows, width = hv.shape
    assert b == 2, "the two batch rows ride as the real and imaginary parts of one transform"
    seq = na * rows
    n = 2 * seq
    f1, f1t, f1t0, f1inv, gf = _dft_tables(seq)

    w1p = jnp.pad(fw1, ((0, FILTER_PAD - FILTER_EMB), (0, 0)))
    taps, s = _filter_taps(_filter_features(seq), w1p, fb1[None], fw2, fb2[None], fw3, fb3[None], fw4,
                           ffreq[None], fdeltas[None])
    af = _fft_outer_filter(taps, f1t, f1t0)
    s = s.reshape(2, 2, 1, HY_WIDTH)

    zz = hv
    for order, gate in enumerate((x1, x2)):
        bsp = _fft_inner(_fft_outer(zz, f1), af, s, gf, order, n)
        zz = _ifft_outer_gate(bsp, f1inv, zz, gate, skip_d[order][None], token_major=order == 1)
    return zz.reshape(b, seq, width)


def _out_mlp_kernel(x_ref, att_ref, hy_ref, ga_ref, gh_ref, wo_ref, g2_ref, w1_ref, w2_ref, gf_ref, o_ref):
    half = att_ref.shape[1]
    for r in range(x_ref.shape[0] // MLP_ROWS):
        rows = slice(r * MLP_ROWS, (r + 1) * MLP_ROWS)
        a = _rms(att_ref[rows, :], ga_ref[...]).astype(BF16)
        hyn = _rms(hy_ref[rows, :].astype(F32), gh_ref[...]).astype(BF16)
        h = x_ref[rows, :] + _dot(a, wo_ref[:half, :]) + _dot(hyn, wo_ref[half:, :])
        m = _rms(h, g2_ref[...]).astype(BF16)
        t = jnp.square(jnp.maximum(_dot(m, w1_ref[...]), 0.0)).astype(BF16)
        h = h + _dot(t, w2_ref[...])
        o_ref[rows, :] = _rms(h, gf_ref[...])


def _out_mlp(x, att, hy, ga, gh, w_out, g2, w1, w2, gfin):
    b, seq, d = x.shape
    tm = ROW_TILE
    row = lambda w: pl.BlockSpec((None, tm, w), lambda bi, i: (bi, i, 0))
    full = lambda a: pl.BlockSpec(a.shape, lambda bi, i: (0,) * a.ndim, pipeline_mode=pl.Buffered(1))
    return pl.pallas_call(
        _out_mlp_kernel,
        grid=(b, seq // tm),
        in_specs=[row(d), row(att.shape[-1]), row(hy.shape[-1]),
                  full(ga), full(gh), full(w_out), full(g2), full(w1), full(w2), full(gfin)],
        out_specs=row(d),
        out_shape=jax.ShapeDtypeStruct((b, seq, d), F32),
        compiler_params=_params("parallel", "parallel"),
        name="out_mlp",
    )(x, att, hy, ga, gh, w_out, g2, w1, w2, gfin)


def _rope_tables(seq):
    rows = seq // GRID_W
    row = np.repeat(np.arange(rows, dtype=np.float64), GRID_W)
    col = np.tile(np.arange(GRID_W, dtype=np.float64), rows)
    half = HEAD_DIM // 2
    inv_freq = ROPE_THETA ** (-np.arange(0, half, 2, dtype=np.float64) / half)
    ang_r = row[:, None] * inv_freq[None, :]
    ang_c = col[:, None] * inv_freq[None, :]
    cos = np.concatenate([np.cos(ang_r)] * 2 + [np.cos(ang_c)] * 2, axis=1)
    sin = np.concatenate([-np.sin(ang_r), np.sin(ang_r), -np.sin(ang_c), np.sin(ang_c)], axis=1)
    reps = (1, LANES // HEAD_DIM)
    return jnp.asarray(np.tile(cos, reps), F32), jnp.asarray(np.tile(sin, reps), F32)


def _head_mean_matrix(width):
    head = jnp.arange(width, dtype=jnp.int32) // HEAD_DIM
    return jnp.where(head[:, None] == head[None, :], 1.0 / HEAD_DIM, 0.0).astype(BF16)


def kernel(x, norm1_g, w_in, q_norm_g, k_norm_g, hy_conv_w, hy_conv_b, filt_w1, filt_b1, filt_w2, filt_b2, filt_w3, filt_b3, filt_w4, filt_freq, filt_deltas, hy_skip_d, attn_out_g, hy_out_g, w_out, norm2_g, w_mlp_in, w_mlp_out, final_g):
    seq = x.shape[1]
    cos_t, sin_t = _rope_tables(seq)
    bdq = _head_mean_matrix(Q_WIDTH)
    bdk = _head_mean_matrix(KV_WIDTH)
    h = x
    for i in range(norm1_g.shape[0]):
        w_qkv = w_in[i][:, :Q_WIDTH + 2 * KV_WIDTH].astype(BF16)
        w_u = w_in[i][:, Q_WIDTH + 2 * KV_WIDTH:].astype(BF16)
        gain_max = jnp.max(jnp.abs(jnp.stack([q_norm_g[i], k_norm_g[i]])), axis=1)
        bound = (1.01 * HEAD_DIM ** 0.5 * LOG2E) * gain_max[0] * gain_max[1]
        q, k, v, hv, x1, x2 = _in_proj(
            h, norm1_g[i][None], w_qkv, w_u, jnp.tile(q_norm_g[i], N_Q_HEADS)[None],
            jnp.tile(k_norm_g[i], N_KV_HEADS)[None], cos_t, sin_t, bdq, bdk, hy_conv_w[i], hy_conv_b[i][None],
            bound.reshape(1, 1))
        att = lax.cond(
            bound <= MAX_FIXED_SHIFT,
            functools.partial(_attention, _attn_bounded_kernel, KV_CHUNK_BOUNDED, "attention_bounded"),
            functools.partial(_attention, _attn_online_kernel, KV_CHUNK, "attention_online"),
            q, k, v)
        hy = _hyena(hv, x1, x2, filt_w1[i], filt_b1[i], filt_w2[i], filt_b2[i], filt_w3[i], filt_b3[i],
                    filt_w4[i], filt_freq[i], filt_deltas[i], hy_skip_d[i])
        last = i == norm1_g.shape[0] - 1
        assert last, "single-layer trunk"
        h = _out_mlp(h, att, hy, attn_out_g[i][None], hy_out_g[i][None], w_out[i].astype(BF16), norm2_g[i][None],
                     w_mlp_in[i].astype(BF16), w_mlp_out[i].astype(BF16), final_g[None])
    return h
```

```python
import functools
import math

import jax
import jax.numpy as jnp
import numpy as np
from jax import lax
from jax.experimental import pallas as pl
from jax.experimental.pallas import tpu as pltpu

F32 = jnp.float32
BF16 = jnp.bfloat16

HEAD_DIM = 64
N_Q_HEADS = 8
N_KV_HEADS = 2
Q_WIDTH = N_Q_HEADS * HEAD_DIM
KV_WIDTH = N_KV_HEADS * HEAD_DIM
HY_WIDTH = 512
GRID_W = 64
ROPE_THETA = 10000.0
FILTER_EMB = 33
FILTER_PAD = 64
EPS = 1e-6
LOG2E = math.log2(math.e)
MAX_FIXED_SHIFT = 50.0

LANES = 128
SUBLANES = 8
TX_ROWS = SUBLANES * LANES
SWAP = 16
HALO = 16
VMEM_LIMIT = 56 * 1024 * 1024

ROW_TILE = TX_ROWS
MLP_ROWS = 512
Q_TILE = 128
KV_CHUNK = 512
KV_CHUNK_BOUNDED = 8192
FILT_TILE = 512


def _dot(a, b):
    return jnp.dot(a, b, preferred_element_type=F32)


def _dot3(a, b):
    ah = a.astype(BF16)
    al = (a - ah.astype(F32)).astype(BF16)
    bh = b.astype(BF16)
    bl = (b - bh.astype(F32)).astype(BF16)
    return _dot(ah, bh) + (_dot(ah, bl) + _dot(al, bh))


def _rms(v, g):
    return v * lax.rsqrt(jnp.mean(v * v, axis=-1, keepdims=True) + EPS) * g


_PI_A = 3.140625
_PI_B = 9.67502593994140625e-4
_PI_C = 1.509957990978376432e-7
_SIN_SMALL = 8192.0
_SIN_TAYLOR = (-1.0 / 6.0, 1.0 / 120.0, -1.0 / 5040.0, 1.0 / 362880.0, -1.0 / 39916800.0, 1.0 / 6227020800.0)


def _sin_small(x):
    k = jnp.floor(x * (1.0 / math.pi) + 0.5)
    r = ((x - k * _PI_A) - k * _PI_B) - k * _PI_C
    r2 = r * r
    poly = _SIN_TAYLOR[-1]
    for coef in _SIN_TAYLOR[-2::-1]:
        poly = poly * r2 + coef
    s = r + r * r2 * poly
    return jnp.where((k.astype(jnp.int32) & 1) == 1, -s, s)


def _sin(x):
    return lax.cond(jnp.max(jnp.abs(x)) <= _SIN_SMALL, _sin_small, jnp.sin, x)


def _params(*sem):
    return pltpu.CompilerParams(dimension_semantics=sem, vmem_limit_bytes=VMEM_LIMIT)


def _in_proj_kernel(x_ref, xp_ref, xn_ref, g1_ref, wqkv_ref, wu_ref, qg_ref, kg_ref, cos_ref, sin_ref,
                    bdq_ref, bdk_ref, cw_ref, cb_ref,
                    q_ref, k_ref, v_ref, hv_ref, x1_ref, x2_ref, bound_ref, a_scr, *, tm, n_tiles):
    i = pl.program_id(1)
    g1 = g1_ref[...]
    a_scr[HALO:HALO + tm, :] = _rms(x_ref[...], g1).astype(BF16)
    prev = jnp.where(i > 0, _rms(xp_ref[...], g1), 0.0)
    nxt = jnp.where(i < n_tiles - 1, _rms(xn_ref[...], g1), 0.0)
    a_scr[0:HALO, :] = prev.astype(BF16)
    a_scr[HALO + tm:, :] = nxt.astype(BF16)

    qkv = _dot(a_scr[HALO:HALO + tm, :], wqkv_ref[...])
    cos = cos_ref[...]
    sin = sin_ref[...]
    lane = lax.broadcasted_iota(jnp.int32, (tm, LANES), 1)
    first16 = (lane % 32) < 16
    low_half = lane < HEAD_DIM

    def norm_rope(blk, ms, gain):
        y = blk * lax.rsqrt(ms + EPS) * gain
        partner = jnp.where(first16, pltpu.roll(y, LANES - 16, 1), pltpu.roll(y, 16, 1))
        return y * cos + partner * sin

    q = qkv[:, :Q_WIDTH]
    q_ms = _dot((q * q).astype(BF16), bdq_ref[...])
    qg = qg_ref[...]
    bound = (1.01 * HEAD_DIM ** 0.5 * LOG2E) * jnp.max(jnp.abs(qg)) * jnp.max(jnp.abs(kg_ref[...]))
    bound_ref[0, 0] = bound
    neg_bound = -bound
    extra = lane == HEAD_DIM
    for m in range(Q_WIDTH // LANES):
        sl = slice(m * LANES, (m + 1) * LANES)
        blk = norm_rope(q[:, sl], q_ms[:, sl], qg[:, sl]) * (HEAD_DIM ** -0.5 * LOG2E)
        q_ref[:, 2 * m * LANES:(2 * m + 1) * LANES] = jnp.where(extra, neg_bound, blk).astype(BF16)
        q_ref[:, (2 * m + 1) * LANES:(2 * m + 2) * LANES] = jnp.where(
            extra, neg_bound, pltpu.roll(blk, HEAD_DIM, 1)).astype(BF16)

    k = qkv[:, Q_WIDTH:Q_WIDTH + KV_WIDTH]
    k_ms = _dot((k * k).astype(BF16), bdk_ref[...])
    k = norm_rope(k, k_ms, kg_ref[...])
    v = qkv[:, Q_WIDTH + KV_WIDTH:]
    pad = jnp.where(extra, 1.0, 0.0)
    k_ref[0] = jnp.where(low_half, k, pad).astype(BF16)
    k_ref[1] = jnp.where(low_half, pltpu.roll(k, HEAD_DIM, 1), pad).astype(BF16)
    v_ref[0] = jnp.where(low_half, v, 1.0).astype(BF16)
    v_ref[1] = jnp.where(low_half, pltpu.roll(v, HEAD_DIM, 1), 1.0).astype(BF16)

    first_p = lax.broadcasted_iota(jnp.int32, (SUBLANES, HY_WIDTH), 0) == 0
    last_p = lax.broadcasted_iota(jnp.int32, (SUBLANES, HY_WIDTH), 0) == SUBLANES - 1
    for c, out in enumerate((hv_ref, x1_ref, x2_ref)):
        cs = slice(c * HY_WIDTH, (c + 1) * HY_WIDTH)
        u = _dot(a_scr[...], wu_ref[:, cs])
        w = cw_ref[:, cs]
        ut = jnp.swapaxes(u[HALO:HALO + tm].reshape(SUBLANES, LANES, HY_WIDTH), 0, 1)
        before = jnp.where(first_p, u[HALO - 1:HALO], pltpu.roll(ut[LANES - 1], 1, 0))
        after = jnp.where(last_p, u[HALO + tm:HALO + tm + 1], pltpu.roll(ut[0], SUBLANES - 1, 0))
        prev = jnp.concatenate([before[None], ut[:-1]], axis=0)
        nxt = jnp.concatenate([ut[1:], after[None]], axis=0)
        y = (prev * w[0:1] + ut * w[1:2] + nxt * w[2:3]) + cb_ref[:, cs]
        out[...] = y.reshape(tm, HY_WIDTH).astype(BF16)


def _in_proj(x, g1, w_qkv, w_u, qg, kg, cos_t, sin_t, bdq, bdk, conv_w, conv_b):
    b, seq, d = x.shape
    tm = ROW_TILE
    n_tiles = seq // tm
    hb = tm // HALO
    full = lambda shape: pl.BlockSpec(shape, lambda bi, i: (0,) * len(shape))
    row_out = lambda w: pl.BlockSpec((None, tm, w), lambda bi, i: (bi, i, 0))
    kv_out = pl.BlockSpec((None, N_KV_HEADS, tm, LANES), lambda bi, i: (bi, 0, i, 0))
    tx_out = pl.BlockSpec((None, None, tm, HY_WIDTH), lambda bi, i: (bi, i, 0, 0))
    tx_shape = jax.ShapeDtypeStruct((b, n_tiles, tm, HY_WIDTH), BF16)
    return pl.pallas_call(
        functools.partial(_in_proj_kernel, tm=tm, n_tiles=n_tiles),
        grid=(b, n_tiles),
        in_specs=[
            pl.BlockSpec((None, tm, d), lambda bi, i: (bi, i, 0)),
            pl.BlockSpec((None, HALO, d), lambda bi, i: (bi, jnp.maximum(i * hb - 1, 0), 0)),
            pl.BlockSpec((None, HALO, d), lambda bi, i: (bi, jnp.minimum((i + 1) * hb, seq // HALO - 1), 0)),
            full((1, d)),
            full(w_qkv.shape),
            full(w_u.shape),
            full((1, Q_WIDTH)),
            full((1, KV_WIDTH)),
            pl.BlockSpec((tm, LANES), lambda bi, i: (i, 0)),
            pl.BlockSpec((tm, LANES), lambda bi, i: (i, 0)),
            full(bdq.shape),
            full(bdk.shape),
            full(conv_w.shape),
            full(conv_b.shape),
        ],
        out_specs=[row_out(2 * Q_WIDTH), kv_out, kv_out, tx_out, tx_out, tx_out,
                   pl.BlockSpec(memory_space=pltpu.SMEM)],
        out_shape=[
            jax.ShapeDtypeStruct((b, seq, 2 * Q_WIDTH), BF16),
            jax.ShapeDtypeStruct((b, N_KV_HEADS, seq, LANES), BF16),
            jax.ShapeDtypeStruct((b, N_KV_HEADS, seq, LANES), BF16),
            tx_shape, tx_shape, tx_shape,
            jax.ShapeDtypeStruct((1, 1), F32),
        ],
        scratch_shapes=[pltpu.VMEM((tm + 2 * HALO, d), BF16)],
        compiler_params=_params("parallel", "arbitrary"),
        name="in_proj",
    )(x, x, x, g1, w_qkv, w_u, qg, kg, cos_t, sin_t, bdq, bdk, conv_w, conv_b)


def _stack_heads(q_ref):
    group = N_Q_HEADS // N_KV_HEADS
    return jnp.concatenate([q_ref[:, g * LANES:(g + 1) * LANES] for g in range(group)], axis=0)


def _store_heads(o_ref, o, tq):
    group = N_Q_HEADS // N_KV_HEADS
    pairs = [o[(2 * p) * tq:(2 * p + 1) * tq] + pltpu.roll(o[(2 * p + 1) * tq:(2 * p + 2) * tq], HEAD_DIM, 1)
             for p in range(group // 2)]
    o_ref[...] = jnp.concatenate(pairs, axis=1)


def _store_heads_normalised(o_ref, acc, tq):
    group = N_Q_HEADS // N_KV_HEADS
    low = lax.broadcasted_iota(jnp.int32, (tq, LANES), 1) < HEAD_DIM
    pairs = []
    for p in range(group // 2):
        even, odd = acc[(2 * p) * tq:(2 * p + 1) * tq], acc[(2 * p + 1) * tq:(2 * p + 2) * tq]
        num = jnp.where(low, even, pltpu.roll(odd, HEAD_DIM, 1))
        den = jnp.where(low, pltpu.roll(even, HEAD_DIM, 1), odd)
        pairs.append(num / den)
    o_ref[...] = jnp.concatenate(pairs, axis=1)


def _attn_bounded_kernel(q_ref, k_ref, v_ref, o_ref, *, tq, sc, n_chunks):
    q = _stack_heads(q_ref)
    rows = q.shape[0]

    def body(c, acc):
        off = pl.multiple_of(c * sc, sc)
        s = lax.dot_general(q, k_ref[pl.ds(off, sc), :], (((1,), (1,)), ((), ())), preferred_element_type=F32)
        return acc + _dot(jnp.exp2(s).astype(BF16), v_ref[pl.ds(off, sc), :])

    _store_heads_normalised(o_ref, lax.fori_loop(0, n_chunks, body, jnp.zeros((rows, LANES), F32)), tq)


def _attn_online_kernel(q_ref, k_ref, v_ref, o_ref, *, tq, sc, n_chunks):
    q = _stack_heads(q_ref)
    rows = q.shape[0]

    def body(c, carry):
        m, l, acc = carry
        off = pl.multiple_of(c * sc, sc)
        s = lax.dot_general(q, k_ref[pl.ds(off, sc), :], (((1,), (1,)), ((), ())), preferred_element_type=F32)
        m_new = jnp.maximum(m, jnp.max(s, axis=1, keepdims=True))
        p = jnp.exp2(s - m_new)
        alpha = jnp.exp2(m - m_new)
        l = alpha * l + jnp.sum(p, axis=1, keepdims=True)
        acc = alpha * acc + _dot(p.astype(BF16), v_ref[pl.ds(off, sc), :])
        return m_new, l, acc

    init = (jnp.full((rows, 1), jnp.finfo(F32).min, F32), jnp.zeros((rows, 1), F32), jnp.zeros((rows, LANES), F32))
    _, l, acc = lax.fori_loop(0, n_chunks, body, init)
    lane = lax.broadcasted_iota(jnp.int32, (rows, LANES), 1)
    _store_heads(o_ref, jnp.where(lane < HEAD_DIM, acc / l, 0.0), tq)


def _attention(body, sc, name, q, k, v):
    b, seq, _ = q.shape
    tq = min(Q_TILE, seq)
    sc = min(sc, seq)
    group = N_Q_HEADS // N_KV_HEADS
    return pl.pallas_call(
        functools.partial(body, tq=tq, sc=sc, n_chunks=seq // sc),
        grid=(b, N_KV_HEADS, seq // tq),
        in_specs=[
            pl.BlockSpec((None, tq, group * LANES), lambda bi, j, i: (bi, i, j)),
            pl.BlockSpec((None, None, seq, LANES), lambda bi, j, i: (bi, j, 0, 0)),
            pl.BlockSpec((None, None, seq, LANES), lambda bi, j, i: (bi, j, 0, 0)),
        ],
        out_specs=pl.BlockSpec((None, tq, group * HEAD_DIM), lambda bi, j, i: (bi, i, j)),
        out_shape=jax.ShapeDtypeStruct((b, seq, Q_WIDTH), F32),
        compiler_params=_params("parallel", "parallel", "arbitrary"),
        name=name,
    )(q, k, v)


def _swap_store(o_ref, slabs, rows):
    t = jnp.swapaxes(jnp.stack(slabs), 0, 1).astype(BF16)
    o_ref[:, 0] = t[:rows]
    o_ref[:, 1] = t[rows:]


def _filter_kernel(z_ref, w1_ref, b1_ref, w2_ref, b2_ref, w3_ref, b3_ref, w4_ref, fr_ref, dl_ref,
                   taps_ref, s_ref, *, nh):
    i = pl.program_id(0)
    z = z_ref[...]
    fr = fr_ref[...]
    h = _sin(fr * (_dot3(z, w1_ref[...]) + b1_ref[...]))
    h = _sin(fr * (_dot3(h, w2_ref[...]) + b2_ref[...]))
    h = _sin(fr * (_dot3(h, w3_ref[...]) + b3_ref[...]))
    h = _dot(h.astype(BF16), w4_ref[...].astype(BF16)) * jnp.exp(-z[:, 0:1] * jnp.abs(dl_ref[...]))
    first = (lax.broadcasted_iota(jnp.int32, (h.shape[0], HY_WIDTH), 0) == 0) & (i == 0)
    sums = []
    for od in range(4):
        hod = h[:, od * HY_WIDTH:(od + 1) * HY_WIDTH]
        if od % 2 == 1:
            hod = jnp.where(first, 0.0, hod)
        sums.append(jnp.sum(jnp.abs(hod), axis=0, keepdims=True))
        for c in range(HY_WIDTH // LANES):
            for g in range(h.shape[0] // nh):
                taps_ref[od // 2, od % 2, c, g] = hod[g * nh:(g + 1) * nh, c * LANES:(c + 1) * LANES].astype(BF16)
    ssum = jnp.concatenate(sums, axis=1)

    @pl.when(i == 0)
    def _():
        s_ref[...] = ssum

    @pl.when(i > 0)
    def _():
        s_ref[...] += ssum


def _filter_taps(zt, w1, b1, w2, b2, w3, b3, w4, freq, deltas):
    seq = zt.shape[0]
    nh = seq // LANES
    groups = FILT_TILE // nh
    cq = HY_WIDTH // LANES
    full = lambda a: pl.BlockSpec(a.shape, lambda i: (0,) * a.ndim)
    return pl.pallas_call(
        functools.partial(_filter_kernel, nh=nh),
        grid=(seq // FILT_TILE,),
        in_specs=[pl.BlockSpec((FILT_TILE, FILTER_PAD), lambda i: (i, 0))]
        + [full(a) for a in (w1, b1, w2, b2, w3, b3, w4, freq, deltas)],
        out_specs=[pl.BlockSpec((2, 2, cq, groups, nh, LANES), lambda i: (0, 0, 0, i, 0, 0)),
                   pl.BlockSpec((1, 4 * HY_WIDTH), lambda i: (0, 0))],
        out_shape=[jax.ShapeDtypeStruct((2, 2, cq, LANES, nh, LANES), BF16),
                   jax.ShapeDtypeStruct((1, 4 * HY_WIDTH), F32)],
        compiler_params=_params("arbitrary"),
        name="filter_taps",
    )(zt, w1, b1, w2, b2, w3, b3, w4, freq, deltas)


def _fft_outer_filter_kernel(fw_ref, bw_ref, bw0_ref, f_ref, f0_ref, a_ref, *, jb, n1):
    first_block = pl.program_id(1) == 0
    lanes = range(HY_WIDTH // LANES)
    slabs = []
    for j in range(jb):
        bw = [bw0_ref[c, 0] for c in lanes] if j == 0 else [bw_ref[c, jb - j] for c in lanes]
        x = jnp.concatenate([jnp.concatenate([fw_ref[c, j] for c in lanes], axis=1),
                             jnp.concatenate(bw, axis=1)], axis=0)
        mat = jnp.where(first_block, f0_ref[...], f_ref[...]) if j == 0 else f_ref[...]
        slabs.append(_dot(mat, x))
    _swap_store(a_ref, slabs, n1)


def _fft_outer_filter(taps, f1t, f1t0):
    n_ord, _, cq, _, nh, _ = taps.shape
    n1 = 2 * nh
    jb = SWAP
    jbn = LANES // jb
    tap_spec = lambda d, blk: pl.BlockSpec((None, None, cq, jb, nh, LANES), lambda o, j: (o, d, 0, blk(j), 0, 0))
    return pl.pallas_call(
        functools.partial(_fft_outer_filter_kernel, jb=jb, n1=n1),
        grid=(n_ord, jbn),
        in_specs=[tap_spec(0, lambda j: j), tap_spec(1, lambda j: jbn - 1 - j), tap_spec(1, lambda j: (jbn - j) % jbn),
                  pl.BlockSpec(f1t.shape, lambda o, j: (0, 0)), pl.BlockSpec(f1t0.shape, lambda o, j: (0, 0))],
        out_specs=pl.BlockSpec((None, n1, 2, jb, HY_WIDTH), lambda o, j: (o, 0, 0, j, 0)),
        out_shape=jax.ShapeDtypeStruct((n_ord, n1, 2, LANES, HY_WIDTH), BF16),
        compiler_params=_params("parallel", "parallel"),
        name="fft_outer_filter",
    )(taps, taps, taps, f1t, f1t0)


def _time_slabs(z_ref, jb):
    z = z_ref[...].astype(F32)
    two, na, _, width = z.shape
    z = z.reshape(two, na, jb, SUBLANES, width)
    return [z[:, :, j].reshape(two * na * SUBLANES, width) for j in range(jb)]


def _fft_outer_kernel(z_ref, f_ref, a_ref, *, jb, n1):
    _swap_store(a_ref, [_dot(f_ref[...], x.astype(BF16)) for x in _time_slabs(z_ref, jb)], n1)


def _fft_outer(z, f1):
    _, na, _, width = z.shape
    n1 = f1.shape[0] // 2
    jb = SWAP
    return pl.pallas_call(
        functools.partial(_fft_outer_kernel, jb=jb, n1=n1),
        grid=(LANES // jb,),
        in_specs=[pl.BlockSpec((2, na, jb * SUBLANES, width), lambda j: (0, 0, j, 0)),
                  pl.BlockSpec(f1.shape, lambda j: (0, 0))],
        out_specs=pl.BlockSpec((n1, 2, jb, width), lambda j: (0, 0, j, 0)),
        out_shape=jax.ShapeDtypeStruct((n1, 2, LANES, width), BF16),
        compiler_params=_params("parallel"),
        name="fft_outer",
    )(z, f1)


def _fft_inner_kernel(a_ref, af_ref, s_ref, g_ref, b_ref, *, kb, n_fft):
    scale = 1.0 / ((s_ref[0] + s_ref[1]) * n_fft)
    hs = [_dot(g_ref[kk], af_ref[kk].reshape(2 * LANES, HY_WIDTH)) for kk in range(kb)]
    cs = [_dot(g_ref[kk], a_ref[kk].reshape(2 * LANES, HY_WIDTH)) for kk in range(kb)]
    ps = []
    for h, c in zip(hs, cs):
        hr, hi = h[:LANES] * scale, h[LANES:] * scale
        cr, ci = c[:LANES], c[LANES:]
        ps.append(jnp.concatenate([cr * hr - ci * hi, cr * hi + ci * hr], axis=0).astype(BF16))
    _swap_store(b_ref, [lax.dot_general(g_ref[kk], p, (((0,), (0,)), ((), ())), preferred_element_type=F32)
                        for kk, p in enumerate(ps)], LANES)


def _fft_inner(a, af, s, gf, order, n_fft):
    n1 = a.shape[0]
    kb = SWAP
    a_blk = (kb, 2, LANES, HY_WIDTH)
    return pl.pallas_call(
        functools.partial(_fft_inner_kernel, kb=kb, n_fft=n_fft),
        grid=(n1 // kb,),
        in_specs=[pl.BlockSpec(a_blk, lambda k: (k, 0, 0, 0)),
                  pl.BlockSpec((None,) + a_blk, lambda k: (order, k, 0, 0, 0)),
                  pl.BlockSpec((None, 2, 1, HY_WIDTH), lambda k: (order, 0, 0, 0)),
                  pl.BlockSpec((kb, 2 * LANES, 2 * LANES), lambda k: (k, 0, 0))],
        out_specs=pl.BlockSpec((LANES, 2, kb, HY_WIDTH), lambda k: (0, 0, k, 0)),
        out_shape=jax.ShapeDtypeStruct((LANES, 2, n1, HY_WIDTH), BF16),
        compiler_params=_params("parallel"),
        name="fft_inner",
    )(a, af, s, gf)


def _ifft_outer_gate_kernel(b_ref, f_ref, z_ref, g_ref, skip_ref, o_ref, *, jb, nh, token_major):
    skip = skip_ref[...]
    vals = [g * (_dot(f_ref[...], b_ref[j].reshape(4 * nh, b_ref.shape[-1])) + skip * z)
            for j, (z, g) in enumerate(zip(_time_slabs(z_ref, jb), _time_slabs(g_ref, jb)))]
    if token_major:
        t = jnp.swapaxes(jnp.stack(vals), 0, 1).astype(BF16)
        o_ref[0] = t[:nh]
        o_ref[1] = t[nh:]
    else:
        for b in range(2):
            for a in range(nh // SUBLANES):
                rows = slice(b * nh + a * SUBLANES, b * nh + (a + 1) * SUBLANES)
                o_ref[b, a] = jnp.concatenate([v[rows] for v in vals], axis=0).astype(BF16)


def _ifft_outer_gate(bsp, f1inv, z, gate, skip_row, token_major):
    _, na, _, width = z.shape
    nh = na * SUBLANES
    jb = SWAP
    t_spec = pl.BlockSpec((2, na, jb * SUBLANES, width), lambda j: (0, 0, j, 0))
    if token_major:
        out_spec = pl.BlockSpec((2, nh, jb, width), lambda j: (0, 0, j, 0))
        out_shape = jax.ShapeDtypeStruct((2, nh, LANES, width), BF16)
    else:
        out_spec, out_shape = t_spec, jax.ShapeDtypeStruct(z.shape, BF16)
    return pl.pallas_call(
        functools.partial(_ifft_outer_gate_kernel, jb=jb, nh=nh, token_major=token_major),
        grid=(LANES // jb,),
        in_specs=[pl.BlockSpec((jb,) + bsp.shape[1:], lambda j: (j, 0, 0, 0)),
                  pl.BlockSpec(f1inv.shape, lambda j: (0, 0)), t_spec, t_spec,
                  pl.BlockSpec((1, width), lambda j: (0, 0))],
        out_specs=out_spec,
        out_shape=out_shape,
        compiler_params=_params("parallel"),
        name="ifft_outer_gate",
    )(bsp, f1inv, z, gate, skip_row)


def _dft_tables(seq):
    n = 2 * seq
    n1 = n // LANES
    nh = n1 // 2
    k1 = np.arange(n1)
    r = np.arange(nh)

    def trig(rows, cols, period):
        ang = ((rows[:, None] * cols[None, :]) % period) * (2.0 * math.pi / period)
        return np.cos(ang), np.sin(ang)

    fc, fs = trig(k1, r, n1)
    bc, bs = trig(k1, n1 - 1 - r, n1)
    zc, zs = trig(k1, (n1 - r) % n1, n1)
    zc, zs = zc * (r > 0), zs * (r > 0)
    block = lambda a, b, c, d: jnp.asarray(np.block([[a, b], [c, d]]), F32).astype(BF16)
    f1 = block(fc, fs, -fs, fc)
    f1t = block(fc, bc, -fs, -bs)
    f1t0 = block(fc, zc, -fs, -zs)
    f1inv = block(fc.T, -fs.T, fs.T, fc.T)
    n2 = np.arange(LANES)
    ca, sa = (jnp.asarray(t, F32)[:, None, :] for t in trig(k1, n2, n))
    cb, sb = (jnp.asarray(t, F32)[None] for t in trig(n2, n2, LANES))
    c, s = ca * cb - sa * sb, sa * cb + ca * sb
    gf = jnp.concatenate([jnp.concatenate([c, s], 2), jnp.concatenate([-s, c], 2)], 1).astype(BF16)
    return f1, f1t, f1t0, f1inv, gf


def _filter_features(seq):
    nh = seq // LANES
    pos = (np.arange(LANES)[:, None] + LANES * np.arange(nh)[None, :]).reshape(seq).astype(np.float64)
    bands = (FILTER_EMB - 1) // 2
    ang = (2.0 * math.pi * pos / seq)[:, None] * np.linspace(1e-4, bands - 1, bands)[None, :]
    z = np.concatenate([(pos / (seq - 1))[:, None], np.cos(ang), -np.sin(ang)], axis=-1)
    return jnp.asarray(np.pad(z, ((0, 0), (0, FILTER_PAD - FILTER_EMB))), F32)


def _hyena(hv, x1, x2, fw1, fb1, fw2, fb2, fw3, fb3, fw4, ffreq, fdeltas, skip_d):
    b, na, rows, width = hv.shape
    assert b == 2, "the two batch rows ride as the real and imaginary parts of one transform"
    seq = na * rows
    n = 2 * seq
    f1, f1t, f1t0, f1inv, gf = _dft_tables(seq)

    w1p = jnp.pad(fw1, ((0, FILTER_PAD - FILTER_EMB), (0, 0)))
    taps, s = _filter_taps(_filter_features(seq), w1p, fb1[None], fw2, fb2[None], fw3, fb3[None], fw4,
                           ffreq[None], fdeltas[None])
    af = _fft_outer_filter(taps, f1t, f1t0)
    s = s.reshape(2, 2, 1, HY_WIDTH)

    zz = hv
    for order, gate in enumerate((x1, x2)):
        bsp = _fft_inner(_fft_outer(zz, f1), af, s, gf, order, n)
        zz = _ifft_outer_gate(bsp, f1inv, zz, gate, skip_d[order][None], token_major=order == 1)
    return zz.reshape(b, seq, width)


def _out_mlp_kernel(x_ref, att_ref, hy_ref, ga_ref, gh_ref, wo_ref, g2_ref, w1_ref, w2_ref, gf_ref, o_ref):
    half = att_ref.shape[1]
    for r in range(x_ref.shape[0] // MLP_ROWS):
        rows = slice(r * MLP_ROWS, (r + 1) * MLP_ROWS)
        a = _rms(att_ref[rows, :], ga_ref[...]).astype(BF16)
        hyn = _rms(hy_ref[rows, :].astype(F32), gh_ref[...]).astype(BF16)
        h = x_ref[rows, :] + _dot(a, wo_ref[:half, :]) + _dot(hyn, wo_ref[half:, :])
        m = _rms(h, g2_ref[...]).astype(BF16)
        t = jnp.square(jnp.maximum(_dot(m, w1_ref[...]), 0.0)).astype(BF16)
        h = h + _dot(t, w2_ref[...])
        o_ref[rows, :] = _rms(h, gf_ref[...])


def _out_mlp(x, att, hy, ga, gh, w_out, g2, w1, w2, gfin):
    b, seq, d = x.shape
    tm = ROW_TILE
    row = lambda w: pl.BlockSpec((None, tm, w), lambda bi, i: (bi, i, 0))
    full = lambda a: pl.BlockSpec(a.shape, lambda bi, i: (0,) * a.ndim, pipeline_mode=pl.Buffered(1))
    return pl.pallas_call(
        _out_mlp_kernel,
        grid=(b, seq // tm),
        in_specs=[row(d), row(att.shape[-1]), row(hy.shape[-1]),
                  full(ga), full(gh), full(w_out), full(g2), full(w1), full(w2), full(gfin)],
        out_specs=row(d),
        out_shape=jax.ShapeDtypeStruct((b, seq, d), F32),
        compiler_params=_params("parallel", "parallel"),
        name="out_mlp",
    )(x, att, hy, ga, gh, w_out, g2, w1, w2, gfin)


def _rope_tables(seq):
    rows = seq // GRID_W
    row = np.repeat(np.arange(rows, dtype=np.float64), GRID_W)
    col = np.tile(np.arange(GRID_W, dtype=np.float64), rows)
    half = HEAD_DIM // 2
    inv_freq = ROPE_THETA ** (-np.arange(0, half, 2, dtype=np.float64) / half)
    ang_r = row[:, None] * inv_freq[None, :]
    ang_c = col[:, None] * inv_freq[None, :]
    cos = np.concatenate([np.cos(ang_r)] * 2 + [np.cos(ang_c)] * 2, axis=1)
    sin = np.concatenate([-np.sin(ang_r), np.sin(ang_r), -np.sin(ang_c), np.sin(ang_c)], axis=1)
    reps = (1, LANES // HEAD_DIM)
    return jnp.asarray(np.tile(cos, reps), F32), jnp.asarray(np.tile(sin, reps), F32)


def _head_mean_matrix(width):
    head = jnp.arange(width, dtype=jnp.int32) // HEAD_DIM
    return jnp.where(head[:, None] == head[None, :], 1.0 / HEAD_DIM, 0.0).astype(BF16)


def kernel(x, norm1_g, w_in, q_norm_g, k_norm_g, hy_conv_w, hy_conv_b, filt_w1, filt_b1, filt_w2, filt_b2, filt_w3, filt_b3, filt_w4, filt_freq, filt_deltas, hy_skip_d, attn_out_g, hy_out_g, w_out, norm2_g, w_mlp_in, w_mlp_out, final_g):
    seq = x.shape[1]
    cos_t, sin_t = _rope_tables(seq)
    bdq = _head_mean_matrix(Q_WIDTH)
    bdk = _head_mean_matrix(KV_WIDTH)
    h = x
    for i in range(norm1_g.shape[0]):
        w_qkv = w_in[i][:, :Q_WIDTH + 2 * KV_WIDTH].astype(BF16)
        w_u = w_in[i][:, Q_WIDTH + 2 * KV_WIDTH:].astype(BF16)
        q, k, v, hv, x1, x2, bound = _in_proj(
            h, norm1_g[i][None], w_qkv, w_u, jnp.tile(q_norm_g[i], N_Q_HEADS)[None],
            jnp.tile(k_norm_g[i], N_KV_HEADS)[None], cos_t, sin_t, bdq, bdk, hy_conv_w[i], hy_conv_b[i][None])
        att = lax.cond(
            bound[0, 0] <= MAX_FIXED_SHIFT,
            functools.partial(_attention, _attn_bounded_kernel, KV_CHUNK_BOUNDED, "attention_bounded"),
            functools.partial(_attention, _attn_online_kernel, KV_CHUNK, "attention_online"),
            q, k, v)
        hy = _hyena(hv, x1, x2, filt_w1[i], filt_b1[i], filt_w2[i], filt_b2[i], filt_w3[i], filt_b3[i],
                    filt_w4[i], filt_freq[i], filt_deltas[i], hy_skip_d[i])
        last = i == norm1_g.shape[0] - 1
        assert last, "single-layer trunk"
        h = _out_mlp(h, att, hy, attn_out_g[i][None], hy_out_g[i][None], w_out[i].astype(BF16), norm2_g[i][None],
                     w_mlp_in[i].astype(BF16), w_mlp_out[i].astype(BF16), final_g[None])
    return h
```

```python
import functools
import math

import jax
import jax.numpy as jnp
import numpy as np
from jax import lax
from jax.experimental import pallas as pl
from jax.experimental.pallas import tpu as pltpu

F32 = jnp.float32
BF16 = jnp.bfloat16

HEAD_DIM = 64
N_Q_HEADS = 8
N_KV_HEADS = 2
Q_WIDTH = N_Q_HEADS * HEAD_DIM
KV_WIDTH = N_KV_HEADS * HEAD_DIM
HY_WIDTH = 512
GRID_W = 64
ROPE_THETA = 10000.0
FILTER_EMB = 33
FILTER_PAD = 64
EPS = 1e-6
LOG2E = math.log2(math.e)
MAX_FIXED_SHIFT = 50.0

LANES = 128
SUBLANES = 8
TX_ROWS = SUBLANES * LANES
SWAP = 16
HALO = 16
VMEM_LIMIT = 56 * 1024 * 1024

ROW_TILE = TX_ROWS
MLP_ROWS = 512
Q_TILE = 128
KV_CHUNK = 512
KV_CHUNK_BOUNDED = 8192
FILT_TILE = 512


def _dot(a, b):
    return jnp.dot(a, b, preferred_element_type=F32)


def _dot3(a, b):
    ah = a.astype(BF16)
    al = (a - ah.astype(F32)).astype(BF16)
    bh = b.astype(BF16)
    bl = (b - bh.astype(F32)).astype(BF16)
    return _dot(ah, bh) + (_dot(ah, bl) + _dot(al, bh))


def _rms(v, g):
    return v * lax.rsqrt(jnp.mean(v * v, axis=-1, keepdims=True) + EPS) * g


_PI_A = 3.140625
_PI_B = 9.67502593994140625e-4
_PI_C = 1.509957990978376432e-7
_SIN_SMALL = 8192.0
_SIN_TAYLOR = (-1.0 / 6.0, 1.0 / 120.0, -1.0 / 5040.0, 1.0 / 362880.0, -1.0 / 39916800.0, 1.0 / 6227020800.0)


def _sin_small(x):
    k = jnp.floor(x * (1.0 / math.pi) + 0.5)
    r = ((x - k * _PI_A) - k * _PI_B) - k * _PI_C
    r2 = r * r
    poly = _SIN_TAYLOR[-1]
    for coef in _SIN_TAYLOR[-2::-1]:
        poly = poly * r2 + coef
    s = r + r * r2 * poly
    return jnp.where((k.astype(jnp.int32) & 1) == 1, -s, s)


def _sin(x):
    return lax.cond(jnp.max(jnp.abs(x)) <= _SIN_SMALL, _sin_small, jnp.sin, x)


def _params(*sem):
    return pltpu.CompilerParams(dimension_semantics=sem, vmem_limit_bytes=VMEM_LIMIT)


def _in_proj_kernel(x_ref, xp_ref, xn_ref, g1_ref, w_ref, qg_ref, kg_ref, cos_ref, sin_ref,
                    bdq_ref, bdk_ref, cw_ref, cb_ref,
                    q_ref, k_ref, v_ref, hv_ref, x1_ref, x2_ref, bound_ref, a_scr, w_scr, *, tm, n_tiles):
    i = pl.program_id(1)

    @pl.when((pl.program_id(0) == 0) & (i == 0))
    def _():
        w_scr[...] = w_ref[...].astype(BF16)

    g1 = g1_ref[...]
    a_scr[HALO:HALO + tm, :] = _rms(x_ref[...], g1).astype(BF16)
    prev = jnp.where(i > 0, _rms(xp_ref[...], g1), 0.0)
    nxt = jnp.where(i < n_tiles - 1, _rms(xn_ref[...], g1), 0.0)
    a_scr[0:HALO, :] = prev.astype(BF16)
    a_scr[HALO + tm:, :] = nxt.astype(BF16)

    n_qkv = Q_WIDTH + 2 * KV_WIDTH
    qkv = _dot(a_scr[HALO:HALO + tm, :], w_scr[:, :n_qkv])
    cos = cos_ref[...]
    sin = sin_ref[...]
    lane = lax.broadcasted_iota(jnp.int32, (tm, LANES), 1)
    first16 = (lane % 32) < 16
    low_half = lane < HEAD_DIM

    def norm_rope(blk, ms, gain):
        y = blk * lax.rsqrt(ms + EPS) * gain
        partner = jnp.where(first16, pltpu.roll(y, LANES - 16, 1), pltpu.roll(y, 16, 1))
        return y * cos + partner * sin

    q = qkv[:, :Q_WIDTH]
    q_ms = _dot((q * q).astype(BF16), bdq_ref[...])
    qg = qg_ref[...]
    bound = (1.01 * HEAD_DIM ** 0.5 * LOG2E) * jnp.max(jnp.abs(qg)) * jnp.max(jnp.abs(kg_ref[...]))
    bound_ref[0, 0] = bound
    neg_bound = -bound
    extra = lane == HEAD_DIM
    for m in range(Q_WIDTH // LANES):
        sl = slice(m * LANES, (m + 1) * LANES)
        blk = norm_rope(q[:, sl], q_ms[:, sl], qg[:, sl]) * (HEAD_DIM ** -0.5 * LOG2E)
        q_ref[:, 2 * m * LANES:(2 * m + 1) * LANES] = jnp.where(extra, neg_bound, blk).astype(BF16)
        q_ref[:, (2 * m + 1) * LANES:(2 * m + 2) * LANES] = jnp.where(
            extra, neg_bound, pltpu.roll(blk, HEAD_DIM, 1)).astype(BF16)

    k = qkv[:, Q_WIDTH:Q_WIDTH + KV_WIDTH]
    k_ms = _dot((k * k).astype(BF16), bdk_ref[...])
    k = norm_rope(k, k_ms, kg_ref[...])
    v = qkv[:, Q_WIDTH + KV_WIDTH:]
    pad = jnp.where(extra, 1.0, 0.0)
    k_ref[0] = jnp.where(low_half, k, pad).astype(BF16)
    k_ref[1] = jnp.where(low_half, pltpu.roll(k, HEAD_DIM, 1), pad).astype(BF16)
    v_ref[0] = jnp.where(low_half, v, 1.0).astype(BF16)
    v_ref[1] = jnp.where(low_half, pltpu.roll(v, HEAD_DIM, 1), 1.0).astype(BF16)

    first_p = lax.broadcasted_iota(jnp.int32, (SUBLANES, HY_WIDTH), 0) == 0
    last_p = lax.broadcasted_iota(jnp.int32, (SUBLANES, HY_WIDTH), 0) == SUBLANES - 1
    for c, out in enumerate((hv_ref, x1_ref, x2_ref)):
        cs = slice(c * HY_WIDTH, (c + 1) * HY_WIDTH)
        u = _dot(a_scr[...], w_scr[:, n_qkv + c * HY_WIDTH:n_qkv + (c + 1) * HY_WIDTH])
        w = cw_ref[:, cs]
        ut = jnp.swapaxes(u[HALO:HALO + tm].reshape(SUBLANES, LANES, HY_WIDTH), 0, 1)
        before = jnp.where(first_p, u[HALO - 1:HALO], pltpu.roll(ut[LANES - 1], 1, 0))
        after = jnp.where(last_p, u[HALO + tm:HALO + tm + 1], pltpu.roll(ut[0], SUBLANES - 1, 0))
        prev = jnp.concatenate([before[None], ut[:-1]], axis=0)
        nxt = jnp.concatenate([ut[1:], after[None]], axis=0)
        y = (prev * w[0:1] + ut * w[1:2] + nxt * w[2:3]) + cb_ref[:, cs]
        out[...] = y.reshape(tm, HY_WIDTH).astype(BF16)


def _in_proj(x, g1, w, qg, kg, cos_t, sin_t, bdq, bdk, conv_w, conv_b):
    b, seq, d = x.shape
    tm = ROW_TILE
    n_tiles = seq // tm
    hb = tm // HALO
    full = lambda shape: pl.BlockSpec(shape, lambda bi, i: (0,) * len(shape))
    row_out = lambda w: pl.BlockSpec((None, tm, w), lambda bi, i: (bi, i, 0))
    kv_out = pl.BlockSpec((None, N_KV_HEADS, tm, LANES), lambda bi, i: (bi, 0, i, 0))
    tx_out = pl.BlockSpec((None, None, tm, HY_WIDTH), lambda bi, i: (bi, i, 0, 0))
    tx_shape = jax.ShapeDtypeStruct((b, n_tiles, tm, HY_WIDTH), BF16)
    return pl.pallas_call(
        functools.partial(_in_proj_kernel, tm=tm, n_tiles=n_tiles),
        grid=(b, n_tiles),
        in_specs=[
            pl.BlockSpec((None, tm, d), lambda bi, i: (bi, i, 0)),
            pl.BlockSpec((None, HALO, d), lambda bi, i: (bi, jnp.maximum(i * hb - 1, 0), 0)),
            pl.BlockSpec((None, HALO, d), lambda bi, i: (bi, jnp.minimum((i + 1) * hb, seq // HALO - 1), 0)),
            full((1, d)),
            pl.BlockSpec(w.shape, lambda bi, i: (0, 0), pipeline_mode=pl.Buffered(1)),
            full((1, Q_WIDTH)),
            full((1, KV_WIDTH)),
            pl.BlockSpec((tm, LANES), lambda bi, i: (i, 0)),
            pl.BlockSpec((tm, LANES), lambda bi, i: (i, 0)),
            full(bdq.shape),
            full(bdk.shape),
            full(conv_w.shape),
            full(conv_b.shape),
        ],
        out_specs=[row_out(2 * Q_WIDTH), kv_out, kv_out, tx_out, tx_out, tx_out,
                   pl.BlockSpec(memory_space=pltpu.SMEM)],
        out_shape=[
            jax.ShapeDtypeStruct((b, seq, 2 * Q_WIDTH), BF16),
            jax.ShapeDtypeStruct((b, N_KV_HEADS, seq, LANES), BF16),
            jax.ShapeDtypeStruct((b, N_KV_HEADS, seq, LANES), BF16),
            tx_shape, tx_shape, tx_shape,
            jax.ShapeDtypeStruct((1, 1), F32),
        ],
        scratch_shapes=[pltpu.VMEM((tm + 2 * HALO, d), BF16), pltpu.VMEM(w.shape, BF16)],
        compiler_params=_params("arbitrary", "arbitrary"),
        name="in_proj",
    )(x, x, x, g1, w, qg, kg, cos_t, sin_t, bdq, bdk, conv_w, conv_b)


def _stack_heads(q_ref):
    group = N_Q_HEADS // N_KV_HEADS
    return jnp.concatenate([q_ref[:, g * LANES:(g + 1) * LANES] for g in range(group)], axis=0)


def _store_heads(o_ref, o, tq):
    group = N_Q_HEADS // N_KV_HEADS
    pairs = [o[(2 * p) * tq:(2 * p + 1) * tq] + pltpu.roll(o[(2 * p + 1) * tq:(2 * p + 2) * tq], HEAD_DIM, 1)
             for p in range(group // 2)]
    o_ref[...] = jnp.concatenate(pairs, axis=1)


def _store_heads_normalised(o_ref, acc, tq):
    group = N_Q_HEADS // N_KV_HEADS
    low = lax.broadcasted_iota(jnp.int32, (tq, LANES), 1) < HEAD_DIM
    pairs = []
    for p in range(group // 2):
        even, odd = acc[(2 * p) * tq:(2 * p + 1) * tq], acc[(2 * p + 1) * tq:(2 * p + 2) * tq]
        num = jnp.where(low, even, pltpu.roll(odd, HEAD_DIM, 1))
        den = jnp.where(low, pltpu.roll(even, HEAD_DIM, 1), odd)
        pairs.append(num / den)
    o_ref[...] = jnp.concatenate(pairs, axis=1)


def _attn_bounded_kernel(q_ref, k_ref, v_ref, o_ref, *, tq, sc, n_chunks):
    q = _stack_heads(q_ref)
    rows = q.shape[0]

    def body(c, acc):
        off = pl.multiple_of(c * sc, sc)
        s = lax.dot_general(q, k_ref[pl.ds(off, sc), :], (((1,), (1,)), ((), ())), preferred_element_type=F32)
        return acc + _dot(jnp.exp2(s).astype(BF16), v_ref[pl.ds(off, sc), :])

    _store_heads_normalised(o_ref, lax.fori_loop(0, n_chunks, body, jnp.zeros((rows, LANES), F32)), tq)


def _attn_online_kernel(q_ref, k_ref, v_ref, o_ref, *, tq, sc, n_chunks):
    q = _stack_heads(q_ref)
    rows = q.shape[0]

    def body(c, carry):
        m, l, acc = carry
        off = pl.multiple_of(c * sc, sc)
        s = lax.dot_general(q, k_ref[pl.ds(off, sc), :], (((1,), (1,)), ((), ())), preferred_element_type=F32)
        m_new = jnp.maximum(m, jnp.max(s, axis=1, keepdims=True))
        p = jnp.exp2(s - m_new)
        alpha = jnp.exp2(m - m_new)
        l = alpha * l + jnp.sum(p, axis=1, keepdims=True)
        acc = alpha * acc + _dot(p.astype(BF16), v_ref[pl.ds(off, sc), :])
        return m_new, l, acc

    init = (jnp.full((rows, 1), jnp.finfo(F32).min, F32), jnp.zeros((rows, 1), F32), jnp.zeros((rows, LANES), F32))
    _, l, acc = lax.fori_loop(0, n_chunks, body, init)
    lane = lax.broadcasted_iota(jnp.int32, (rows, LANES), 1)
    _store_heads(o_ref, jnp.where(lane < HEAD_DIM, acc / l, 0.0), tq)


def _attention(body, sc, name, q, k, v):
    b, seq, _ = q.shape
    tq = min(Q_TILE, seq)
    sc = min(sc, seq)
    group = N_Q_HEADS // N_KV_HEADS
    return pl.pallas_call(
        functools.partial(body, tq=tq, sc=sc, n_chunks=seq // sc),
        grid=(b, N_KV_HEADS, seq // tq),
        in_specs=[
            pl.BlockSpec((None, tq, group * LANES), lambda bi, j, i: (bi, i, j)),
            pl.BlockSpec((None, None, seq, LANES), lambda bi, j, i: (bi, j, 0, 0)),
            pl.BlockSpec((None, None, seq, LANES), lambda bi, j, i: (bi, j, 0, 0)),
        ],
        out_specs=pl.BlockSpec((None, tq, group * HEAD_DIM), lambda bi, j, i: (bi, i, j)),
        out_shape=jax.ShapeDtypeStruct((b, seq, Q_WIDTH), F32),
        compiler_params=_params("parallel", "parallel", "arbitrary"),
        name=name,
    )(q, k, v)


def _swap_store(o_ref, slabs, rows):
    t = jnp.swapaxes(jnp.stack(slabs), 0, 1).astype(BF16)
    o_ref[:, 0] = t[:rows]
    o_ref[:, 1] = t[rows:]


def _filter_kernel(z_ref, w1_ref, b1_ref, w2_ref, b2_ref, w3_ref, b3_ref, w4_ref, fr_ref, dl_ref,
                   taps_ref, s_ref, *, nh):
    i = pl.program_id(0)
    z = z_ref[...]
    fr = fr_ref[...]
    h = _sin(fr * (_dot3(z, w1_ref[...]) + b1_ref[...]))
    h = _sin(fr * (_dot3(h, w2_ref[...]) + b2_ref[...]))
    h = _sin(fr * (_dot3(h, w3_ref[...]) + b3_ref[...]))
    h = _dot(h.astype(BF16), w4_ref[...].astype(BF16)) * jnp.exp(-z[:, 0:1] * jnp.abs(dl_ref[...]))
    first = (lax.broadcasted_iota(jnp.int32, (h.shape[0], HY_WIDTH), 0) == 0) & (i == 0)
    sums = []
    for od in range(4):
        hod = h[:, od * HY_WIDTH:(od + 1) * HY_WIDTH]
        if od % 2 == 1:
            hod = jnp.where(first, 0.0, hod)
        sums.append(jnp.sum(jnp.abs(hod), axis=0, keepdims=True))
        for c in range(HY_WIDTH // LANES):
            for g in range(h.shape[0] // nh):
                taps_ref[od // 2, od % 2, c, g] = hod[g * nh:(g + 1) * nh, c * LANES:(c + 1) * LANES].astype(BF16)
    ssum = jnp.concatenate(sums, axis=1)

    @pl.when(i == 0)
    def _():
        s_ref[...] = ssum

    @pl.when(i > 0)
    def _():
        s_ref[...] += ssum


def _filter_taps(zt, w1, b1, w2, b2, w3, b3, w4, freq, deltas):
    seq = zt.shape[0]
    nh = seq // LANES
    groups = FILT_TILE // nh
    cq = HY_WIDTH // LANES
    full = lambda a: pl.BlockSpec(a.shape, lambda i: (0,) * a.ndim)
    return pl.pallas_call(
        functools.partial(_filter_kernel, nh=nh),
        grid=(seq // FILT_TILE,),
        in_specs=[pl.BlockSpec((FILT_TILE, FILTER_PAD), lambda i: (i, 0))]
        + [full(a) for a in (w1, b1, w2, b2, w3, b3, w4, freq, deltas)],
        out_specs=[pl.BlockSpec((2, 2, cq, groups, nh, LANES), lambda i: (0, 0, 0, i, 0, 0)),
                   pl.BlockSpec((1, 4 * HY_WIDTH), lambda i: (0, 0))],
        out_shape=[jax.ShapeDtypeStruct((2, 2, cq, LANES, nh, LANES), BF16),
                   jax.ShapeDtypeStruct((1, 4 * HY_WIDTH), F32)],
        compiler_params=_params("arbitrary"),
        name="filter_taps",
    )(zt, w1, b1, w2, b2, w3, b3, w4, freq, deltas)


def _fft_outer_filter_kernel(fw_ref, bw_ref, bw0_ref, f_ref, f0_ref, a_ref, *, jb, n1):
    first_block = pl.program_id(1) == 0
    lanes = range(HY_WIDTH // LANES)
    slabs = []
    for j in range(jb):
        bw = [bw0_ref[c, 0] for c in lanes] if j == 0 else [bw_ref[c, jb - j] for c in lanes]
        x = jnp.concatenate([jnp.concatenate([fw_ref[c, j] for c in lanes], axis=1),
                             jnp.concatenate(bw, axis=1)], axis=0)
        mat = jnp.where(first_block, f0_ref[...], f_ref[...]) if j == 0 else f_ref[...]
        slabs.append(_dot(mat, x))
    _swap_store(a_ref, slabs, n1)


def _fft_outer_filter(taps, f1t, f1t0):
    n_ord, _, cq, _, nh, _ = taps.shape
    n1 = 2 * nh
    jb = SWAP
    jbn = LANES // jb
    tap_spec = lambda d, blk: pl.BlockSpec((None, None, cq, jb, nh, LANES), lambda o, j: (o, d, 0, blk(j), 0, 0))
    return pl.pallas_call(
        functools.partial(_fft_outer_filter_kernel, jb=jb, n1=n1),
        grid=(n_ord, jbn),
        in_specs=[tap_spec(0, lambda j: j), tap_spec(1, lambda j: jbn - 1 - j), tap_spec(1, lambda j: (jbn - j) % jbn),
                  pl.BlockSpec(f1t.shape, lambda o, j: (0, 0)), pl.BlockSpec(f1t0.shape, lambda o, j: (0, 0))],
        out_specs=pl.BlockSpec((None, n1, 2, jb, HY_WIDTH), lambda o, j: (o, 0, 0, j, 0)),
        out_shape=jax.ShapeDtypeStruct((n_ord, n1, 2, LANES, HY_WIDTH), BF16),
        compiler_params=_params("parallel", "parallel"),
        name="fft_outer_filter",
    )(taps, taps, taps, f1t, f1t0)


def _time_slabs(z_ref, jb):
    z = z_ref[...].astype(F32)
    two, na, _, width = z.shape
    z = z.reshape(two, na, jb, SUBLANES, width)
    return [z[:, :, j].reshape(two * na * SUBLANES, width) for j in range(jb)]


def _fft_outer_kernel(z_ref, f_ref, a_ref, *, jb, n1):
    _swap_store(a_ref, [_dot(f_ref[...], x.astype(BF16)) for x in _time_slabs(z_ref, jb)], n1)


def _fft_outer(z, f1):
    _, na, _, width = z.shape
    n1 = f1.shape[0] // 2
    jb = SWAP
    return pl.pallas_call(
        functools.partial(_fft_outer_kernel, jb=jb, n1=n1),
        grid=(LANES // jb,),
        in_specs=[pl.BlockSpec((2, na, jb * SUBLANES, width), lambda j: (0, 0, j, 0)),
                  pl.BlockSpec(f1.shape, lambda j: (0, 0))],
        out_specs=pl.BlockSpec((n1, 2, jb, width), lambda j: (0, 0, j, 0)),
        out_shape=jax.ShapeDtypeStruct((n1, 2, LANES, width), BF16),
        compiler_params=_params("parallel"),
        name="fft_outer",
    )(z, f1)


def _fft_inner_kernel(a_ref, af_ref, s_ref, gc_ref, gs_ref, b_ref, *, kb, n_fft):
    scale = 1.0 / ((s_ref[0] + s_ref[1]) * n_fft)
    gs = [jnp.concatenate([jnp.concatenate([gc_ref[kk], gs_ref[kk]], axis=1),
                           jnp.concatenate([-gs_ref[kk], gc_ref[kk]], axis=1)], axis=0) for kk in range(kb)]
    hs = [_dot(gs[kk], af_ref[kk].reshape(2 * LANES, HY_WIDTH)) for kk in range(kb)]
    cs = [_dot(gs[kk], a_ref[kk].reshape(2 * LANES, HY_WIDTH)) for kk in range(kb)]
    ps = []
    for h, c in zip(hs, cs):
        hr, hi = h[:LANES] * scale, h[LANES:] * scale
        cr, ci = c[:LANES], c[LANES:]
        ps.append(jnp.concatenate([cr * hr - ci * hi, cr * hi + ci * hr], axis=0).astype(BF16))
    _swap_store(b_ref, [lax.dot_general(gs[kk], p, (((0,), (0,)), ((), ())), preferred_element_type=F32)
                        for kk, p in enumerate(ps)], LANES)


def _fft_inner(a, af, s, gc, gs, order, n_fft):
    n1 = a.shape[0]
    kb = SWAP
    a_blk = (kb, 2, LANES, HY_WIDTH)
    return pl.pallas_call(
        functools.partial(_fft_inner_kernel, kb=kb, n_fft=n_fft),
        grid=(n1 // kb,),
        in_specs=[pl.BlockSpec(a_blk, lambda k: (k, 0, 0, 0)),
                  pl.BlockSpec((None,) + a_blk, lambda k: (order, k, 0, 0, 0)),
                  pl.BlockSpec((None, 2, 1, HY_WIDTH), lambda k: (order, 0, 0, 0)),
                  pl.BlockSpec((kb, LANES, LANES), lambda k: (k, 0, 0)),
                  pl.BlockSpec((kb, LANES, LANES), lambda k: (k, 0, 0))],
        out_specs=pl.BlockSpec((LANES, 2, kb, HY_WIDTH), lambda k: (0, 0, k, 0)),
        out_shape=jax.ShapeDtypeStruct((LANES, 2, n1, HY_WIDTH), BF16),
        compiler_params=_params("parallel"),
        name="fft_inner",
    )(a, af, s, gc, gs)


def _ifft_outer_gate_kernel(b_ref, f_ref, z_ref, g_ref, skip_ref, o_ref, *, jb, nh, token_major):
    skip = skip_ref[...]
    vals = [g * (_dot(f_ref[...], b_ref[j].reshape(4 * nh, b_ref.shape[-1])) + skip * z)
            for j, (z, g) in enumerate(zip(_time_slabs(z_ref, jb), _time_slabs(g_ref, jb)))]
    if token_major:
        t = jnp.swapaxes(jnp.stack(vals), 0, 1).astype(BF16)
        o_ref[0] = t[:nh]
        o_ref[1] = t[nh:]
    else:
        for b in range(2):
            for a in range(nh // SUBLANES):
                rows = slice(b * nh + a * SUBLANES, b * nh + (a + 1) * SUBLANES)
                o_ref[b, a] = jnp.concatenate([v[rows] for v in vals], axis=0).astype(BF16)


def _ifft_outer_gate(bsp, f1inv, z, gate, skip_row, token_major):
    _, na, _, width = z.shape
    nh = na * SUBLANES
    jb = SWAP
    t_spec = pl.BlockSpec((2, na, jb * SUBLANES, width), lambda j: (0, 0, j, 0))
    if token_major:
        out_spec = pl.BlockSpec((2, nh, jb, width), lambda j: (0, 0, j, 0))
        out_shape = jax.ShapeDtypeStruct((2, nh, LANES, width), BF16)
    else:
        out_spec, out_shape = t_spec, jax.ShapeDtypeStruct(z.shape, BF16)
    return pl.pallas_call(
        functools.partial(_ifft_outer_gate_kernel, jb=jb, nh=nh, token_major=token_major),
        grid=(LANES // jb,),
        in_specs=[pl.BlockSpec((jb,) + bsp.shape[1:], lambda j: (j, 0, 0, 0)),
                  pl.BlockSpec(f1inv.shape, lambda j: (0, 0)), t_spec, t_spec,
                  pl.BlockSpec((1, width), lambda j: (0, 0))],
        out_specs=out_spec,
        out_shape=out_shape,
        compiler_params=_params("parallel"),
        name="ifft_outer_gate",
    )(bsp, f1inv, z, gate, skip_row)


def _dft_tables(seq):
    n = 2 * seq
    n1 = n // LANES
    nh = n1 // 2
    k1 = np.arange(n1)
    r = np.arange(nh)

    def trig(rows, cols, period):
        ang = ((rows[:, None] * cols[None, :]) % period) * (2.0 * math.pi / period)
        return np.cos(ang), np.sin(ang)

    fc, fs = trig(k1, r, n1)
    bc, bs = trig(k1, n1 - 1 - r, n1)
    zc, zs = trig(k1, (n1 - r) % n1, n1)
    zc, zs = zc * (r > 0), zs * (r > 0)
    block = lambda a, b, c, d: jnp.asarray(np.block([[a, b], [c, d]]), F32).astype(BF16)
    f1 = block(fc, fs, -fs, fc)
    f1t = block(fc, bc, -fs, -bs)
    f1t0 = block(fc, zc, -fs, -zs)
    f1inv = block(fc.T, -fs.T, fs.T, fc.T)
    n2 = np.arange(LANES)
    ca, sa = (jnp.asarray(t, F32)[:, None, :] for t in trig(k1, n2, n))
    cb, sb = (jnp.asarray(t, F32)[None] for t in trig(n2, n2, LANES))
    gc, gs = (ca * cb - sa * sb).astype(BF16), (sa * cb + ca * sb).astype(BF16)
    return f1, f1t, f1t0, f1inv, gc, gs


def _filter_features(seq):
    nh = seq // LANES
    pos = (np.arange(LANES)[:, None] + LANES * np.arange(nh)[None, :]).reshape(seq).astype(np.float64)
    bands = (FILTER_EMB - 1) // 2
    ang = (2.0 * math.pi * pos / seq)[:, None] * np.linspace(1e-4, bands - 1, bands)[None, :]
    z = np.concatenate([(pos / (seq - 1))[:, None], np.cos(ang), -np.sin(ang)], axis=-1)
    return jnp.asarray(np.pad(z, ((0, 0), (0, FILTER_PAD - FILTER_EMB))), F32)


def _hyena(hv, x1, x2, fw1, fb1, fw2, fb2, fw3, fb3, fw4, ffreq, fdeltas, skip_d):
    b, na, rows, width = hv.shape
    assert b == 2, "the two batch rows ride as the real and imaginary parts of one transform"
    seq = na * rows
    n = 2 * seq
    f1, f1t, f1t0, f1inv, gc, gs = _dft_tables(seq)

    w1p = jnp.pad(fw1, ((0, FILTER_PAD - FILTER_EMB), (0, 0)))
    taps, s = _filter_taps(_filter_features(seq), w1p, fb1[None], fw2, fb2[None], fw3, fb3[None], fw4,
                           ffreq[None], fdeltas[None])
    af = _fft_outer_filter(taps, f1t, f1t0)
    s = s.reshape(2, 2, 1, HY_WIDTH)

    zz = hv
    for order, gate in enumerate((x1, x2)):
        bsp = _fft_inner(_fft_outer(zz, f1), af, s, gc, gs, order, n)
        zz = _ifft_outer_gate(bsp, f1inv, zz, gate, skip_d[order][None], token_major=order == 1)
    return zz.reshape(b, seq, width)


def _out_mlp_kernel(x_ref, att_ref, hy_ref, ga_ref, gh_ref, wo_ref, g2_ref, w1_ref, w2_ref, gf_ref, o_ref):
    half = att_ref.shape[1]
    for r in range(x_ref.shape[0] // MLP_ROWS):
        rows = slice(r * MLP_ROWS, (r + 1) * MLP_ROWS)
        a = _rms(att_ref[rows, :], ga_ref[...]).astype(BF16)
        hyn = _rms(hy_ref[rows, :].astype(F32), gh_ref[...]).astype(BF16)
        h = x_ref[rows, :] + _dot(a, wo_ref[:half, :]) + _dot(hyn, wo_ref[half:, :])
        m = _rms(h, g2_ref[...]).astype(BF16)
        t = jnp.square(jnp.maximum(_dot(m, w1_ref[...]), 0.0)).astype(BF16)
        h = h + _dot(t, w2_ref[...])
        o_ref[rows, :] = _rms(h, gf_ref[...])


def _out_mlp(x, att, hy, ga, gh, w_out, g2, w1, w2, gfin):
    b, seq, d = x.shape
    tm = ROW_TILE
    row = lambda w: pl.BlockSpec((None, tm, w), lambda bi, i: (bi, i, 0))
    full = lambda a: pl.BlockSpec(a.shape, lambda bi, i: (0,) * a.ndim, pipeline_mode=pl.Buffered(1))
    return pl.pallas_call(
        _out_mlp_kernel,
        grid=(b, seq // tm),
        in_specs=[row(d), row(att.shape[-1]), row(hy.shape[-1]),
                  full(ga), full(gh), full(w_out), full(g2), full(w1), full(w2), full(gfin)],
        out_specs=row(d),
        out_shape=jax.ShapeDtypeStruct((b, seq, d), F32),
        compiler_params=_params("parallel", "parallel"),
        name="out_mlp",
    )(x, att, hy, ga, gh, w_out, g2, w1, w2, gfin)


def _rope_tables(seq):
    rows = seq // GRID_W
    row = np.repeat(np.arange(rows, dtype=np.float64), GRID_W)
    col = np.tile(np.arange(GRID_W, dtype=np.float64), rows)
    half = HEAD_DIM // 2
    inv_freq = ROPE_THETA ** (-np.arange(0, half, 2, dtype=np.float64) / half)
    ang_r = row[:, None] * inv_freq[None, :]
    ang_c = col[:, None] * inv_freq[None, :]
    cos = np.concatenate([np.cos(ang_r)] * 2 + [np.cos(ang_c)] * 2, axis=1)
    sin = np.concatenate([-np.sin(ang_r), np.sin(ang_r), -np.sin(ang_c), np.sin(ang_c)], axis=1)
    reps = (1, LANES // HEAD_DIM)
    return jnp.asarray(np.tile(cos, reps), F32), jnp.asarray(np.tile(sin, reps), F32)


def _head_mean_matrix(width):
    head = jnp.arange(width, dtype=jnp.int32) // HEAD_DIM
    return jnp.where(head[:, None] == head[None, :], 1.0 / HEAD_DIM, 0.0).astype(BF16)


def kernel(x, norm1_g, w_in, q_norm_g, k_norm_g, hy_conv_w, hy_conv_b, filt_w1, filt_b1, filt_w2, filt_b2, filt_w3, filt_b3, filt_w4, filt_freq, filt_deltas, hy_skip_d, attn_out_g, hy_out_g, w_out, norm2_g, w_mlp_in, w_mlp_out, final_g):
    seq = x.shape[1]
    cos_t, sin_t = _rope_tables(seq)
    bdq = _head_mean_matrix(Q_WIDTH)
    bdk = _head_mean_matrix(KV_WIDTH)
    h = x
    for i in range(norm1_g.shape[0]):
        q, k, v, hv, x1, x2, bound = _in_proj(
            h, norm1_g[i][None], w_in[i], jnp.tile(q_norm_g[i], N_Q_HEADS)[None],
            jnp.tile(k_norm_g[i], N_KV_HEADS)[None], cos_t, sin_t, bdq, bdk, hy_conv_w[i], hy_conv_b[i][None])
        att = lax.cond(
            bound[0, 0] <= MAX_FIXED_SHIFT,
            functools.partial(_attention, _attn_bounded_kernel, KV_CHUNK_BOUNDED, "attention_bounded"),
            functools.partial(_attention, _attn_online_kernel, KV_CHUNK, "attention_online"),
            q, k, v)
        hy = _hyena(hv, x1, x2, filt_w1[i], filt_b1[i], filt_w2[i], filt_b2[i], filt_w3[i], filt_b3[i],
                    filt_w4[i], filt_freq[i], filt_deltas[i], hy_skip_d[i])
        last = i == norm1_g.shape[0] - 1
        assert last, "single-layer trunk"
        h = _out_mlp(h, att, hy, attn_out_g[i][None], hy_out_g[i][None], w_out[i].astype(BF16), norm2_g[i][None],
                     w_mlp_in[i].astype(BF16), w_mlp_out[i].astype(BF16), final_g[None])
    return h
```

```python
import functools
import math

import jax
import jax.numpy as jnp
import numpy as np
from jax import lax
from jax.experimental import pallas as pl
from jax.experimental.pallas import tpu as pltpu

F32 = jnp.float32
BF16 = jnp.bfloat16

HEAD_DIM = 64
N_Q_HEADS = 8
N_KV_HEADS = 2
Q_WIDTH = N_Q_HEADS * HEAD_DIM
KV_WIDTH = N_KV_HEADS * HEAD_DIM
HY_WIDTH = 512
GRID_W = 64
ROPE_THETA = 10000.0
FILTER_EMB = 33
FILTER_PAD = 64
EPS = 1e-6
LOG2E = math.log2(math.e)
MAX_FIXED_SHIFT = 50.0

LANES = 128
SUBLANES = 8
TX_ROWS = SUBLANES * LANES
SWAP = 16
HALO = 16
VMEM_LIMIT = 56 * 1024 * 1024

ROW_TILE = TX_ROWS
MLP_ROWS = 512
Q_TILE = 128
KV_CHUNK = 512
KV_CHUNK_BOUNDED = 8192
FILT_TILE = 512


def _dot(a, b):
    return jnp.dot(a, b, preferred_element_type=F32)


def _dot_f32(a, b):
    return jnp.dot(a, b, precision=lax.Precision.HIGHEST, preferred_element_type=F32)


def _rms(v, g):
    return v * lax.rsqrt(jnp.mean(v * v, axis=-1, keepdims=True) + EPS) * g


_PI_A = 3.140625
_PI_B = 9.67502593994140625e-4
_PI_C = 1.509957990978376432e-7
_SIN_SMALL = 8192.0
_SIN_TAYLOR = (-1.0 / 6.0, 1.0 / 120.0, -1.0 / 5040.0, 1.0 / 362880.0, -1.0 / 39916800.0, 1.0 / 6227020800.0)


def _sin_small(x):
    k = jnp.floor(x * (1.0 / math.pi) + 0.5)
    r = ((x - k * _PI_A) - k * _PI_B) - k * _PI_C
    r2 = r * r
    poly = _SIN_TAYLOR[-1]
    for coef in _SIN_TAYLOR[-2::-1]:
        poly = poly * r2 + coef
    s = r + r * r2 * poly
    return jnp.where((k.astype(jnp.int32) & 1) == 1, -s, s)


def _params(*sem):
    return pltpu.CompilerParams(dimension_semantics=sem, vmem_limit_bytes=VMEM_LIMIT)


def _in_proj_kernel(x_ref, xp_ref, xn_ref, g1_ref, w_ref, qg_ref, kg_ref, cos_ref, sin_ref,
                    bdq_ref, bdk_ref, cw_ref, cb_ref,
                    q_ref, k_ref, v_ref, hv_ref, x1_ref, x2_ref, bound_ref, a_scr, w_scr, *, tm, n_tiles):
    i = pl.program_id(1)

    @pl.when((pl.program_id(0) == 0) & (i == 0))
    def _():
        w_scr[...] = w_ref[...].astype(BF16)

    g1 = g1_ref[...]
    a_scr[HALO:HALO + tm, :] = _rms(x_ref[...], g1).astype(BF16)
    prev = jnp.where(i > 0, _rms(xp_ref[...], g1), 0.0)
    nxt = jnp.where(i < n_tiles - 1, _rms(xn_ref[...], g1), 0.0)
    a_scr[0:HALO, :] = prev.astype(BF16)
    a_scr[HALO + tm:, :] = nxt.astype(BF16)

    n_qkv = Q_WIDTH + 2 * KV_WIDTH
    qkv = _dot(a_scr[HALO:HALO + tm, :], w_scr[:, :n_qkv])
    cos = cos_ref[...]
    sin = sin_ref[...]
    lane = lax.broadcasted_iota(jnp.int32, (tm, LANES), 1)
    first16 = (lane % 32) < 16
    low_half = lane < HEAD_DIM

    def norm_rope(blk, ms, gain):
        y = blk * lax.rsqrt(ms + EPS) * gain
        partner = jnp.where(first16, pltpu.roll(y, LANES - 16, 1), pltpu.roll(y, 16, 1))
        return y * cos + partner * sin

    q = qkv[:, :Q_WIDTH]
    q_ms = _dot((q * q).astype(BF16), bdq_ref[...])
    qg = qg_ref[...]
    bound = (1.01 * HEAD_DIM ** 0.5 * LOG2E) * jnp.max(jnp.abs(qg)) * jnp.max(jnp.abs(kg_ref[...]))
    bound_ref[0, 0] = bound
    neg_bound = -bound
    extra = lane == HEAD_DIM
    for m in range(Q_WIDTH // LANES):
        sl = slice(m * LANES, (m + 1) * LANES)
        blk = norm_rope(q[:, sl], q_ms[:, sl], qg[:, sl]) * (HEAD_DIM ** -0.5 * LOG2E)
        q_ref[:, 2 * m * LANES:(2 * m + 1) * LANES] = jnp.where(extra, neg_bound, blk).astype(BF16)
        q_ref[:, (2 * m + 1) * LANES:(2 * m + 2) * LANES] = jnp.where(
            extra, neg_bound, pltpu.roll(blk, HEAD_DIM, 1)).astype(BF16)

    k = qkv[:, Q_WIDTH:Q_WIDTH + KV_WIDTH]
    k_ms = _dot((k * k).astype(BF16), bdk_ref[...])
    k = norm_rope(k, k_ms, kg_ref[...])
    v = qkv[:, Q_WIDTH + KV_WIDTH:]
    pad = jnp.where(extra, 1.0, 0.0)
    k_ref[0] = jnp.where(low_half, k, pad).astype(BF16)
    k_ref[1] = jnp.where(low_half, pltpu.roll(k, HEAD_DIM, 1), pad).astype(BF16)
    v_ref[0] = jnp.where(low_half, v, 1.0).astype(BF16)
    v_ref[1] = jnp.where(low_half, pltpu.roll(v, HEAD_DIM, 1), 1.0).astype(BF16)

    first_p = lax.broadcasted_iota(jnp.int32, (SUBLANES, HY_WIDTH), 0) == 0
    last_p = lax.broadcasted_iota(jnp.int32, (SUBLANES, HY_WIDTH), 0) == SUBLANES - 1
    for c, out in enumerate((hv_ref, x1_ref, x2_ref)):
        cs = slice(c * HY_WIDTH, (c + 1) * HY_WIDTH)
        u = _dot(a_scr[...], w_scr[:, n_qkv + c * HY_WIDTH:n_qkv + (c + 1) * HY_WIDTH])
        w = cw_ref[:, cs]
        ut = jnp.swapaxes(u[HALO:HALO + tm].reshape(SUBLANES, LANES, HY_WIDTH), 0, 1)
        before = jnp.where(first_p, u[HALO - 1:HALO], pltpu.roll(ut[LANES - 1], 1, 0))
        after = jnp.where(last_p, u[HALO + tm:HALO + tm + 1], pltpu.roll(ut[0], SUBLANES - 1, 0))
        prev = jnp.concatenate([before[None], ut[:-1]], axis=0)
        nxt = jnp.concatenate([ut[1:], after[None]], axis=0)
        y = (prev * w[0:1] + ut * w[1:2] + nxt * w[2:3]) + cb_ref[:, cs]
        out[...] = y.reshape(tm, HY_WIDTH).astype(BF16)


def _in_proj(x, g1, w, qg, kg, cos_t, sin_t, bdq, bdk, conv_w, conv_b):
    b, seq, d = x.shape
    tm = ROW_TILE
    n_tiles = seq // tm
    hb = tm // HALO
    full = lambda shape: pl.BlockSpec(shape, lambda bi, i: (0,) * len(shape))
    row_out = lambda w: pl.BlockSpec((None, tm, w), lambda bi, i: (bi, i, 0))
    kv_out = pl.BlockSpec((None, N_KV_HEADS, tm, LANES), lambda bi, i: (bi, 0, i, 0))
    tx_out = pl.BlockSpec((None, None, tm, HY_WIDTH), lambda bi, i: (bi, i, 0, 0))
    tx_shape = jax.ShapeDtypeStruct((b, n_tiles, tm, HY_WIDTH), BF16)
    return pl.pallas_call(
        functools.partial(_in_proj_kernel, tm=tm, n_tiles=n_tiles),
        grid=(b, n_tiles),
        in_specs=[
            pl.BlockSpec((None, tm, d), lambda bi, i: (bi, i, 0)),
            pl.BlockSpec((None, HALO, d), lambda bi, i: (bi, jnp.maximum(i * hb - 1, 0), 0)),
            pl.BlockSpec((None, HALO, d), lambda bi, i: (bi, jnp.minimum((i + 1) * hb, seq // HALO - 1), 0)),
            full((1, d)),
            pl.BlockSpec(w.shape, lambda bi, i: (0, 0), pipeline_mode=pl.Buffered(1)),
            full((1, Q_WIDTH)),
            full((1, KV_WIDTH)),
            pl.BlockSpec((tm, LANES), lambda bi, i: (i, 0)),
            pl.BlockSpec((tm, LANES), lambda bi, i: (i, 0)),
            full(bdq.shape),
            full(bdk.shape),
            full(conv_w.shape),
            full(conv_b.shape),
        ],
        out_specs=[row_out(2 * Q_WIDTH), kv_out, kv_out, tx_out, tx_out, tx_out,
                   pl.BlockSpec(memory_space=pltpu.SMEM)],
        out_shape=[
            jax.ShapeDtypeStruct((b, seq, 2 * Q_WIDTH), BF16),
            jax.ShapeDtypeStruct((b, N_KV_HEADS, seq, LANES), BF16),
            jax.ShapeDtypeStruct((b, N_KV_HEADS, seq, LANES), BF16),
            tx_shape, tx_shape, tx_shape,
            jax.ShapeDtypeStruct((1, 1), F32),
        ],
        scratch_shapes=[pltpu.VMEM((tm + 2 * HALO, d), BF16), pltpu.VMEM(w.shape, BF16)],
        compiler_params=_params("arbitrary", "arbitrary"),
        name="in_proj",
    )(x, x, x, g1, w, qg, kg, cos_t, sin_t, bdq, bdk, conv_w, conv_b)


def _stack_heads(q_ref):
    group = N_Q_HEADS // N_KV_HEADS
    return jnp.concatenate([q_ref[:, g * LANES:(g + 1) * LANES] for g in range(group)], axis=0)


def _store_heads(o_ref, o, tq):
    group = N_Q_HEADS // N_KV_HEADS
    pairs = [o[(2 * p) * tq:(2 * p + 1) * tq] + pltpu.roll(o[(2 * p + 1) * tq:(2 * p + 2) * tq], HEAD_DIM, 1)
             for p in range(group // 2)]
    o_ref[...] = jnp.concatenate(pairs, axis=1)


def _store_heads_normalised(o_ref, acc, tq):
    group = N_Q_HEADS // N_KV_HEADS
    low = lax.broadcasted_iota(jnp.int32, (tq, LANES), 1) < HEAD_DIM
    pairs = []
    for p in range(group // 2):
        even, odd = acc[(2 * p) * tq:(2 * p + 1) * tq], acc[(2 * p + 1) * tq:(2 * p + 2) * tq]
        num = jnp.where(low, even, pltpu.roll(odd, HEAD_DIM, 1))
        den = jnp.where(low, pltpu.roll(even, HEAD_DIM, 1), odd)
        pairs.append(num / den)
    o_ref[...] = jnp.concatenate(pairs, axis=1)


def _attn_bounded_kernel(q_ref, k_ref, v_ref, o_ref, *, tq, sc, n_chunks):
    q = _stack_heads(q_ref)
    rows = q.shape[0]

    def body(c, acc):
        off = pl.multiple_of(c * sc, sc)
        s = lax.dot_general(q, k_ref[pl.ds(off, sc), :], (((1,), (1,)), ((), ())), preferred_element_type=F32)
        return acc + _dot(jnp.exp2(s).astype(BF16), v_ref[pl.ds(off, sc), :])

    _store_heads_normalised(o_ref, lax.fori_loop(0, n_chunks, body, jnp.zeros((rows, LANES), F32)), tq)


def _attn_online_kernel(q_ref, k_ref, v_ref, o_ref, *, tq, sc, n_chunks):
    q = _stack_heads(q_ref)
    rows = q.shape[0]

    def body(c, carry):
        m, l, acc = carry
        off = pl.multiple_of(c * sc, sc)
        s = lax.dot_general(q, k_ref[pl.ds(off, sc), :], (((1,), (1,)), ((), ())), preferred_element_type=F32)
        m_new = jnp.maximum(m, jnp.max(s, axis=1, keepdims=True))
        p = jnp.exp2(s - m_new)
        alpha = jnp.exp2(m - m_new)
        l = alpha * l + jnp.sum(p, axis=1, keepdims=True)
        acc = alpha * acc + _dot(p.astype(BF16), v_ref[pl.ds(off, sc), :])
        return m_new, l, acc

    init = (jnp.full((rows, 1), jnp.finfo(F32).min, F32), jnp.zeros((rows, 1), F32), jnp.zeros((rows, LANES), F32))
    _, l, acc = lax.fori_loop(0, n_chunks, body, init)
    lane = lax.broadcasted_iota(jnp.int32, (rows, LANES), 1)
    _store_heads(o_ref, jnp.where(lane < HEAD_DIM, acc / l, 0.0), tq)


def _attention(body, sc, name, q, k, v):
    b, seq, _ = q.shape
    tq = min(Q_TILE, seq)
    sc = min(sc, seq)
    group = N_Q_HEADS // N_KV_HEADS
    return pl.pallas_call(
        functools.partial(body, tq=tq, sc=sc, n_chunks=seq // sc),
        grid=(b, N_KV_HEADS, seq // tq),
        in_specs=[
            pl.BlockSpec((None, tq, group * LANES), lambda bi, j, i: (bi, i, j)),
            pl.BlockSpec((None, None, seq, LANES), lambda bi, j, i: (bi, j, 0, 0)),
            pl.BlockSpec((None, None, seq, LANES), lambda bi, j, i: (bi, j, 0, 0)),
        ],
        out_specs=pl.BlockSpec((None, tq, group * HEAD_DIM), lambda bi, j, i: (bi, i, j)),
        out_shape=jax.ShapeDtypeStruct((b, seq, Q_WIDTH), F32),
        compiler_params=_params("parallel", "parallel", "arbitrary"),
        name=name,
    )(q, k, v)


def _swap_store(o_ref, slabs, rows):
    t = jnp.swapaxes(jnp.stack(slabs), 0, 1).astype(BF16)
    o_ref[:, 0] = t[:rows]
    o_ref[:, 1] = t[rows:]


def _filter_hidden(z, w1, b1, w2, b2, w3, b3, fr, sin):
    x1 = fr * (_dot_f32(z, w1) + b1)
    x2 = fr * (_dot_f32(sin(x1), w2) + b2)
    x3 = fr * (_dot_f32(sin(x2), w3) + b3)
    biggest = jnp.maximum(jnp.max(jnp.abs(x1)), jnp.maximum(jnp.max(jnp.abs(x2)), jnp.max(jnp.abs(x3))))
    return sin(x3), biggest


def _filter_kernel(z_ref, w1_ref, b1_ref, w2_ref, b2_ref, w3_ref, b3_ref, w4_ref, fr_ref, dl_ref,
                   taps_ref, s_ref, *, nh):
    i = pl.program_id(0)
    z = z_ref[...]
    layers = (z, w1_ref[...], b1_ref[...], w2_ref[...], b2_ref[...], w3_ref[...], b3_ref[...], fr_ref[...])
    h, biggest = _filter_hidden(*layers, _sin_small)
    h = lax.cond(biggest <= _SIN_SMALL, lambda: h, lambda: _filter_hidden(*layers, jnp.sin)[0])
    h = _dot(h.astype(BF16), w4_ref[...].astype(BF16)) * jnp.exp(-z[:, 0:1] * jnp.abs(dl_ref[...]))
    first = (lax.broadcasted_iota(jnp.int32, (h.shape[0], HY_WIDTH), 0) == 0) & (i == 0)
    sums = []
    for od in range(4):
        hod = h[:, od * HY_WIDTH:(od + 1) * HY_WIDTH]
        if od % 2 == 1:
            hod = jnp.where(first, 0.0, hod)
        sums.append(jnp.sum(jnp.abs(hod), axis=0, keepdims=True))
        for c in range(HY_WIDTH // LANES):
            for g in range(h.shape[0] // nh):
                taps_ref[od // 2, od % 2, c, g] = hod[g * nh:(g + 1) * nh, c * LANES:(c + 1) * LANES].astype(BF16)
    ssum = jnp.concatenate(sums, axis=1)

    @pl.when(i == 0)
    def _():
        s_ref[...] = ssum

    @pl.when(i > 0)
    def _():
        s_ref[...] += ssum


def _filter_taps(zt, w1, b1, w2, b2, w3, b3, w4, freq, deltas):
    seq = zt.shape[0]
    nh = seq // LANES
    groups = FILT_TILE // nh
    cq = HY_WIDTH // LANES
    full = lambda a: pl.BlockSpec(a.shape, lambda i: (0,) * a.ndim)
    return pl.pallas_call(
        functools.partial(_filter_kernel, nh=nh),
        grid=(seq // FILT_TILE,),
        in_specs=[pl.BlockSpec((FILT_TILE, FILTER_PAD), lambda i: (i, 0))]
        + [full(a) for a in (w1, b1, w2, b2, w3, b3, w4, freq, deltas)],
        out_specs=[pl.BlockSpec((2, 2, cq, groups, nh, LANES), lambda i: (0, 0, 0, i, 0, 0)),
                   pl.BlockSpec((1, 4 * HY_WIDTH), lambda i: (0, 0))],
        out_shape=[jax.ShapeDtypeStruct((2, 2, cq, LANES, nh, LANES), BF16),
                   jax.ShapeDtypeStruct((1, 4 * HY_WIDTH), F32)],
        compiler_params=_params("arbitrary"),
        name="filter_taps",
    )(zt, w1, b1, w2, b2, w3, b3, w4, freq, deltas)


def _fft_outer_filter_kernel(fw_ref, bw_ref, bw0_ref, f_ref, f0_ref, a_ref, *, jb, n1):
    first_block = pl.program_id(1) == 0
    lanes = range(HY_WIDTH // LANES)
    slabs = []
    for j in range(jb):
        bw = [bw0_ref[c, 0] for c in lanes] if j == 0 else [bw_ref[c, jb - j] for c in lanes]
        x = jnp.concatenate([jnp.concatenate([fw_ref[c, j] for c in lanes], axis=1),
                             jnp.concatenate(bw, axis=1)], axis=0)
        mat = jnp.where(first_block, f0_ref[...], f_ref[...]) if j == 0 else f_ref[...]
        slabs.append(_dot(mat, x))
    _swap_store(a_ref, slabs, n1)


def _fft_outer_filter(taps, f1t, f1t0):
    n_ord, _, cq, _, nh, _ = taps.shape
    n1 = 2 * nh
    jb = SWAP
    jbn = LANES // jb
    tap_spec = lambda d, blk: pl.BlockSpec((None, None, cq, jb, nh, LANES), lambda o, j: (o, d, 0, blk(j), 0, 0))
    return pl.pallas_call(
        functools.partial(_fft_outer_filter_kernel, jb=jb, n1=n1),
        grid=(n_ord, jbn),
        in_specs=[tap_spec(0, lambda j: j), tap_spec(1, lambda j: jbn - 1 - j), tap_spec(1, lambda j: (jbn - j) % jbn),
                  pl.BlockSpec(f1t.shape, lambda o, j: (0, 0)), pl.BlockSpec(f1t0.shape, lambda o, j: (0, 0))],
        out_specs=pl.BlockSpec((None, n1, 2, jb, HY_WIDTH), lambda o, j: (o, 0, 0, j, 0)),
        out_shape=jax.ShapeDtypeStruct((n_ord, n1, 2, LANES, HY_WIDTH), BF16),
        compiler_params=_params("parallel", "parallel"),
        name="fft_outer_filter",
    )(taps, taps, taps, f1t, f1t0)


def _time_slabs(z_ref, jb):
    z = z_ref[...].astype(F32)
    two, na, _, width = z.shape
    z = z.reshape(two, na, jb, SUBLANES, width)
    return [z[:, :, j].reshape(two * na * SUBLANES, width) for j in range(jb)]


def _fft_outer_kernel(z_ref, f_ref, a_ref, *, jb, n1):
    _swap_store(a_ref, [_dot(f_ref[...], x.astype(BF16)) for x in _time_slabs(z_ref, jb)], n1)


def _fft_outer(z, f1):
    _, na, _, width = z.shape
    n1 = f1.shape[0] // 2
    jb = SWAP
    return pl.pallas_call(
        functools.partial(_fft_outer_kernel, jb=jb, n1=n1),
        grid=(LANES // jb,),
        in_specs=[pl.BlockSpec((2, na, jb * SUBLANES, width), lambda j: (0, 0, j, 0)),
                  pl.BlockSpec(f1.shape, lambda j: (0, 0))],
        out_specs=pl.BlockSpec((n1, 2, jb, width), lambda j: (0, 0, j, 0)),
        out_shape=jax.ShapeDtypeStruct((n1, 2, LANES, width), BF16),
        compiler_params=_params("parallel"),
        name="fft_outer",
    )(z, f1)


def _fft_inner_kernel(a_ref, af_ref, s_ref, gc_ref, gs_ref, b_ref, *, kb, n_fft):
    scale = 1.0 / ((s_ref[0] + s_ref[1]) * n_fft)
    gs = [jnp.concatenate([jnp.concatenate([gc_ref[kk], gs_ref[kk]], axis=1),
                           jnp.concatenate([-gs_ref[kk], gc_ref[kk]], axis=1)], axis=0) for kk in range(kb)]
    hs = [_dot(gs[kk], af_ref[kk].reshape(2 * LANES, HY_WIDTH)) for kk in range(kb)]
    cs = [_dot(gs[kk], a_ref[kk].reshape(2 * LANES, HY_WIDTH)) for kk in range(kb)]
    ps = []
    for h, c in zip(hs, cs):
        hr, hi = h[:LANES] * scale, h[LANES:] * scale
        cr, ci = c[:LANES], c[LANES:]
        ps.append(jnp.concatenate([cr * hr - ci * hi, cr * hi + ci * hr], axis=0).astype(BF16))
    _swap_store(b_ref, [lax.dot_general(gs[kk], p, (((0,), (0,)), ((), ())), preferred_element_type=F32)
                        for kk, p in enumerate(ps)], LANES)


def _fft_inner(a, af, s, gc, gs, order, n_fft):
    n1 = a.shape[0]
    kb = SWAP
    a_blk = (kb, 2, LANES, HY_WIDTH)
    return pl.pallas_call(
        functools.partial(_fft_inner_kernel, kb=kb, n_fft=n_fft),
        grid=(n1 // kb,),
        in_specs=[pl.BlockSpec(a_blk, lambda k: (k, 0, 0, 0)),
                  pl.BlockSpec((None,) + a_blk, lambda k: (order, k, 0, 0, 0)),
                  pl.BlockSpec((None, 2, 1, HY_WIDTH), lambda k: (order, 0, 0, 0)),
                  pl.BlockSpec((kb, LANES, LANES), lambda k: (k, 0, 0)),
                  pl.BlockSpec((kb, LANES, LANES), lambda k: (k, 0, 0))],
        out_specs=pl.BlockSpec((LANES, 2, kb, HY_WIDTH), lambda k: (0, 0, k, 0)),
        out_shape=jax.ShapeDtypeStruct((LANES, 2, n1, HY_WIDTH), BF16),
        compiler_params=_params("parallel"),
        name="fft_inner",
    )(a, af, s, gc, gs)


def _ifft_outer_gate_kernel(b_ref, f_ref, z_ref, g_ref, skip_ref, o_ref, *, jb, nh, token_major):
    skip = skip_ref[...]
    vals = [g * (_dot(f_ref[...], b_ref[j].reshape(4 * nh, b_ref.shape[-1])) + skip * z)
            for j, (z, g) in enumerate(zip(_time_slabs(z_ref, jb), _time_slabs(g_ref, jb)))]
    if token_major:
        t = jnp.swapaxes(jnp.stack(vals), 0, 1).astype(BF16)
        o_ref[0] = t[:nh]
        o_ref[1] = t[nh:]
    else:
        for b in range(2):
            for a in range(nh // SUBLANES):
                rows = slice(b * nh + a * SUBLANES, b * nh + (a + 1) * SUBLANES)
                o_ref[b, a] = jnp.concatenate([v[rows] for v in vals], axis=0).astype(BF16)


def _ifft_outer_gate(bsp, f1inv, z, gate, skip_row, token_major):
    _, na, _, width = z.shape
    nh = na * SUBLANES
    jb = SWAP
    t_spec = pl.BlockSpec((2, na, jb * SUBLANES, width), lambda j: (0, 0, j, 0))
    if token_major:
        out_spec = pl.BlockSpec((2, nh, jb, width), lambda j: (0, 0, j, 0))
        out_shape = jax.ShapeDtypeStruct((2, nh, LANES, width), BF16)
    else:
        out_spec, out_shape = t_spec, jax.ShapeDtypeStruct(z.shape, BF16)
    return pl.pallas_call(
        functools.partial(_ifft_outer_gate_kernel, jb=jb, nh=nh, token_major=token_major),
        grid=(LANES // jb,),
        in_specs=[pl.BlockSpec((jb,) + bsp.shape[1:], lambda j: (j, 0, 0, 0)),
                  pl.BlockSpec(f1inv.shape, lambda j: (0, 0)), t_spec, t_spec,
                  pl.BlockSpec((1, width), lambda j: (0, 0))],
        out_specs=out_spec,
        out_shape=out_shape,
        compiler_params=_params("parallel"),
        name="ifft_outer_gate",
    )(bsp, f1inv, z, gate, skip_row)


def _dft_tables(seq):
    n = 2 * seq
    n1 = n // LANES
    nh = n1 // 2
    k1 = np.arange(n1)
    r = np.arange(nh)

    def trig(rows, cols, period):
        ang = ((rows[:, None] * cols[None, :]) % period) * (2.0 * math.pi / period)
        return np.cos(ang), np.sin(ang)

    fc, fs = trig(k1, r, n1)
    bc, bs = trig(k1, n1 - 1 - r, n1)
    zc, zs = trig(k1, (n1 - r) % n1, n1)
    zc, zs = zc * (r > 0), zs * (r > 0)
    block = lambda a, b, c, d: jnp.asarray(np.block([[a, b], [c, d]]), F32).astype(BF16)
    f1 = block(fc, fs, -fs, fc)
    f1t = block(fc, bc, -fs, -bs)
    f1t0 = block(fc, zc, -fs, -zs)
    f1inv = block(fc.T, -fs.T, fs.T, fc.T)
    n2 = np.arange(LANES)
    ca, sa = (jnp.asarray(t, F32)[:, None, :] for t in trig(k1, n2, n))
    cb, sb = (jnp.asarray(t, F32)[None] for t in trig(n2, n2, LANES))
    gc, gs = (ca * cb - sa * sb).astype(BF16), (sa * cb + ca * sb).astype(BF16)
    return f1, f1t, f1t0, f1inv, gc, gs


def _filter_features(seq):
    nh = seq // LANES
    pos = (np.arange(LANES)[:, None] + LANES * np.arange(nh)[None, :]).reshape(seq).astype(np.float64)
    bands = (FILTER_EMB - 1) // 2
    ang = (2.0 * math.pi * pos / seq)[:, None] * np.linspace(1e-4, bands - 1, bands)[None, :]
    z = np.concatenate([(pos / (seq - 1))[:, None], np.cos(ang), -np.sin(ang)], axis=-1)
    return jnp.asarray(np.pad(z, ((0, 0), (0, FILTER_PAD - FILTER_EMB))), F32)


def _hyena(hv, x1, x2, fw1, fb1, fw2, fb2, fw3, fb3, fw4, ffreq, fdeltas, skip_d):
    b, na, rows, width = hv.shape
    assert b == 2, "the two batch rows ride as the real and imaginary parts of one transform"
    seq = na * rows
    n = 2 * seq
    f1, f1t, f1t0, f1inv, gc, gs = _dft_tables(seq)

    w1p = jnp.pad(fw1, ((0, FILTER_PAD - FILTER_EMB), (0, 0)))
    taps, s = _filter_taps(_filter_features(seq), w1p, fb1[None], fw2, fb2[None], fw3, fb3[None], fw4,
                           ffreq[None], fdeltas[None])
    af = _fft_outer_filter(taps, f1t, f1t0)
    s = s.reshape(2, 2, 1, HY_WIDTH)

    zz = hv
    for order, gate in enumerate((x1, x2)):
        bsp = _fft_inner(_fft_outer(zz, f1), af, s, gc, gs, order, n)
        zz = _ifft_outer_gate(bsp, f1inv, zz, gate, skip_d[order][None], token_major=order == 1)
    return zz.reshape(b, seq, width)


def _out_mlp_kernel(x_ref, att_ref, hy_ref, ga_ref, gh_ref, wo_ref, g2_ref, w1_ref, w2_ref, gf_ref, o_ref):
    half = att_ref.shape[1]
    for r in range(x_ref.shape[0] // MLP_ROWS):
        rows = slice(r * MLP_ROWS, (r + 1) * MLP_ROWS)
        a = _rms(att_ref[rows, :], ga_ref[...]).astype(BF16)
        hyn = _rms(hy_ref[rows, :].astype(F32), gh_ref[...]).astype(BF16)
        h = x_ref[rows, :] + _dot(a, wo_ref[:half, :]) + _dot(hyn, wo_ref[half:, :])
        m = _rms(h, g2_ref[...]).astype(BF16)
        t = jnp.square(jnp.maximum(_dot(m, w1_ref[...]), 0.0)).astype(BF16)
        h = h + _dot(t, w2_ref[...])
        o_ref[rows, :] = _rms(h, gf_ref[...])


def _out_mlp(x, att, hy, ga, gh, w_out, g2, w1, w2, gfin):
    b, seq, d = x.shape
    tm = ROW_TILE
    row = lambda w: pl.BlockSpec((None, tm, w), lambda bi, i: (bi, i, 0))
    full = lambda a: pl.BlockSpec(a.shape, lambda bi, i: (0,) * a.ndim, pipeline_mode=pl.Buffered(1))
    return pl.pallas_call(
        _out_mlp_kernel,
        grid=(b, seq // tm),
        in_specs=[row(d), row(att.shape[-1]), row(hy.shape[-1]),
                  full(ga), full(gh), full(w_out), full(g2), full(w1), full(w2), full(gfin)],
        out_specs=row(d),
        out_shape=jax.ShapeDtypeStruct((b, seq, d), F32),
        compiler_params=_params("parallel", "parallel"),
        name="out_mlp",
    )(x, att, hy, ga, gh, w_out, g2, w1, w2, gfin)


def _rope_tables(seq):
    rows = seq // GRID_W
    row = np.repeat(np.arange(rows, dtype=np.float64), GRID_W)
    col = np.tile(np.arange(GRID_W, dtype=np.float64), rows)
    half = HEAD_DIM // 2
    inv_freq = ROPE_THETA ** (-np.arange(0, half, 2, dtype=np.float64) / half)
    ang_r = row[:, None] * inv_freq[None, :]
    ang_c = col[:, None] * inv_freq[None, :]
    cos = np.concatenate([np.cos(ang_r)] * 2 + [np.cos(ang_c)] * 2, axis=1)
    sin = np.concatenate([-np.sin(ang_r), np.sin(ang_r), -np.sin(ang_c), np.sin(ang_c)], axis=1)
    reps = (1, LANES // HEAD_DIM)
    return jnp.asarray(np.tile(cos, reps), F32), jnp.asarray(np.tile(sin, reps), F32)


def _head_mean_matrix(width):
    head = jnp.arange(width, dtype=jnp.int32) // HEAD_DIM
    return jnp.where(head[:, None] == head[None, :], 1.0 / HEAD_DIM, 0.0).astype(BF16)


def kernel(x, norm1_g, w_in, q_norm_g, k_norm_g, hy_conv_w, hy_conv_b, filt_w1, filt_b1, filt_w2, filt_b2, filt_w3, filt_b3, filt_w4, filt_freq, filt_deltas, hy_skip_d, attn_out_g, hy_out_g, w_out, norm2_g, w_mlp_in, w_mlp_out, final_g):
    seq = x.shape[1]
    cos_t, sin_t = _rope_tables(seq)
    bdq = _head_mean_matrix(Q_WIDTH)
    bdk = _head_mean_matrix(KV_WIDTH)
    h = x
    for i in range(norm1_g.shape[0]):
        q, k, v, hv, x1, x2, bound = _in_proj(
            h, norm1_g[i][None], w_in[i], jnp.tile(q_norm_g[i], N_Q_HEADS)[None],
            jnp.tile(k_norm_g[i], N_KV_HEADS)[None], cos_t, sin_t, bdq, bdk, hy_conv_w[i], hy_conv_b[i][None])
        att = lax.cond(
            bound[0, 0] <= MAX_FIXED_SHIFT,
            functools.partial(_attention, _attn_bounded_kernel, KV_CHUNK_BOUNDED, "attention_bounded"),
            functools.partial(_attention, _attn_online_kernel, KV_CHUNK, "attention_online"),
            q, k, v)
        hy = _hyena(hv, x1, x2, filt_w1[i], filt_b1[i], filt_w2[i], filt_b2[i], filt_w3[i], filt_b3[i],
                    filt_w4[i], filt_freq[i], filt_deltas[i], hy_skip_d[i])
        last = i == norm1_g.shape[0] - 1
        assert last, "single-layer trunk"
        h = _out_mlp(h, att, hy, attn_out_g[i][None], hy_out_g[i][None], w_out[i].astype(BF16), norm2_g[i][None],
                     w_mlp_in[i].astype(BF16), w_mlp_out[i].astype(BF16), final_g[None])
    return h
```

```python
import functools
import math

import jax
import jax.numpy as jnp
import numpy as np
from jax import lax
from jax.experimental import pallas as pl
from jax.experimental.pallas import tpu as pltpu

F32 = jnp.float32
BF16 = jnp.bfloat16

HEAD_DIM = 64
N_Q_HEADS = 8
N_KV_HEADS = 2
Q_WIDTH = N_Q_HEADS * HEAD_DIM
KV_WIDTH = N_KV_HEADS * HEAD_DIM
HY_WIDTH = 512
GRID_W = 64
ROPE_THETA = 10000.0
FILTER_EMB = 33
FILTER_PAD = 64
EPS = 1e-6
LOG2E = math.log2(math.e)
MAX_FIXED_SHIFT = 50.0

LANES = 128
SUBLANES = 8
TX_ROWS = SUBLANES * LANES
SWAP = 16
HALO = 16
VMEM_LIMIT = 56 * 1024 * 1024

ROW_TILE = TX_ROWS
MLP_ROWS = 512
Q_TILE = 256
KV_CHUNK = 512
KV_CHUNK_BOUNDED = 2048
FILT_TILE = 512


def _dot(a, b):
    return jnp.dot(a, b, preferred_element_type=F32)


def _dot_f32(a, b):
    return jnp.dot(a, b, precision=lax.Precision.HIGHEST, preferred_element_type=F32)


def _rms(v, g):
    return v * lax.rsqrt(jnp.mean(v * v, axis=-1, keepdims=True) + EPS) * g


_PI_A = 3.140625
_PI_B = 9.67502593994140625e-4
_PI_C = 1.509957990978376432e-7
_SIN_SMALL = 8192.0
_SIN_TAYLOR = (-1.0 / 6.0, 1.0 / 120.0, -1.0 / 5040.0, 1.0 / 362880.0, -1.0 / 39916800.0, 1.0 / 6227020800.0)


def _sin_small(x):
    k = jnp.floor(x * (1.0 / math.pi) + 0.5)
    r = ((x - k * _PI_A) - k * _PI_B) - k * _PI_C
    r2 = r * r
    poly = _SIN_TAYLOR[-1]
    for coef in _SIN_TAYLOR[-2::-1]:
        poly = poly * r2 + coef
    s = r + r * r2 * poly
    return jnp.where((k.astype(jnp.int32) & 1) == 1, -s, s)


def _params(*sem):
    return pltpu.CompilerParams(dimension_semantics=sem, vmem_limit_bytes=VMEM_LIMIT)


def _in_proj_kernel(x_ref, xp_ref, xn_ref, g1_ref, w_ref, qg_ref, kg_ref, cos_ref, sin_ref,
                    bdq_ref, bdk_ref, cw_ref, cb_ref,
                    q_ref, k_ref, v_ref, hv_ref, x1_ref, x2_ref, bound_ref, a_scr, w_scr, *, tm, n_tiles):
    i = pl.program_id(1)

    @pl.when((pl.program_id(0) == 0) & (i == 0))
    def _():
        w_scr[...] = w_ref[...].astype(BF16)

    g1 = g1_ref[...]
    a_scr[HALO:HALO + tm, :] = _rms(x_ref[...], g1).astype(BF16)
    prev = jnp.where(i > 0, _rms(xp_ref[...], g1), 0.0)
    nxt = jnp.where(i < n_tiles - 1, _rms(xn_ref[...], g1), 0.0)
    a_scr[0:HALO, :] = prev.astype(BF16)
    a_scr[HALO + tm:, :] = nxt.astype(BF16)

    n_qkv = Q_WIDTH + 2 * KV_WIDTH
    qkv = _dot(a_scr[HALO:HALO + tm, :], w_scr[:, :n_qkv])
    cos = cos_ref[...]
    sin = sin_ref[...]
    lane = lax.broadcasted_iota(jnp.int32, (tm, LANES), 1)
    first16 = (lane % 32) < 16
    low_half = lane < HEAD_DIM

    def norm_rope(blk, ms, gain):
        y = blk * lax.rsqrt(ms + EPS) * gain
        partner = jnp.where(first16, pltpu.roll(y, LANES - 16, 1), pltpu.roll(y, 16, 1))
        return y * cos + partner * sin

    q = qkv[:, :Q_WIDTH]
    q_ms = _dot((q * q).astype(BF16), bdq_ref[...])
    qg = qg_ref[...]
    bound = (1.01 * HEAD_DIM ** 0.5 * LOG2E) * jnp.max(jnp.abs(qg)) * jnp.max(jnp.abs(kg_ref[...]))
    bound_ref[0, 0] = bound
    neg_bound = -bound
    extra = lane == HEAD_DIM
    for m in range(Q_WIDTH // LANES):
        sl = slice(m * LANES, (m + 1) * LANES)
        blk = norm_rope(q[:, sl], q_ms[:, sl], qg[:, sl]) * (HEAD_DIM ** -0.5 * LOG2E)
        q_ref[:, 2 * m * LANES:(2 * m + 1) * LANES] = jnp.where(extra, neg_bound, blk).astype(BF16)
        q_ref[:, (2 * m + 1) * LANES:(2 * m + 2) * LANES] = jnp.where(
            extra, neg_bound, pltpu.roll(blk, HEAD_DIM, 1)).astype(BF16)

    k = qkv[:, Q_WIDTH:Q_WIDTH + KV_WIDTH]
    k_ms = _dot((k * k).astype(BF16), bdk_ref[...])
    k = norm_rope(k, k_ms, kg_ref[...])
    v = qkv[:, Q_WIDTH + KV_WIDTH:]
    pad = jnp.where(extra, 1.0, 0.0)
    k_ref[0] = jnp.where(low_half, k, pad).astype(BF16)
    k_ref[1] = jnp.where(low_half, pltpu.roll(k, HEAD_DIM, 1), pad).astype(BF16)
    v_ref[0] = jnp.where(low_half, v, 1.0).astype(BF16)
    v_ref[1] = jnp.where(low_half, pltpu.roll(v, HEAD_DIM, 1), 1.0).astype(BF16)

    first_p = lax.broadcasted_iota(jnp.int32, (SUBLANES, HY_WIDTH), 0) == 0
    last_p = lax.broadcasted_iota(jnp.int32, (SUBLANES, HY_WIDTH), 0) == SUBLANES - 1
    for c, out in enumerate((hv_ref, x1_ref, x2_ref)):
        cs = slice(c * HY_WIDTH, (c + 1) * HY_WIDTH)
        u = _dot(a_scr[...], w_scr[:, n_qkv + c * HY_WIDTH:n_qkv + (c + 1) * HY_WIDTH])
        w = cw_ref[:, cs]
        ut = jnp.swapaxes(u[HALO:HALO + tm].reshape(SUBLANES, LANES, HY_WIDTH), 0, 1)
        before = jnp.where(first_p, u[HALO - 1:HALO], pltpu.roll(ut[LANES - 1], 1, 0))
        after = jnp.where(last_p, u[HALO + tm:HALO + tm + 1], pltpu.roll(ut[0], SUBLANES - 1, 0))
        prev = jnp.concatenate([before[None], ut[:-1]], axis=0)
        nxt = jnp.concatenate([ut[1:], after[None]], axis=0)
        y = (prev * w[0:1] + ut * w[1:2] + nxt * w[2:3]) + cb_ref[:, cs]
        out[...] = y.reshape(tm, HY_WIDTH).astype(BF16)


def _in_proj(x, g1, w, qg, kg, cos_t, sin_t, bdq, bdk, conv_w, conv_b):
    b, seq, d = x.shape
    tm = ROW_TILE
    n_tiles = seq // tm
    hb = tm // HALO
    full = lambda shape: pl.BlockSpec(shape, lambda bi, i: (0,) * len(shape))
    row_out = lambda w: pl.BlockSpec((None, tm, w), lambda bi, i: (bi, i, 0))
    kv_out = pl.BlockSpec((None, N_KV_HEADS, tm, LANES), lambda bi, i: (bi, 0, i, 0))
    tx_out = pl.BlockSpec((None, None, tm, HY_WIDTH), lambda bi, i: (bi, i, 0, 0))
    tx_shape = jax.ShapeDtypeStruct((b, n_tiles, tm, HY_WIDTH), BF16)
    return pl.pallas_call(
        functools.partial(_in_proj_kernel, tm=tm, n_tiles=n_tiles),
        grid=(b, n_tiles),
        in_specs=[
            pl.BlockSpec((None, tm, d), lambda bi, i: (bi, i, 0)),
            pl.BlockSpec((None, HALO, d), lambda bi, i: (bi, jnp.maximum(i * hb - 1, 0), 0)),
            pl.BlockSpec((None, HALO, d), lambda bi, i: (bi, jnp.minimum((i + 1) * hb, seq // HALO - 1), 0)),
            full((1, d)),
            pl.BlockSpec(w.shape, lambda bi, i: (0, 0), pipeline_mode=pl.Buffered(1)),
            full((1, Q_WIDTH)),
            full((1, KV_WIDTH)),
            pl.BlockSpec((tm, LANES), lambda bi, i: (i, 0)),
            pl.BlockSpec((tm, LANES), lambda bi, i: (i, 0)),
            full(bdq.shape),
            full(bdk.shape),
            full(conv_w.shape),
            full(conv_b.shape),
        ],
        out_specs=[row_out(2 * Q_WIDTH), kv_out, kv_out, tx_out, tx_out, tx_out,
                   pl.BlockSpec(memory_space=pltpu.SMEM)],
        out_shape=[
            jax.ShapeDtypeStruct((b, seq, 2 * Q_WIDTH), BF16),
            jax.ShapeDtypeStruct((b, N_KV_HEADS, seq, LANES), BF16),
            jax.ShapeDtypeStruct((b, N_KV_HEADS, seq, LANES), BF16),
            tx_shape, tx_shape, tx_shape,
            jax.ShapeDtypeStruct((1, 1), F32),
        ],
        scratch_shapes=[pltpu.VMEM((tm + 2 * HALO, d), BF16), pltpu.VMEM(w.shape, BF16)],
        compiler_params=_params("arbitrary", "arbitrary"),
        name="in_proj",
    )(x, x, x, g1, w, qg, kg, cos_t, sin_t, bdq, bdk, conv_w, conv_b)


def _stack_heads(q_ref):
    group = N_Q_HEADS // N_KV_HEADS
    return jnp.concatenate([q_ref[:, g * LANES:(g + 1) * LANES] for g in range(group)], axis=0)


def _store_heads(o_ref, o, tq):
    group = N_Q_HEADS // N_KV_HEADS
    pairs = [o[(2 * p) * tq:(2 * p + 1) * tq] + pltpu.roll(o[(2 * p + 1) * tq:(2 * p + 2) * tq], HEAD_DIM, 1)
             for p in range(group // 2)]
    o_ref[...] = jnp.concatenate(pairs, axis=1)


def _store_heads_normalised(o_ref, acc, tq):
    group = N_Q_HEADS // N_KV_HEADS
    low = lax.broadcasted_iota(jnp.int32, (tq, LANES), 1) < HEAD_DIM
    pairs = []
    for p in range(group // 2):
        even, odd = acc[(2 * p) * tq:(2 * p + 1) * tq], acc[(2 * p + 1) * tq:(2 * p + 2) * tq]
        num = jnp.where(low, even, pltpu.roll(odd, HEAD_DIM, 1))
        den = jnp.where(low, pltpu.roll(even, HEAD_DIM, 1), odd)
        pairs.append(num / den)
    o_ref[...] = jnp.concatenate(pairs, axis=1)


def _attn_bounded_kernel(q_ref, k_ref, v_ref, o_ref, *, tq, sc, n_chunks):
    q = _stack_heads(q_ref)
    p = jnp.concatenate(
        [jnp.exp2(lax.dot_general(q, k_ref[c * sc:(c + 1) * sc, :], (((1,), (1,)), ((), ())),
                                  preferred_element_type=F32)).astype(BF16) for c in range(n_chunks)], axis=1)
    _store_heads_normalised(o_ref, _dot(p, v_ref[...]), tq)


def _attn_online_kernel(q_ref, k_ref, v_ref, o_ref, *, tq, sc, n_chunks):
    q = _stack_heads(q_ref)
    rows = q.shape[0]

    def body(c, carry):
        m, l, acc = carry
        off = pl.multiple_of(c * sc, sc)
        s = lax.dot_general(q, k_ref[pl.ds(off, sc), :], (((1,), (1,)), ((), ())), preferred_element_type=F32)
        m_new = jnp.maximum(m, jnp.max(s, axis=1, keepdims=True))
        p = jnp.exp2(s - m_new)
        alpha = jnp.exp2(m - m_new)
        l = alpha * l + jnp.sum(p, axis=1, keepdims=True)
        acc = alpha * acc + _dot(p.astype(BF16), v_ref[pl.ds(off, sc), :])
        return m_new, l, acc

    init = (jnp.full((rows, 1), jnp.finfo(F32).min, F32), jnp.zeros((rows, 1), F32), jnp.zeros((rows, LANES), F32))
    _, l, acc = lax.fori_loop(0, n_chunks, body, init)
    lane = lax.broadcasted_iota(jnp.int32, (rows, LANES), 1)
    _store_heads(o_ref, jnp.where(lane < HEAD_DIM, acc / l, 0.0), tq)


def _attention(body, sc, name, q, k, v):
    b, seq, _ = q.shape
    tq = min(Q_TILE, seq)
    sc = min(sc, seq)
    group = N_Q_HEADS // N_KV_HEADS
    return pl.pallas_call(
        functools.partial(body, tq=tq, sc=sc, n_chunks=seq // sc),
        grid=(b, N_KV_HEADS, seq // tq),
        in_specs=[
            pl.BlockSpec((None, tq, group * LANES), lambda bi, j, i: (bi, i, j)),
            pl.BlockSpec((None, None, seq, LANES), lambda bi, j, i: (bi, j, 0, 0)),
            pl.BlockSpec((None, None, seq, LANES), lambda bi, j, i: (bi, j, 0, 0)),
        ],
        out_specs=pl.BlockSpec((None, tq, group * HEAD_DIM), lambda bi, j, i: (bi, i, j)),
        out_shape=jax.ShapeDtypeStruct((b, seq, Q_WIDTH), F32),
        compiler_params=_params("parallel", "parallel", "arbitrary"),
        name=name,
    )(q, k, v)


def _swap_store(o_ref, slabs, rows):
    t = jnp.swapaxes(jnp.stack(slabs), 0, 1).astype(BF16)
    o_ref[:, 0] = t[:rows]
    o_ref[:, 1] = t[rows:]


def _filter_hidden(z, w1, b1, w2, b2, w3, b3, fr, sin):
    x1 = fr * (_dot_f32(z, w1) + b1)
    x2 = fr * (_dot_f32(sin(x1), w2) + b2)
    x3 = fr * (_dot_f32(sin(x2), w3) + b3)
    biggest = jnp.maximum(jnp.max(jnp.abs(x1)), jnp.maximum(jnp.max(jnp.abs(x2)), jnp.max(jnp.abs(x3))))
    return sin(x3), biggest


def _filter_kernel(z_ref, w1_ref, b1_ref, w2_ref, b2_ref, w3_ref, b3_ref, w4_ref, fr_ref, dl_ref,
                   taps_ref, s_ref, *, nh):
    i = pl.program_id(0)
    z = z_ref[...]
    layers = (z, w1_ref[...], b1_ref[...], w2_ref[...], b2_ref[...], w3_ref[...], b3_ref[...], fr_ref[...])
    h, biggest = _filter_hidden(*layers, _sin_small)
    h = lax.cond(biggest <= _SIN_SMALL, lambda: h, lambda: _filter_hidden(*layers, jnp.sin)[0])
    h = _dot(h.astype(BF16), w4_ref[...].astype(BF16)) * jnp.exp(-z[:, 0:1] * jnp.abs(dl_ref[...]))
    first = (lax.broadcasted_iota(jnp.int32, (h.shape[0], HY_WIDTH), 0) == 0) & (i == 0)
    sums = []
    for od in range(4):
        hod = h[:, od * HY_WIDTH:(od + 1) * HY_WIDTH]
        if od % 2 == 1:
            hod = jnp.where(first, 0.0, hod)
        sums.append(jnp.sum(jnp.abs(hod), axis=0, keepdims=True))
        for c in range(HY_WIDTH // LANES):
            for g in range(h.shape[0] // nh):
                taps_ref[od // 2, od % 2, c, g] = hod[g * nh:(g + 1) * nh, c * LANES:(c + 1) * LANES].astype(BF16)
    ssum = jnp.concatenate(sums, axis=1)

    @pl.when(i == 0)
    def _():
        s_ref[...] = ssum

    @pl.when(i > 0)
    def _():
        s_ref[...] += ssum


def _filter_taps(zt, w1, b1, w2, b2, w3, b3, w4, freq, deltas):
    seq = zt.shape[0]
    nh = seq // LANES
    groups = FILT_TILE // nh
    cq = HY_WIDTH // LANES
    full = lambda a: pl.BlockSpec(a.shape, lambda i: (0,) * a.ndim)
    return pl.pallas_call(
        functools.partial(_filter_kernel, nh=nh),
        grid=(seq // FILT_TILE,),
        in_specs=[pl.BlockSpec((FILT_TILE, FILTER_PAD), lambda i: (i, 0))]
        + [full(a) for a in (w1, b1, w2, b2, w3, b3, w4, freq, deltas)],
        out_specs=[pl.BlockSpec((2, 2, cq, groups, nh, LANES), lambda i: (0, 0, 0, i, 0, 0)),
                   pl.BlockSpec((1, 4 * HY_WIDTH), lambda i: (0, 0))],
        out_shape=[jax.ShapeDtypeStruct((2, 2, cq, LANES, nh, LANES), BF16),
                   jax.ShapeDtypeStruct((1, 4 * HY_WIDTH), F32)],
        compiler_params=_params("arbitrary"),
        name="filter_taps",
    )(zt, w1, b1, w2, b2, w3, b3, w4, freq, deltas)


def _fft_outer_filter_kernel(fw_ref, bw_ref, bw0_ref, f_ref, f0_ref, a_ref, *, jb, n1):
    first_block = pl.program_id(1) == 0
    lanes = range(HY_WIDTH // LANES)
    slabs = []
    for j in range(jb):
        bw = [bw0_ref[c, 0] for c in lanes] if j == 0 else [bw_ref[c, jb - j] for c in lanes]
        x = jnp.concatenate([jnp.concatenate([fw_ref[c, j] for c in lanes], axis=1),
                             jnp.concatenate(bw, axis=1)], axis=0)
        mat = jnp.where(first_block, f0_ref[...], f_ref[...]) if j == 0 else f_ref[...]
        slabs.append(_dot(mat, x))
    _swap_store(a_ref, slabs, n1)


def _fft_outer_filter(taps, f1t, f1t0):
    n_ord, _, cq, _, nh, _ = taps.shape
    n1 = 2 * nh
    jb = SWAP
    jbn = LANES // jb
    tap_spec = lambda d, blk: pl.BlockSpec((None, None, cq, jb, nh, LANES), lambda o, j: (o, d, 0, blk(j), 0, 0))
    return pl.pallas_call(
        functools.partial(_fft_outer_filter_kernel, jb=jb, n1=n1),
        grid=(n_ord, jbn),
        in_specs=[tap_spec(0, lambda j: j), tap_spec(1, lambda j: jbn - 1 - j), tap_spec(1, lambda j: (jbn - j) % jbn),
                  pl.BlockSpec(f1t.shape, lambda o, j: (0, 0)), pl.BlockSpec(f1t0.shape, lambda o, j: (0, 0))],
        out_specs=pl.BlockSpec((None, n1, 2, jb, HY_WIDTH), lambda o, j: (o, 0, 0, j, 0)),
        out_shape=jax.ShapeDtypeStruct((n_ord, n1, 2, LANES, HY_WIDTH), BF16),
        compiler_params=_params("parallel", "parallel"),
        name="fft_outer_filter",
    )(taps, taps, taps, f1t, f1t0)


def _time_slabs(z_ref, jb):
    z = z_ref[...].astype(F32)
    two, na, _, width = z.shape
    z = z.reshape(two, na, jb, SUBLANES, width)
    return [z[:, :, j].reshape(two * na * SUBLANES, width) for j in range(jb)]


def _fft_outer_kernel(z_ref, f_ref, a_ref, *, jb, n1):
    _swap_store(a_ref, [_dot(f_ref[...], x.astype(BF16)) for x in _time_slabs(z_ref, jb)], n1)


def _fft_outer(z, f1):
    _, na, _, width = z.shape
    n1 = f1.shape[0] // 2
    jb = SWAP
    return pl.pallas_call(
        functools.partial(_fft_outer_kernel, jb=jb, n1=n1),
        grid=(LANES // jb,),
        in_specs=[pl.BlockSpec((2, na, jb * SUBLANES, width), lambda j: (0, 0, j, 0)),
                  pl.BlockSpec(f1.shape, lambda j: (0, 0))],
        out_specs=pl.BlockSpec((n1, 2, jb, width), lambda j: (0, 0, j, 0)),
        out_shape=jax.ShapeDtypeStruct((n1, 2, LANES, width), BF16),
        compiler_params=_params("parallel"),
        name="fft_outer",
    )(z, f1)


def _fft_inner_kernel(a_ref, af_ref, s_ref, gc_ref, gs_ref, b_ref, *, kb, n_fft):
    scale = 1.0 / ((s_ref[0] + s_ref[1]) * n_fft)
    gs = [jnp.concatenate([jnp.concatenate([gc_ref[kk], gs_ref[kk]], axis=1),
                           jnp.concatenate([-gs_ref[kk], gc_ref[kk]], axis=1)], axis=0) for kk in range(kb)]
    hs = [_dot(gs[kk], af_ref[kk].reshape(2 * LANES, HY_WIDTH)) for kk in range(kb)]
    cs = [_dot(gs[kk], a_ref[kk].reshape(2 * LANES, HY_WIDTH)) for kk in range(kb)]
    ps = []
    for h, c in zip(hs, cs):
        hr, hi = h[:LANES] * scale, h[LANES:] * scale
        cr, ci = c[:LANES], c[LANES:]
        ps.append(jnp.concatenate([cr * hr - ci * hi, cr * hi + ci * hr], axis=0).astype(BF16))
    _swap_store(b_ref, [lax.dot_general(gs[kk], p, (((0,), (0,)), ((), ())), preferred_element_type=F32)
                        for kk, p in enumerate(ps)], LANES)


def _fft_inner(a, af, s, gc, gs, order, n_fft):
    n1 = a.shape[0]
    kb = SWAP
    a_blk = (kb, 2, LANES, HY_WIDTH)
    return pl.pallas_call(
        functools.partial(_fft_inner_kernel, kb=kb, n_fft=n_fft),
        grid=(n1 // kb,),
        in_specs=[pl.BlockSpec(a_blk, lambda k: (k, 0, 0, 0)),
                  pl.BlockSpec((None,) + a_blk, lambda k: (order, k, 0, 0, 0)),
                  pl.BlockSpec((None, 2, 1, HY_WIDTH), lambda k: (order, 0, 0, 0)),
                  pl.BlockSpec((kb, LANES, LANES), lambda k: (k, 0, 0)),
                  pl.BlockSpec((kb, LANES, LANES), lambda k: (k, 0, 0))],
        out_specs=pl.BlockSpec((LANES, 2, kb, HY_WIDTH), lambda k: (0, 0, k, 0)),
        out_shape=jax.ShapeDtypeStruct((LANES, 2, n1, HY_WIDTH), BF16),
        compiler_params=_params("parallel"),
        name="fft_inner",
    )(a, af, s, gc, gs)


def _ifft_outer_gate_kernel(b_ref, f_ref, z_ref, g_ref, skip_ref, o_ref, *, jb, nh, token_major):
    skip = skip_ref[...]
    vals = [g * (_dot(f_ref[...], b_ref[j].reshape(4 * nh, b_ref.shape[-1])) + skip * z)
            for j, (z, g) in enumerate(zip(_time_slabs(z_ref, jb), _time_slabs(g_ref, jb)))]
    if token_major:
        t = jnp.swapaxes(jnp.stack(vals), 0, 1).astype(BF16)
        o_ref[0] = t[:nh]
        o_ref[1] = t[nh:]
    else:
        for b in range(2):
            for a in range(nh // SUBLANES):
                rows = slice(b * nh + a * SUBLANES, b * nh + (a + 1) * SUBLANES)
                o_ref[b, a] = jnp.concatenate([v[rows] for v in vals], axis=0).astype(BF16)


def _ifft_outer_gate(bsp, f1inv, z, gate, skip_row, token_major):
    _, na, _, width = z.shape
    nh = na * SUBLANES
    jb = SWAP
    t_spec = pl.BlockSpec((2, na, jb * SUBLANES, width), lambda j: (0, 0, j, 0))
    if token_major:
        out_spec = pl.BlockSpec((2, nh, jb, width), lambda j: (0, 0, j, 0))
        out_shape = jax.ShapeDtypeStruct((2, nh, LANES, width), BF16)
    else:
        out_spec, out_shape = t_spec, jax.ShapeDtypeStruct(z.shape, BF16)
    return pl.pallas_call(
        functools.partial(_ifft_outer_gate_kernel, jb=jb, nh=nh, token_major=token_major),
        grid=(LANES // jb,),
        in_specs=[pl.BlockSpec((jb,) + bsp.shape[1:], lambda j: (j, 0, 0, 0)),
                  pl.BlockSpec(f1inv.shape, lambda j: (0, 0)), t_spec, t_spec,
                  pl.BlockSpec((1, width), lambda j: (0, 0))],
        out_specs=out_spec,
        out_shape=out_shape,
        compiler_params=_params("parallel"),
        name="ifft_outer_gate",
    )(bsp, f1inv, z, gate, skip_row)


def _dft_tables(seq):
    n = 2 * seq
    n1 = n // LANES
    nh = n1 // 2
    k1 = np.arange(n1)
    r = np.arange(nh)

    def trig(rows, cols, period):
        ang = ((rows[:, None] * cols[None, :]) % period) * (2.0 * math.pi / period)
        return np.cos(ang), np.sin(ang)

    fc, fs = trig(k1, r, n1)
    bc, bs = trig(k1, n1 - 1 - r, n1)
    zc, zs = trig(k1, (n1 - r) % n1, n1)
    zc, zs = zc * (r > 0), zs * (r > 0)
    block = lambda a, b, c, d: jnp.asarray(np.block([[a, b], [c, d]]), F32).astype(BF16)
    f1 = block(fc, fs, -fs, fc)
    f1t = block(fc, bc, -fs, -bs)
    f1t0 = block(fc, zc, -fs, -zs)
    f1inv = block(fc.T, -fs.T, fs.T, fc.T)
    n2 = np.arange(LANES)
    ca, sa = (jnp.asarray(t, F32)[:, None, :] for t in trig(k1, n2, n))
    cb, sb = (jnp.asarray(t, F32)[None] for t in trig(n2, n2, LANES))
    gc, gs = (ca * cb - sa * sb).astype(BF16), (sa * cb + ca * sb).astype(BF16)
    return f1, f1t, f1t0, f1inv, gc, gs


def _filter_features(seq):
    nh = seq // LANES
    pos = (np.arange(LANES)[:, None] + LANES * np.arange(nh)[None, :]).reshape(seq).astype(np.float64)
    bands = (FILTER_EMB - 1) // 2
    ang = (2.0 * math.pi * pos / seq)[:, None] * np.linspace(1e-4, bands - 1, bands)[None, :]
    z = np.concatenate([(pos / (seq - 1))[:, None], np.cos(ang), -np.sin(ang)], axis=-1)
    return jnp.asarray(np.pad(z, ((0, 0), (0, FILTER_PAD - FILTER_EMB))), F32)


def _hyena(hv, x1, x2, fw1, fb1, fw2, fb2, fw3, fb3, fw4, ffreq, fdeltas, skip_d):
    b, na, rows, width = hv.shape
    assert b == 2, "the two batch rows ride as the real and imaginary parts of one transform"
    seq = na * rows
    n = 2 * seq
    f1, f1t, f1t0, f1inv, gc, gs = _dft_tables(seq)

    w1p = jnp.pad(fw1, ((0, FILTER_PAD - FILTER_EMB), (0, 0)))
    taps, s = _filter_taps(_filter_features(seq), w1p, fb1[None], fw2, fb2[None], fw3, fb3[None], fw4,
                           ffreq[None], fdeltas[None])
    af = _fft_outer_filter(taps, f1t, f1t0)
    s = s.reshape(2, 2, 1, HY_WIDTH)

    zz = hv
    for order, gate in enumerate((x1, x2)):
        bsp = _fft_inner(_fft_outer(zz, f1), af, s, gc, gs, order, n)
        zz = _ifft_outer_gate(bsp, f1inv, zz, gate, skip_d[order][None], token_major=order == 1)
    return zz.reshape(b, seq, width)


def _out_mlp_kernel(x_ref, att_ref, hy_ref, ga_ref, gh_ref, wo_ref, g2_ref, w1_ref, w2_ref, gf_ref, o_ref):
    half = att_ref.shape[1]
    for r in range(x_ref.shape[0] // MLP_ROWS):
        rows = slice(r * MLP_ROWS, (r + 1) * MLP_ROWS)
        a = _rms(att_ref[rows, :], ga_ref[...]).astype(BF16)
        hyn = _rms(hy_ref[rows, :].astype(F32), gh_ref[...]).astype(BF16)
        h = x_ref[rows, :] + _dot(a, wo_ref[:half, :]) + _dot(hyn, wo_ref[half:, :])
        m = _rms(h, g2_ref[...]).astype(BF16)
        t = jnp.square(jnp.maximum(_dot(m, w1_ref[...]), 0.0)).astype(BF16)
        h = h + _dot(t, w2_ref[...])
        o_ref[rows, :] = _rms(h, gf_ref[...])


def _out_mlp(x, att, hy, ga, gh, w_out, g2, w1, w2, gfin):
    b, seq, d = x.shape
    tm = ROW_TILE
    row = lambda w: pl.BlockSpec((None, tm, w), lambda bi, i: (bi, i, 0))
    full = lambda a: pl.BlockSpec(a.shape, lambda bi, i: (0,) * a.ndim, pipeline_mode=pl.Buffered(1))
    return pl.pallas_call(
        _out_mlp_kernel,
        grid=(b, seq // tm),
        in_specs=[row(d), row(att.shape[-1]), row(hy.shape[-1]),
                  full(ga), full(gh), full(w_out), full(g2), full(w1), full(w2), full(gfin)],
        out_specs=row(d),
        out_shape=jax.ShapeDtypeStruct((b, seq, d), F32),
        compiler_params=_params("parallel", "parallel"),
        name="out_mlp",
    )(x, att, hy, ga, gh, w_out, g2, w1, w2, gfin)


def _rope_tables(seq):
    rows = seq // GRID_W
    row = np.repeat(np.arange(rows, dtype=np.float64), GRID_W)
    col = np.tile(np.arange(GRID_W, dtype=np.float64), rows)
    half = HEAD_DIM // 2
    inv_freq = ROPE_THETA ** (-np.arange(0, half, 2, dtype=np.float64) / half)
    ang_r = row[:, None] * inv_freq[None, :]
    ang_c = col[:, None] * inv_freq[None, :]
    cos = np.concatenate([np.cos(ang_r)] * 2 + [np.cos(ang_c)] * 2, axis=1)
    sin = np.concatenate([-np.sin(ang_r), np.sin(ang_r), -np.sin(ang_c), np.sin(ang_c)], axis=1)
    reps = (1, LANES // HEAD_DIM)
    return jnp.asarray(np.tile(cos, reps), F32), jnp.asarray(np.tile(sin, reps), F32)


def _head_mean_matrix(width):
    head = jnp.arange(width, dtype=jnp.int32) // HEAD_DIM
    return jnp.where(head[:, None] == head[None, :], 1.0 / HEAD_DIM, 0.0).astype(BF16)


def kernel(x, norm1_g, w_in, q_norm_g, k_norm_g, hy_conv_w, hy_conv_b, filt_w1, filt_b1, filt_w2, filt_b2, filt_w3, filt_b3, filt_w4, filt_freq, filt_deltas, hy_skip_d, attn_out_g, hy_out_g, w_out, norm2_g, w_mlp_in, w_mlp_out, final_g):
    seq = x.shape[1]
    cos_t, sin_t = _rope_tables(seq)
    bdq = _head_mean_matrix(Q_WIDTH)
    bdk = _head_mean_matrix(KV_WIDTH)
    h = x
    for i in range(norm1_g.shape[0]):
        q, k, v, hv, x1, x2, bound = _in_proj(
            h, norm1_g[i][None], w_in[i], jnp.tile(q_norm_g[i], N_Q_HEADS)[None],
            jnp.tile(k_norm_g[i], N_KV_HEADS)[None], cos_t, sin_t, bdq, bdk, hy_conv_w[i], hy_conv_b[i][None])
        att = lax.cond(
            bound[0, 0] <= MAX_FIXED_SHIFT,
            functools.partial(_attention, _attn_bounded_kernel, KV_CHUNK_BOUNDED, "attention_bounded"),
            functools.partial(_attention, _attn_online_kernel, KV_CHUNK, "attention_online"),
            q, k, v)
        hy = _hyena(hv, x1, x2, filt_w1[i], filt_b1[i], filt_w2[i], filt_b2[i], filt_w3[i], filt_b3[i],
                    filt_w4[i], filt_freq[i], filt_deltas[i], hy_skip_d[i])
        last = i == norm1_g.shape[0] - 1
        assert last, "single-layer trunk"
        h = _out_mlp(h, att, hy, attn_out_g[i][None], hy_out_g[i][None], w_out[i].astype(BF16), norm2_g[i][None],
                     w_mlp_in[i].astype(BF16), w_mlp_out[i].astype(BF16), final_g[None])
    return h
```

```python
import functools
import math

import jax
import jax.numpy as jnp
import numpy as np
from jax import lax
from jax.experimental import pallas as pl
from jax.experimental.pallas import tpu as pltpu

F32 = jnp.float32
BF16 = jnp.bfloat16

HEAD_DIM = 64
N_Q_HEADS = 8
N_KV_HEADS = 2
Q_WIDTH = N_Q_HEADS * HEAD_DIM
KV_WIDTH = N_KV_HEADS * HEAD_DIM
HY_WIDTH = 512
GRID_W = 64
ROPE_THETA = 10000.0
FILTER_EMB = 33
FILTER_PAD = 64
EPS = 1e-6
LOG2E = math.log2(math.e)
MAX_FIXED_SHIFT = 50.0

LANES = 128
SUBLANES = 8
TX_ROWS = SUBLANES * LANES
SWAP = 16
DFT_LANES = 256
HALO = 16
VMEM_LIMIT = 56 * 1024 * 1024

ROW_TILE = TX_ROWS
MLP_ROWS = 512
Q_TILE = 256
KV_CHUNK = 512
KV_CHUNK_BOUNDED = 2048
FILT_TILE = 512


def _dot(a, b):
    return jnp.dot(a, b, preferred_element_type=F32)


def _dot_f32(a, b):
    return jnp.dot(a, b, precision=lax.Precision.HIGHEST, preferred_element_type=F32)


def _rms(v, g):
    return v * lax.rsqrt(jnp.mean(v * v, axis=-1, keepdims=True) + EPS) * g


_PI_A = 3.140625
_PI_B = 9.67502593994140625e-4
_PI_C = 1.509957990978376432e-7
_SIN_SMALL = 8192.0
_SIN_TAYLOR = (-1.0 / 6.0, 1.0 / 120.0, -1.0 / 5040.0, 1.0 / 362880.0, -1.0 / 39916800.0, 1.0 / 6227020800.0)


def _sin_small(x):
    k = jnp.floor(x * (1.0 / math.pi) + 0.5)
    r = ((x - k * _PI_A) - k * _PI_B) - k * _PI_C
    r2 = r * r
    poly = _SIN_TAYLOR[-1]
    for coef in _SIN_TAYLOR[-2::-1]:
        poly = poly * r2 + coef
    s = r + r * r2 * poly
    return jnp.where((k.astype(jnp.int32) & 1) == 1, -s, s)


def _params(*sem):
    return pltpu.CompilerParams(dimension_semantics=sem, vmem_limit_bytes=VMEM_LIMIT)


def _in_proj_kernel(x_ref, xp_ref, xn_ref, g1_ref, w_ref, qg_ref, kg_ref, cos_ref, sin_ref,
                    bdq_ref, bdk_ref, cw_ref, cb_ref,
                    q_ref, k_ref, v_ref, hv_ref, x1_ref, x2_ref, bound_ref, a_scr, w_scr, *, tm, n_tiles):
    i = pl.program_id(1)

    @pl.when((pl.program_id(0) == 0) & (i == 0))
    def _():
        w_scr[...] = w_ref[...].astype(BF16)

    g1 = g1_ref[...]
    a_scr[HALO:HALO + tm, :] = _rms(x_ref[...], g1).astype(BF16)
    prev = jnp.where(i > 0, _rms(xp_ref[...], g1), 0.0)
    nxt = jnp.where(i < n_tiles - 1, _rms(xn_ref[...], g1), 0.0)
    a_scr[0:HALO, :] = prev.astype(BF16)
    a_scr[HALO + tm:, :] = nxt.astype(BF16)

    n_qkv = Q_WIDTH + 2 * KV_WIDTH
    qkv = _dot(a_scr[HALO:HALO + tm, :], w_scr[:, :n_qkv])
    cos = cos_ref[...]
    sin = sin_ref[...]
    lane = lax.broadcasted_iota(jnp.int32, (tm, LANES), 1)
    first16 = (lane % 32) < 16
    low_half = lane < HEAD_DIM

    def norm_rope(blk, ms, gain):
        y = blk * lax.rsqrt(ms + EPS) * gain
        partner = jnp.where(first16, pltpu.roll(y, LANES - 16, 1), pltpu.roll(y, 16, 1))
        return y * cos + partner * sin

    q = qkv[:, :Q_WIDTH]
    q_ms = _dot((q * q).astype(BF16), bdq_ref[...])
    qg = qg_ref[...]
    bound = (1.01 * HEAD_DIM ** 0.5 * LOG2E) * jnp.max(jnp.abs(qg)) * jnp.max(jnp.abs(kg_ref[...]))
    bound_ref[0, 0] = bound
    neg_bound = -bound
    extra = lane == HEAD_DIM
    for m in range(Q_WIDTH // LANES):
        sl = slice(m * LANES, (m + 1) * LANES)
        blk = norm_rope(q[:, sl], q_ms[:, sl], qg[:, sl]) * (HEAD_DIM ** -0.5 * LOG2E)
        q_ref[:, 2 * m * LANES:(2 * m + 1) * LANES] = jnp.where(extra, neg_bound, blk).astype(BF16)
        q_ref[:, (2 * m + 1) * LANES:(2 * m + 2) * LANES] = jnp.where(
            extra, neg_bound, pltpu.roll(blk, HEAD_DIM, 1)).astype(BF16)

    k = qkv[:, Q_WIDTH:Q_WIDTH + KV_WIDTH]
    k_ms = _dot((k * k).astype(BF16), bdk_ref[...])
    k = norm_rope(k, k_ms, kg_ref[...])
    v = qkv[:, Q_WIDTH + KV_WIDTH:]
    pad = jnp.where(extra, 1.0, 0.0)
    k_ref[0] = jnp.where(low_half, k, pad).astype(BF16)
    k_ref[1] = jnp.where(low_half, pltpu.roll(k, HEAD_DIM, 1), pad).astype(BF16)
    v_ref[0] = jnp.where(low_half, v, 1.0).astype(BF16)
    v_ref[1] = jnp.where(low_half, pltpu.roll(v, HEAD_DIM, 1), 1.0).astype(BF16)

    first_p = lax.broadcasted_iota(jnp.int32, (SUBLANES, HY_WIDTH), 0) == 0
    last_p = lax.broadcasted_iota(jnp.int32, (SUBLANES, HY_WIDTH), 0) == SUBLANES - 1
    for c, out in enumerate((hv_ref, x1_ref, x2_ref)):
        cs = slice(c * HY_WIDTH, (c + 1) * HY_WIDTH)
        u = _dot(a_scr[...], w_scr[:, n_qkv + c * HY_WIDTH:n_qkv + (c + 1) * HY_WIDTH])
        w = cw_ref[:, cs]
        ut = jnp.swapaxes(u[HALO:HALO + tm].reshape(SUBLANES, LANES, HY_WIDTH), 0, 1)
        before = jnp.where(first_p, u[HALO - 1:HALO], pltpu.roll(ut[LANES - 1], 1, 0))
        after = jnp.where(last_p, u[HALO + tm:HALO + tm + 1], pltpu.roll(ut[0], SUBLANES - 1, 0))
        prev = jnp.concatenate([before[None], ut[:-1]], axis=0)
        nxt = jnp.concatenate([ut[1:], after[None]], axis=0)
        y = (prev * w[0:1] + ut * w[1:2] + nxt * w[2:3]) + cb_ref[:, cs]
        out[...] = y.reshape(tm, HY_WIDTH).astype(BF16)


def _in_proj(x, g1, w, qg, kg, cos_t, sin_t, bdq, bdk, conv_w, conv_b):
    b, seq, d = x.shape
    tm = ROW_TILE
    n_tiles = seq // tm
    hb = tm // HALO
    full = lambda shape: pl.BlockSpec(shape, lambda bi, i: (0,) * len(shape))
    row_out = lambda w: pl.BlockSpec((None, tm, w), lambda bi, i: (bi, i, 0))
    kv_out = pl.BlockSpec((None, N_KV_HEADS, tm, LANES), lambda bi, i: (bi, 0, i, 0))
    tx_out = pl.BlockSpec((None, None, tm, HY_WIDTH), lambda bi, i: (bi, i, 0, 0))
    tx_shape = jax.ShapeDtypeStruct((b, n_tiles, tm, HY_WIDTH), BF16)
    return pl.pallas_call(
        functools.partial(_in_proj_kernel, tm=tm, n_tiles=n_tiles),
        grid=(b, n_tiles),
        in_specs=[
            pl.BlockSpec((None, tm, d), lambda bi, i: (bi, i, 0)),
            pl.BlockSpec((None, HALO, d), lambda bi, i: (bi, jnp.maximum(i * hb - 1, 0), 0)),
            pl.BlockSpec((None, HALO, d), lambda bi, i: (bi, jnp.minimum((i + 1) * hb, seq // HALO - 1), 0)),
            full((1, d)),
            pl.BlockSpec(w.shape, lambda bi, i: (0, 0), pipeline_mode=pl.Buffered(1)),
            full((1, Q_WIDTH)),
            full((1, KV_WIDTH)),
            pl.BlockSpec((tm, LANES), lambda bi, i: (i, 0)),
            pl.BlockSpec((tm, LANES), lambda bi, i: (i, 0)),
            full(bdq.shape),
            full(bdk.shape),
            full(conv_w.shape),
            full(conv_b.shape),
        ],
        out_specs=[row_out(2 * Q_WIDTH), kv_out, kv_out, tx_out, tx_out, tx_out,
                   pl.BlockSpec(memory_space=pltpu.SMEM)],
        out_shape=[
            jax.ShapeDtypeStruct((b, seq, 2 * Q_WIDTH), BF16),
            jax.ShapeDtypeStruct((b, N_KV_HEADS, seq, LANES), BF16),
            jax.ShapeDtypeStruct((b, N_KV_HEADS, seq, LANES), BF16),
            tx_shape, tx_shape, tx_shape,
            jax.ShapeDtypeStruct((1, 1), F32),
        ],
        scratch_shapes=[pltpu.VMEM((tm + 2 * HALO, d), BF16), pltpu.VMEM(w.shape, BF16)],
        compiler_params=_params("arbitrary", "arbitrary"),
        name="in_proj",
    )(x, x, x, g1, w, qg, kg, cos_t, sin_t, bdq, bdk, conv_w, conv_b)


def _stack_heads(q_ref):
    group = N_Q_HEADS // N_KV_HEADS
    return jnp.concatenate([q_ref[:, g * LANES:(g + 1) * LANES] for g in range(group)], axis=0)


def _store_heads(o_ref, o, tq):
    group = N_Q_HEADS // N_KV_HEADS
    pairs = [o[(2 * p) * tq:(2 * p + 1) * tq] + pltpu.roll(o[(2 * p + 1) * tq:(2 * p + 2) * tq], HEAD_DIM, 1)
             for p in range(group // 2)]
    o_ref[...] = jnp.concatenate(pairs, axis=1)


def _store_heads_normalised(o_ref, acc, tq):
    group = N_Q_HEADS // N_KV_HEADS
    low = lax.broadcasted_iota(jnp.int32, (tq, LANES), 1) < HEAD_DIM
    pairs = []
    for p in range(group // 2):
        even, odd = acc[(2 * p) * tq:(2 * p + 1) * tq], acc[(2 * p + 1) * tq:(2 * p + 2) * tq]
        num = jnp.where(low, even, pltpu.roll(odd, HEAD_DIM, 1))
        den = jnp.where(low, pltpu.roll(even, HEAD_DIM, 1), odd)
        pairs.append(num / den)
    o_ref[...] = jnp.concatenate(pairs, axis=1)


def _attn_bounded_kernel(q_ref, k_ref, v_ref, o_ref, *, tq, sc, n_chunks):
    q = _stack_heads(q_ref)
    p = jnp.concatenate(
        [jnp.exp2(lax.dot_general(q, k_ref[c * sc:(c + 1) * sc, :], (((1,), (1,)), ((), ())),
                                  preferred_element_type=F32)).astype(BF16) for c in range(n_chunks)], axis=1)
    _store_heads_normalised(o_ref, _dot(p, v_ref[...]), tq)


def _attn_online_kernel(q_ref, k_ref, v_ref, o_ref, *, tq, sc, n_chunks):
    q = _stack_heads(q_ref)
    rows = q.shape[0]

    def body(c, carry):
        m, l, acc = carry
        off = pl.multiple_of(c * sc, sc)
        s = lax.dot_general(q, k_ref[pl.ds(off, sc), :], (((1,), (1,)), ((), ())), preferred_element_type=F32)
        m_new = jnp.maximum(m, jnp.max(s, axis=1, keepdims=True))
        p = jnp.exp2(s - m_new)
        alpha = jnp.exp2(m - m_new)
        l = alpha * l + jnp.sum(p, axis=1, keepdims=True)
        acc = alpha * acc + _dot(p.astype(BF16), v_ref[pl.ds(off, sc), :])
        return m_new, l, acc

    init = (jnp.full((rows, 1), jnp.finfo(F32).min, F32), jnp.zeros((rows, 1), F32), jnp.zeros((rows, LANES), F32))
    _, l, acc = lax.fori_loop(0, n_chunks, body, init)
    lane = lax.broadcasted_iota(jnp.int32, (rows, LANES), 1)
    _store_heads(o_ref, jnp.where(lane < HEAD_DIM, acc / l, 0.0), tq)


def _attention(body, sc, name, q, k, v):
    b, seq, _ = q.shape
    tq = min(Q_TILE, seq)
    sc = min(sc, seq)
    group = N_Q_HEADS // N_KV_HEADS
    return pl.pallas_call(
        functools.partial(body, tq=tq, sc=sc, n_chunks=seq // sc),
        grid=(b, N_KV_HEADS, seq // tq),
        in_specs=[
            pl.BlockSpec((None, tq, group * LANES), lambda bi, j, i: (bi, i, j)),
            pl.BlockSpec((None, None, seq, LANES), lambda bi, j, i: (bi, j, 0, 0)),
            pl.BlockSpec((None, None, seq, LANES), lambda bi, j, i: (bi, j, 0, 0)),
        ],
        out_specs=pl.BlockSpec((None, tq, group * HEAD_DIM), lambda bi, j, i: (bi, i, j)),
        out_shape=jax.ShapeDtypeStruct((b, seq, Q_WIDTH), F32),
        compiler_params=_params("parallel", "parallel", "arbitrary"),
        name=name,
    )(q, k, v)


def _swap_store(o_ref, slabs, rows):
    t = jnp.swapaxes(jnp.stack(slabs), 0, 1).astype(BF16)
    o_ref[:, 0] = t[:rows]
    o_ref[:, 1] = t[rows:]


def _filter_hidden(z, w1, b1, w2, b2, w3, b3, fr, sin):
    x1 = fr * (_dot_f32(z, w1) + b1)
    x2 = fr * (_dot_f32(sin(x1), w2) + b2)
    x3 = fr * (_dot_f32(sin(x2), w3) + b3)
    biggest = jnp.maximum(jnp.max(jnp.abs(x1)), jnp.maximum(jnp.max(jnp.abs(x2)), jnp.max(jnp.abs(x3))))
    return sin(x3), biggest


def _filter_kernel(z_ref, w1_ref, b1_ref, w2_ref, b2_ref, w3_ref, b3_ref, w4_ref, fr_ref, dl_ref,
                   taps_ref, s_ref, *, nh):
    i = pl.program_id(0)
    z = z_ref[...]
    layers = (z, w1_ref[...], b1_ref[...], w2_ref[...], b2_ref[...], w3_ref[...], b3_ref[...], fr_ref[...])
    h, biggest = _filter_hidden(*layers, _sin_small)
    h = lax.cond(biggest <= _SIN_SMALL, lambda: h, lambda: _filter_hidden(*layers, jnp.sin)[0])
    h = _dot(h.astype(BF16), w4_ref[...].astype(BF16)) * jnp.exp(-z[:, 0:1] * jnp.abs(dl_ref[...]))
    first = (lax.broadcasted_iota(jnp.int32, (h.shape[0], HY_WIDTH), 0) == 0) & (i == 0)
    sums = []
    for od in range(4):
        hod = h[:, od * HY_WIDTH:(od + 1) * HY_WIDTH]
        if od % 2 == 1:
            hod = jnp.where(first, 0.0, hod)
        sums.append(jnp.sum(jnp.abs(hod), axis=0, keepdims=True))
        for c in range(HY_WIDTH // LANES):
            for g in range(h.shape[0] // nh):
                taps_ref[od // 2, od % 2, c, g] = hod[g * nh:(g + 1) * nh, c * LANES:(c + 1) * LANES].astype(BF16)
    ssum = jnp.concatenate(sums, axis=1)

    @pl.when(i == 0)
    def _():
        s_ref[...] = ssum

    @pl.when(i > 0)
    def _():
        s_ref[...] += ssum


def _filter_taps(zt, w1, b1, w2, b2, w3, b3, w4, freq, deltas):
    seq = zt.shape[0]
    nh = seq // LANES
    groups = FILT_TILE // nh
    cq = HY_WIDTH // LANES
    full = lambda a: pl.BlockSpec(a.shape, lambda i: (0,) * a.ndim)
    return pl.pallas_call(
        functools.partial(_filter_kernel, nh=nh),
        grid=(seq // FILT_TILE,),
        in_specs=[pl.BlockSpec((FILT_TILE, FILTER_PAD), lambda i: (i, 0))]
        + [full(a) for a in (w1, b1, w2, b2, w3, b3, w4, freq, deltas)],
        out_specs=[pl.BlockSpec((2, 2, cq, groups, nh, LANES), lambda i: (0, 0, 0, i, 0, 0)),
                   pl.BlockSpec((1, 4 * HY_WIDTH), lambda i: (0, 0))],
        out_shape=[jax.ShapeDtypeStruct((2, 2, cq, LANES, nh, LANES), BF16),
                   jax.ShapeDtypeStruct((1, 4 * HY_WIDTH), F32)],
        compiler_params=_params("arbitrary"),
        name="filter_taps",
    )(zt, w1, b1, w2, b2, w3, b3, w4, freq, deltas)


def _fft_outer_filter_kernel(fw_ref, bw_ref, bw0_ref, f_ref, f0_ref, a_ref, *, jb, n1):
    first_block = pl.program_id(1) == 0
    lanes = range(HY_WIDTH // LANES)
    slabs = []
    for j in range(jb):
        bw = [bw0_ref[c, 0] for c in lanes] if j == 0 else [bw_ref[c, jb - j] for c in lanes]
        x = jnp.concatenate([jnp.concatenate([fw_ref[c, j] for c in lanes], axis=1),
                             jnp.concatenate(bw, axis=1)], axis=0)
        mat = jnp.where(first_block, f0_ref[...], f_ref[...]) if j == 0 else f_ref[...]
        slabs.append(_dot(mat, x))
    _swap_store(a_ref, slabs, n1)


def _fft_outer_filter(taps, f1t, f1t0):
    n_ord, _, cq, _, nh, _ = taps.shape
    n1 = 2 * nh
    jb = SWAP
    jbn = LANES // jb
    tap_spec = lambda d, blk: pl.BlockSpec((None, None, cq, jb, nh, LANES), lambda o, j: (o, d, 0, blk(j), 0, 0))
    return pl.pallas_call(
        functools.partial(_fft_outer_filter_kernel, jb=jb, n1=n1),
        grid=(n_ord, jbn),
        in_specs=[tap_spec(0, lambda j: j), tap_spec(1, lambda j: jbn - 1 - j), tap_spec(1, lambda j: (jbn - j) % jbn),
                  pl.BlockSpec(f1t.shape, lambda o, j: (0, 0)), pl.BlockSpec(f1t0.shape, lambda o, j: (0, 0))],
        out_specs=pl.BlockSpec((None, n1, 2, jb, HY_WIDTH), lambda o, j: (o, 0, 0, j, 0)),
        out_shape=jax.ShapeDtypeStruct((n_ord, n1, 2, LANES, HY_WIDTH), BF16),
        compiler_params=_params("parallel", "parallel"),
        name="fft_outer_filter",
    )(taps, taps, taps, f1t, f1t0)


def _time_slabs(z_ref, jb):
    z = z_ref[...].astype(F32)
    two, na, _, width = z.shape
    z = z.reshape(two, na, jb, SUBLANES, width)
    return [z[:, :, j].reshape(two * na * SUBLANES, width) for j in range(jb)]


def _fft_outer_kernel(z_ref, f_ref, a_ref, *, jb, n1):
    _swap_store(a_ref, [_dot(f_ref[...], x.astype(BF16)) for x in _time_slabs(z_ref, jb)], n1)


def _fft_outer(z, f1):
    _, na, _, width = z.shape
    n1 = f1.shape[0] // 2
    jb = SWAP
    return pl.pallas_call(
        functools.partial(_fft_outer_kernel, jb=jb, n1=n1),
        grid=(LANES // jb,),
        in_specs=[pl.BlockSpec((2, na, jb * SUBLANES, width), lambda j: (0, 0, j, 0)),
                  pl.BlockSpec(f1.shape, lambda j: (0, 0))],
        out_specs=pl.BlockSpec((n1, 2, jb, width), lambda j: (0, 0, j, 0)),
        out_shape=jax.ShapeDtypeStruct((n1, 2, LANES, width), BF16),
        compiler_params=_params("parallel"),
        name="fft_outer",
    )(z, f1)


def _fft_inner_kernel(a_ref, af_ref, s_ref, gc_ref, gs_ref, b_ref, *, kb, n_fft):
    scale = 1.0 / ((s_ref[0] + s_ref[1]) * n_fft)
    gs = [jnp.concatenate([jnp.concatenate([gc_ref[kk], gs_ref[kk]], axis=1),
                           jnp.concatenate([-gs_ref[kk], gc_ref[kk]], axis=1)], axis=0) for kk in range(kb)]
    hs = [_dot(gs[kk], af_ref[kk].reshape(2 * LANES, HY_WIDTH)) for kk in range(kb)]
    cs = [_dot(gs[kk], a_ref[kk].reshape(2 * LANES, HY_WIDTH)) for kk in range(kb)]
    ps = []
    for h, c in zip(hs, cs):
        hr, hi = h[:LANES] * scale, h[LANES:] * scale
        cr, ci = c[:LANES], c[LANES:]
        ps.append(jnp.concatenate([cr * hr - ci * hi, cr * hi + ci * hr], axis=0).astype(BF16))
    _swap_store(b_ref, [lax.dot_general(gs[kk], p, (((0,), (0,)), ((), ())), preferred_element_type=F32)
                        for kk, p in enumerate(ps)], LANES)


def _fft_inner(a, af, s, gc, gs, order, n_fft):
    n1 = a.shape[0]
    kb = SWAP
    a_blk = (kb, 2, LANES, HY_WIDTH)
    return pl.pallas_call(
        functools.partial(_fft_inner_kernel, kb=kb, n_fft=n_fft),
        grid=(n1 // kb,),
        in_specs=[pl.BlockSpec(a_blk, lambda k: (k, 0, 0, 0)),
                  pl.BlockSpec((None,) + a_blk, lambda k: (order, k, 0, 0, 0)),
                  pl.BlockSpec((None, 2, 1, HY_WIDTH), lambda k: (order, 0, 0, 0)),
                  pl.BlockSpec((kb, LANES, LANES), lambda k: (k, 0, 0)),
                  pl.BlockSpec((kb, LANES, LANES), lambda k: (k, 0, 0))],
        out_specs=pl.BlockSpec((LANES, 2, kb, HY_WIDTH), lambda k: (0, 0, k, 0)),
        out_shape=jax.ShapeDtypeStruct((LANES, 2, n1, HY_WIDTH), BF16),
        compiler_params=_params("parallel"),
        name="fft_inner",
    )(a, af, s, gc, gs)


def _gated_slabs(b_ref, f_ref, z_ref, g_ref, skip_ref, jb, nh):
    skip = skip_ref[...]
    return [g * (_dot(f_ref[...], b_ref[j].reshape(4 * nh, b_ref.shape[-1])) + skip * z)
            for j, (z, g) in enumerate(zip(_time_slabs(z_ref, jb), _time_slabs(g_ref, jb)))]


def _gate_fft_outer_kernel(b_ref, fi_ref, f_ref, z_ref, g_ref, skip_ref, o_ref, a_ref, *, jb, nh):
    vals = _gated_slabs(b_ref, fi_ref, z_ref, g_ref, skip_ref, jb, nh)
    for b in range(2):
        for a in range(nh // SUBLANES):
            rows = slice(b * nh + a * SUBLANES, b * nh + (a + 1) * SUBLANES)
            o_ref[b, a] = jnp.concatenate([v[rows] for v in vals], axis=0).astype(BF16)
    _swap_store(a_ref, [_dot(f_ref[...], v.astype(BF16)) for v in vals], 2 * nh)


def _gate_fft_outer(bsp, f1inv, f1, z, gate, skip_row):
    _, na, _, width = z.shape
    nh = na * SUBLANES
    n1 = 2 * nh
    jb = SWAP
    lanes = DFT_LANES
    t_spec = pl.BlockSpec((2, na, jb * SUBLANES, lanes), lambda j, h: (0, 0, j, h))
    const = lambda a: pl.BlockSpec(a.shape, lambda j, h: (0, 0))
    return pl.pallas_call(
        functools.partial(_gate_fft_outer_kernel, jb=jb, nh=nh),
        grid=(LANES // jb, width // lanes),
        in_specs=[pl.BlockSpec((jb, 2, n1, lanes), lambda j, h: (j, 0, 0, h)), const(f1inv), const(f1),
                  t_spec, t_spec, pl.BlockSpec((1, lanes), lambda j, h: (0, h))],
        out_specs=[t_spec, pl.BlockSpec((n1, 2, jb, lanes), lambda j, h: (0, 0, j, h))],
        out_shape=[jax.ShapeDtypeStruct(z.shape, BF16), jax.ShapeDtypeStruct((n1, 2, LANES, width), BF16)],
        compiler_params=_params("parallel", "parallel"),
        name="gate_fft_outer",
    )(bsp, f1inv, f1, z, gate, skip_row)


def _ifft_outer_gate_kernel(b_ref, f_ref, z_ref, g_ref, skip_ref, o_ref, *, jb, nh):
    t = jnp.swapaxes(jnp.stack(_gated_slabs(b_ref, f_ref, z_ref, g_ref, skip_ref, jb, nh)), 0, 1).astype(BF16)
    o_ref[0] = t[:nh]
    o_ref[1] = t[nh:]


def _ifft_outer_gate(bsp, f1inv, z, gate, skip_row):
    _, na, _, width = z.shape
    nh = na * SUBLANES
    jb = SWAP
    t_spec = pl.BlockSpec((2, na, jb * SUBLANES, width), lambda j: (0, 0, j, 0))
    return pl.pallas_call(
        functools.partial(_ifft_outer_gate_kernel, jb=jb, nh=nh),
        grid=(LANES // jb,),
        in_specs=[pl.BlockSpec((jb,) + bsp.shape[1:], lambda j: (j, 0, 0, 0)),
                  pl.BlockSpec(f1inv.shape, lambda j: (0, 0)), t_spec, t_spec,
                  pl.BlockSpec((1, width), lambda j: (0, 0))],
        out_specs=pl.BlockSpec((2, nh, jb, width), lambda j: (0, 0, j, 0)),
        out_shape=jax.ShapeDtypeStruct((2, nh, LANES, width), BF16),
        compiler_params=_params("parallel"),
        name="ifft_outer_gate",
    )(bsp, f1inv, z, gate, skip_row)


def _dft_tables(seq):
    n = 2 * seq
    n1 = n // LANES
    nh = n1 // 2
    k1 = np.arange(n1)
    r = np.arange(nh)

    def trig(rows, cols, period):
        ang = ((rows[:, None] * cols[None, :]) % period) * (2.0 * math.pi / period)
        return np.cos(ang), np.sin(ang)

    fc, fs = trig(k1, r, n1)
    bc, bs = trig(k1, n1 - 1 - r, n1)
    zc, zs = trig(k1, (n1 - r) % n1, n1)
    zc, zs = zc * (r > 0), zs * (r > 0)
    block = lambda a, b, c, d: jnp.asarray(np.block([[a, b], [c, d]]), F32).astype(BF16)
    f1 = block(fc, fs, -fs, fc)
    f1t = block(fc, bc, -fs, -bs)
    f1t0 = block(fc, zc, -fs, -zs)
    f1inv = block(fc.T, -fs.T, fs.T, fc.T)
    n2 = np.arange(LANES)
    ca, sa = (jnp.asarray(t, F32)[:, None, :] for t in trig(k1, n2, n))
    cb, sb = (jnp.asarray(t, F32)[None] for t in trig(n2, n2, LANES))
    gc, gs = (ca * cb - sa * sb).astype(BF16), (sa * cb + ca * sb).astype(BF16)
    return f1, f1t, f1t0, f1inv, gc, gs


def _filter_features(seq):
    nh = seq // LANES
    pos = (np.arange(LANES)[:, None] + LANES * np.arange(nh)[None, :]).reshape(seq).astype(np.float64)
    bands = (FILTER_EMB - 1) // 2
    ang = (2.0 * math.pi * pos / seq)[:, None] * np.linspace(1e-4, bands - 1, bands)[None, :]
    z = np.concatenate([(pos / (seq - 1))[:, None], np.cos(ang), -np.sin(ang)], axis=-1)
    return jnp.asarray(np.pad(z, ((0, 0), (0, FILTER_PAD - FILTER_EMB))), F32)


def _hyena(hv, x1, x2, fw1, fb1, fw2, fb2, fw3, fb3, fw4, ffreq, fdeltas, skip_d):
    b, na, rows, width = hv.shape
    assert b == 2, "the two batch rows ride as the real and imaginary parts of one transform"
    seq = na * rows
    n = 2 * seq
    f1, f1t, f1t0, f1inv, gc, gs = _dft_tables(seq)

    w1p = jnp.pad(fw1, ((0, FILTER_PAD - FILTER_EMB), (0, 0)))
    taps, s = _filter_taps(_filter_features(seq), w1p, fb1[None], fw2, fb2[None], fw3, fb3[None], fw4,
                           ffreq[None], fdeltas[None])
    af = _fft_outer_filter(taps, f1t, f1t0)
    s = s.reshape(2, 2, 1, HY_WIDTH)

    bsp = _fft_inner(_fft_outer(hv, f1), af, s, gc, gs, 0, n)
    zz, a = _gate_fft_outer(bsp, f1inv, f1, hv, x1, skip_d[0][None])
    bsp = _fft_inner(a, af, s, gc, gs, 1, n)
    return _ifft_outer_gate(bsp, f1inv, zz, x2, skip_d[1][None]).reshape(b, seq, width)


def _out_mlp_kernel(x_ref, att_ref, hy_ref, ga_ref, gh_ref, wo_ref, g2_ref, w1_ref, w2_ref, gf_ref, o_ref):
    half = att_ref.shape[1]
    for r in range(x_ref.shape[0] // MLP_ROWS):
        rows = slice(r * MLP_ROWS, (r + 1) * MLP_ROWS)
        a = _rms(att_ref[rows, :], ga_ref[...]).astype(BF16)
        hyn = _rms(hy_ref[rows, :].astype(F32), gh_ref[...]).astype(BF16)
        h = x_ref[rows, :] + _dot(a, wo_ref[:half, :]) + _dot(hyn, wo_ref[half:, :])
        m = _rms(h, g2_ref[...]).astype(BF16)
        t = jnp.square(jnp.maximum(_dot(m, w1_ref[...]), 0.0)).astype(BF16)
        h = h + _dot(t, w2_ref[...])
        o_ref[rows, :] = _rms(h, gf_ref[...])


def _out_mlp(x, att, hy, ga, gh, w_out, g2, w1, w2, gfin):
    b, seq, d = x.shape
    tm = ROW_TILE
    row = lambda w: pl.BlockSpec((None, tm, w), lambda bi, i: (bi, i, 0))
    full = lambda a: pl.BlockSpec(a.shape, lambda bi, i: (0,) * a.ndim, pipeline_mode=pl.Buffered(1))
    return pl.pallas_call(
        _out_mlp_kernel,
        grid=(b, seq // tm),
        in_specs=[row(d), row(att.shape[-1]), row(hy.shape[-1]),
                  full(ga), full(gh), full(w_out), full(g2), full(w1), full(w2), full(gfin)],
        out_specs=row(d),
        out_shape=jax.ShapeDtypeStruct((b, seq, d), F32),
        compiler_params=_params("parallel", "parallel"),
        name="out_mlp",
    )(x, att, hy, ga, gh, w_out, g2, w1, w2, gfin)


def _rope_tables(seq):
    rows = seq // GRID_W
    row = np.repeat(np.arange(rows, dtype=np.float64), GRID_W)
    col = np.tile(np.arange(GRID_W, dtype=np.float64), rows)
    half = HEAD_DIM // 2
    inv_freq = ROPE_THETA ** (-np.arange(0, half, 2, dtype=np.float64) / half)
    ang_r = row[:, None] * inv_freq[None, :]
    ang_c = col[:, None] * inv_freq[None, :]
    cos = np.concatenate([np.cos(ang_r)] * 2 + [np.cos(ang_c)] * 2, axis=1)
    sin = np.concatenate([-np.sin(ang_r), np.sin(ang_r), -np.sin(ang_c), np.sin(ang_c)], axis=1)
    reps = (1, LANES // HEAD_DIM)
    return jnp.asarray(np.tile(cos, reps), F32), jnp.asarray(np.tile(sin, reps), F32)


def _head_mean_matrix(width):
    head = jnp.arange(width, dtype=jnp.int32) // HEAD_DIM
    return jnp.where(head[:, None] == head[None, :], 1.0 / HEAD_DIM, 0.0).astype(BF16)


def kernel(x, norm1_g, w_in, q_norm_g, k_norm_g, hy_conv_w, hy_conv_b, filt_w1, filt_b1, filt_w2, filt_b2, filt_w3, filt_b3, filt_w4, filt_freq, filt_deltas, hy_skip_d, attn_out_g, hy_out_g, w_out, norm2_g, w_mlp_in, w_mlp_out, final_g):
    seq = x.shape[1]
    cos_t, sin_t = _rope_tables(seq)
    bdq = _head_mean_matrix(Q_WIDTH)
    bdk = _head_mean_matrix(KV_WIDTH)
    h = x
    for i in range(norm1_g.shape[0]):
        q, k, v, hv, x1, x2, bound = _in_proj(
            h, norm1_g[i][None], w_in[i], jnp.tile(q_norm_g[i], N_Q_HEADS)[None],
            jnp.tile(k_norm_g[i], N_KV_HEADS)[None], cos_t, sin_t, bdq, bdk, hy_conv_w[i], hy_conv_b[i][None])
        att = lax.cond(
            bound[0, 0] <= MAX_FIXED_SHIFT,
            functools.partial(_attention, _attn_bounded_kernel, KV_CHUNK_BOUNDED, "attention_bounded"),
            functools.partial(_attention, _attn_online_kernel, KV_CHUNK, "attention_online"),
            q, k, v)
        hy = _hyena(hv, x1, x2, filt_w1[i], filt_b1[i], filt_w2[i], filt_b2[i], filt_w3[i], filt_b3[i],
                    filt_w4[i], filt_freq[i], filt_deltas[i], hy_skip_d[i])
        last = i == norm1_g.shape[0] - 1
        assert last, "single-layer trunk"
        h = _out_mlp(h, att, hy, attn_out_g[i][None], hy_out_g[i][None], w_out[i].astype(BF16), norm2_g[i][None],
                     w_mlp_in[i].astype(BF16), w_mlp_out[i].astype(BF16), final_g[None])
    return h
```
